```python
import jax
import jax.numpy as jnp
from jax import lax
import numpy as np

D_MODEL = 1024
BATCH = 2
SEQ = 8192
DEPTH = 2
DEC_BATCH = 32
DEC_SEQ = 8
PAST_LEN = 8192
PAGE_SIZE = 128

N_A_LAYERS = DEPTH // 2
N_B_LAYERS = DEPTH - N_A_LAYERS
MIX_WIDTH = D_MODEL
MEM_HEADS = 4
MEM_HEAD_DIM = 64
MEM_WIDTH = MEM_HEADS * MEM_HEAD_DIM
N_MEM = 256
MAIN_WIDTH = MIX_WIDTH - MEM_WIDTH
RET_HEADS = 6
RET_HEAD_DIM = MAIN_WIDTH // RET_HEADS
RET_CHUNK = 128
DIL_PAIRS = ((128, 1), (512, 4), (2048, 16))
GROUP_HEADS = 4
DIL_HEADS = GROUP_HEADS * len(DIL_PAIRS)
DIL_HEAD_DIM = MAIN_WIDTH // DIL_HEADS
FFN_HIDDEN = -((-8 * D_MODEL) // (3 * 256)) * 256
ROPE_THETA = 10000.0
LN_EPS = 1e-5
ALPHA = (2 * DEPTH) ** 0.25
BETA = (8 * DEPTH) ** -0.25

kernel_name = 'yoco_retention_dilated_attn_step'


def layer_norm(x, g, b):
    xf = x.astype(jnp.float32)
    mu = jnp.mean(xf, -1, keepdims=True)
    var = jnp.mean(jnp.square(xf - mu), -1, keepdims=True)
    y = (xf - mu) * lax.rsqrt(var + LN_EPS) * g.astype(jnp.float32) + b.astype(jnp.float32)
    return y.astype(x.dtype)


def deepnorm_residual(x, h, g, b):
    return layer_norm(ALPHA * x + h, g, b)


def swiglu_ffn(x, w_in, w_out):
    gate, up = jnp.split(x @ w_in, 2, axis=-1)
    return (jax.nn.silu(gate) * up) @ w_out


def rope(x, pos):
    d = x.shape[-1]
    inv = ROPE_THETA ** (-jnp.arange(0, d, 2, dtype=jnp.float32) / d)
    ang = pos[:, None] * inv[None, :]
    cos = jnp.cos(ang)[:, None, :]
    sin = jnp.sin(ang)[:, None, :]
    xf = x.astype(jnp.float32)
    x1, x2 = xf[..., : d // 2], xf[..., d // 2:]
    return jnp.concatenate([x1 * cos - x2 * sin, x2 * cos + x1 * sin], -1).astype(x.dtype)


def project_memory_kv(mem, w):
    b, n, _ = mem.shape
    return (mem @ w).reshape(b, n, 2, MEM_HEADS, MEM_HEAD_DIM)


def memory_attention(q, mem_kv):
    b, s = q.shape[0], q.shape[1]
    sc = jnp.einsum('bshd,bnhd->bhsn', q, mem_kv[:, :, 0]).astype(jnp.float32) * MEM_HEAD_DIM ** -0.5
    p = jax.nn.softmax(sc, axis=-1).astype(q.dtype)
    o = jnp.einsum('bhsn,bnhd->bshd', p, mem_kv[:, :, 1])
    return o.reshape(b, s, MEM_WIDTH)


def retention_log_decay():
    return jnp.log1p(-jnp.exp2(-5.0 - jnp.arange(RET_HEADS, dtype=jnp.float32)))


def retention_inputs(x, w_in, pos):
    b, s, _ = x.shape
    q, k, v, gate, qm = jnp.split(x @ w_in, [MAIN_WIDTH, 2 * MAIN_WIDTH, 3 * MAIN_WIDTH, 4 * MAIN_WIDTH], axis=-1)
    heads = lambda t: t.reshape(b, s, RET_HEADS, RET_HEAD_DIM)
    q = rope(heads(q), pos)
    k = rope(heads(k), pos) * RET_HEAD_DIM ** -0.5
    tr = lambda t: t.astype(jnp.float32).transpose(0, 2, 1, 3)
    return tr(q), tr(k), tr(heads(v)), gate, qm.reshape(b, s, MEM_HEADS, MEM_HEAD_DIM)


def retention_chunk(q, k, v, state, log_g):
    c = q.shape[2]
    i = jnp.arange(c, dtype=jnp.float32)
    diff = i[:, None] - i[None, :]
    lg = log_g[:, None, None]
    decay = jnp.where(diff >= 0, jnp.exp(jnp.maximum(diff, 0.0) * lg), 0.0)
    inner = jnp.einsum('bhqk,bhkv->bhqv', jnp.einsum('bhqd,bhkd->bhqk', q, k) * decay, v)
    cross = jnp.einsum('bhqd,bhdv->bhqv', q, state) * jnp.exp((i[None, :, None] + 1.0) * lg)
    k_dec = k * jnp.exp((c - 1.0 - i)[None, :, None] * lg)
    new_state = jnp.exp(c * lg) * state + jnp.einsum('bhkd,bhkv->bhdv', k_dec, v)
    return inner + cross, new_state


def retention_prompt(q, k, v, log_g):
    b, h, s, dk = q.shape
    dv = v.shape[-1]
    nc = s // RET_CHUNK
    blocks = lambda t: t.reshape(b, h, nc, RET_CHUNK, t.shape[-1]).transpose(2, 0, 1, 3, 4)

    def step(state, qkv):
        qc, kc, vc = qkv
        o, state = retention_chunk(qc, kc, vc, state, log_g)
        return state, o

    state0 = jnp.zeros((b, h, dk, dv), jnp.float32)
    state, o = lax.scan(step, state0, (blocks(q), blocks(k), blocks(v)))
    return o.transpose(1, 2, 0, 3, 4).reshape(b, h, s, dv), state


def retention_output(o, gate):
    b, h, s, dv = o.shape
    mu = jnp.mean(o, -1, keepdims=True)
    var = jnp.mean(jnp.square(o - mu), -1, keepdims=True)
    on = ((o - mu) * lax.rsqrt(var + LN_EPS)).transpose(0, 2, 1, 3).reshape(b, s, h * dv)
    return jax.nn.silu(gate) * on.astype(gate.dtype)


def dilated_inputs(x, w_in, pos):
    b, s, _ = x.shape
    q, qm = jnp.split(x @ w_in, [MAIN_WIDTH], axis=-1)
    q = rope(q.reshape(b, s, DIL_HEADS, DIL_HEAD_DIM), pos)
    return q, qm.reshape(b, s, MEM_HEADS, MEM_HEAD_DIM)


def shared_kv(x, w_kv, pos):
    b, s, _ = x.shape
    kv = (x @ w_kv).reshape(b, s, 2, DIL_HEADS, DIL_HEAD_DIM)
    return jnp.stack([rope(kv[:, :, 0], pos), kv[:, :, 1]], axis=2)


def dilated_prompt(q, k, v, window, dil):
    b, s, hg, d = q.shape
    blk = window // dil
    m = s // dil
    nb = -(-m // blk)
    mp = nb * blk

    def to_blocks(t):
        t = t.reshape(b, m, dil, hg, d).transpose(0, 2, 1, 3, 4)
        t = jnp.pad(t, ((0, 0), (0, 0), (0, mp - m), (0, 0), (0, 0)))
        return t.reshape(b, dil, nb, blk, hg, d)

    def with_prev(t):
        prev = jnp.concatenate([jnp.zeros_like(t[:, :, :1]), t[:, :, :-1]], axis=2)
        return jnp.concatenate([prev, t], axis=3)

    qb = to_blocks(q)
    kk = with_prev(to_blocks(k))
    vv = with_prev(to_blocks(v))
    sc = jnp.einsum('brcqhd,brckhd->brchqk', qb, kk).astype(jnp.float32) * d ** -0.5
    qi = jnp.arange(blk)[:, None]
    kj = jnp.arange(2 * blk)[None, :]
    delta = qi + blk - kj
    band = (delta >= 0) & (delta <= blk)
    valid = band[None] & ((jnp.arange(nb)[:, None, None] > 0) | (kj >= blk)[None])
    sc = jnp.where(valid[:, None], sc, -jnp.inf)
    lse = jax.nn.logsumexp(sc, axis=-1)
    p = jnp.exp(sc - lse[..., None]).astype(v.dtype)
    o = jnp.einsum('brchqk,brckhd->brcqhd', p, vv)

    def from_blocks(t):
        t = t.reshape((b, dil, mp) + t.shape[4:])[:, :, :m]
        t = jnp.swapaxes(t, 1, 2)
        return t.reshape((b, s) + t.shape[3:])

    return from_blocks(o), from_blocks(lse.transpose(0, 1, 2, 4, 3))


def dilated_sample(q, kv_all, window, dil, n_buf):
    t = q.shape[1]
    d = q.shape[-1]
    n = window // dil + 1
    idx = n_buf + jnp.arange(t)[:, None] - dil * jnp.arange(n)[None, :]
    valid = idx >= 0
    kg = kv_all[:, jnp.maximum(idx, 0)]
    sc = jnp.einsum('bthd,btjhd->bhtj', q, kg[:, :, :, 0]).astype(jnp.float32) * d ** -0.5
    sc = jnp.where(valid[None, None], sc, -jnp.inf)
    lse = jax.nn.logsumexp(sc, axis=-1)
    p = jnp.exp(sc - lse[..., None]).astype(q.dtype)
    o = jnp.einsum('bhtj,btjhd->bthd', p, kg[:, :, :, 1])
    return o, lse.transpose(0, 2, 1)


def combine_groups(outs, lses):
    o = jnp.stack(outs, axis=2)
    w = jax.nn.softmax(jnp.stack(lses, axis=2), axis=2)
    o = o * w[..., None].astype(o.dtype)
    return o.reshape(o.shape[0], o.shape[1], -1)


def setup_inputs(seed: int = 0) -> dict:
    key = jax.random.key(seed)
    ks = jax.random.split(key, 20)
    nrm = lambda k, shape, scale=1.0: scale * jax.random.normal(k, shape, jnp.float32)
    d = D_MODEL
    return {
        'x_prompt': nrm(ks[0], (BATCH, SEQ, d)),
        'x_sample': nrm(ks[1], (DEC_BATCH, DEC_SEQ, d)),
        'mem_prompt': nrm(ks[2], (BATCH, N_MEM, d)),
        'cache_mem_kv': nrm(ks[3], (DEPTH, DEC_BATCH, N_MEM, 2, MEM_HEADS, MEM_HEAD_DIM)),
        'state_ret': nrm(ks[4], (N_A_LAYERS, DEC_BATCH, RET_HEADS, RET_HEAD_DIM, RET_HEAD_DIM)),
        'cache_win_kv_g1': nrm(ks[5], (DEC_BATCH, min(DIL_PAIRS[0][0], PAST_LEN), 2, GROUP_HEADS, DIL_HEAD_DIM)),
        'cache_win_kv_g2': nrm(ks[6], (DEC_BATCH, min(DIL_PAIRS[1][0], PAST_LEN), 2, GROUP_HEADS, DIL_HEAD_DIM)),
        'cache_win_kv_g3': nrm(ks[7], (DEC_BATCH, min(DIL_PAIRS[2][0], PAST_LEN), 2, GROUP_HEADS, DIL_HEAD_DIM)),
        'w_in_a': nrm(ks[8], (N_A_LAYERS, d, 4 * MAIN_WIDTH + MEM_WIDTH), d ** -0.5),
        'w_in_b': nrm(ks[9], (N_B_LAYERS, d, MAIN_WIDTH + MEM_WIDTH), d ** -0.5),
        'w_out': nrm(ks[10], (DEPTH, MIX_WIDTH, d), BETA * MIX_WIDTH ** -0.5),
        'w_kv_shared': nrm(ks[11], (d, 2 * MAIN_WIDTH), d ** -0.5),
        'w_mem_kv': nrm(ks[12], (DEPTH, d, 2 * MEM_WIDTH), d ** -0.5),
        'ln_mix_g': 1.0 + nrm(ks[13], (DEPTH, d), 0.02),
        'ln_mix_b': nrm(ks[14], (DEPTH, d), 0.02),
        'ln_ffn_g': 1.0 + nrm(ks[15], (DEPTH, d), 0.02),
        'ln_ffn_b': nrm(ks[16], (DEPTH, d), 0.02),
        'w_ffn_in': nrm(ks[17], (DEPTH, d, 2 * FFN_HIDDEN), d ** -0.5),
        'w_ffn_out': nrm(ks[18], (DEPTH, FFN_HIDDEN, d), BETA * FFN_HIDDEN ** -0.5),
    }


def reference(x_prompt, x_sample, mem_prompt, cache_mem_kv, state_ret, cache_win_kv_g1,
              cache_win_kv_g2, cache_win_kv_g3, w_in_a, w_in_b, w_out, w_kv_shared, w_mem_kv,
              ln_mix_g, ln_mix_b, ln_ffn_g, ln_ffn_b, w_ffn_in, w_ffn_out):
    s = x_prompt.shape[1]
    t = x_sample.shape[1]
    pos_p = jnp.arange(s, dtype=jnp.float32)
    pos_s = PAST_LEN + jnp.arange(t, dtype=jnp.float32)
    log_g = retention_log_decay()
    win_caches = (cache_win_kv_g1, cache_win_kv_g2, cache_win_kv_g3)
    xp, xs = x_prompt, x_sample
    ret_p, ret_s, mem_kv_p = [], [], []
    for l in range(DEPTH):
        mkv_p = project_memory_kv(mem_prompt, w_mem_kv[l])
        mkv_s = cache_mem_kv[l]
        mem_kv_p.append(mkv_p)
        if l < N_A_LAYERS:
            q, k, v, gate, qm = retention_inputs(xp, w_in_a[l], pos_p)
            o, st = retention_prompt(q, k, v, log_g)
            mix_p = jnp.concatenate([retention_output(o, gate), memory_attention(qm, mkv_p)], -1)
            ret_p.append(st.astype(xp.dtype))
            q, k, v, gate, qm = retention_inputs(xs, w_in_a[l], pos_s)
            o, st = retention_chunk(q, k, v, state_ret[l].astype(jnp.float32), log_g)
            mix_s = jnp.concatenate([retention_output(o, gate), memory_attention(qm, mkv_s)], -1)
            ret_s.append(st.astype(state_ret.dtype))
        else:
            if l == N_A_LAYERS:
                kv_p = shared_kv(xp, w_kv_shared, pos_p)
                kv_s = shared_kv(xs, w_kv_shared, pos_s)
                win_p, win_s, kv_all_s = [], [], []
                for g, (window, _) in enumerate(DIL_PAIRS):
                    hs = slice(g * GROUP_HEADS, (g + 1) * GROUP_HEADS)
                    win_p.append(kv_p[:, s - min(window, s):, :, hs])
                    kva = jnp.concatenate([win_caches[g], kv_s[:, :, :, hs]], axis=1)
                    kv_all_s.append(kva)
                    win_s.append(kva[:, kva.shape[1] - min(window, kva.shape[1]):])
            bl = l - N_A_LAYERS
            q_p, qm_p = dilated_inputs(xp, w_in_b[bl], pos_p)
            q_s, qm_s = dilated_inputs(xs, w_in_b[bl], pos_s)
            outs_p, lses_p, outs_s, lses_s = [], [], [], []
            for g, (window, dil) in enumerate(DIL_PAIRS):
                hs = slice(g * GROUP_HEADS, (g + 1) * GROUP_HEADS)
                o, lse = dilated_prompt(q_p[:, :, hs], kv_p[:, :, 0, hs], kv_p[:, :, 1, hs], window, dil)
                outs_p.append(o)
                lses_p.append(lse)
                o, lse = dilated_sample(q_s[:, :, hs], kv_all_s[g], window, dil, win_caches[g].shape[1])
                outs_s.append(o)
                lses_s.append(lse)
            mix_p = jnp.concatenate([combine_groups(outs_p, lses_p), memory_attention(qm_p, mkv_p)], -1)
            mix_s = jnp.concatenate([combine_groups(outs_s, lses_s), memory_attention(qm_s, mkv_s)], -1)
        xp = deepnorm_residual(xp, mix_p @ w_out[l], ln_mix_g[l], ln_mix_b[l])
        xs = deepnorm_residual(xs, mix_s @ w_out[l], ln_mix_g[l], ln_mix_b[l])
        xp = deepnorm_residual(xp, swiglu_ffn(xp, w_ffn_in[l], w_ffn_out[l]), ln_ffn_g[l], ln_ffn_b[l])
        xs = deepnorm_residual(xs, swiglu_ffn(xs, w_ffn_in[l], w_ffn_out[l]), ln_ffn_g[l], ln_ffn_b[l])
    return (xp, xs, jnp.stack(ret_p), jnp.stack(ret_s), jnp.stack(mem_kv_p),
            win_p[0], win_p[1], win_p[2], win_s[0], win_s[1], win_s[2])
```

```python
import functools
import math

import jax
import jax.numpy as jnp
from jax import lax
from jax.experimental import pallas as pl
from jax.experimental.pallas import tpu as pltpu

F32 = jnp.float32
BF16 = jnp.bfloat16

D_MODEL = 1024
MEM_HEADS = 4
MEM_HEAD_DIM = 64
MEM_WIDTH = MEM_HEADS * MEM_HEAD_DIM
MAIN_WIDTH = D_MODEL - MEM_WIDTH
RET_HEADS = 6
RET_HEAD_DIM = MAIN_WIDTH // RET_HEADS
RET_CHUNK = 128
DIL_PAIRS = ((128, 1), (512, 4), (2048, 16))
GROUP_HEADS = 4
DIL_HEAD_DIM = 64
GROUP_WIDTH = GROUP_HEADS * DIL_HEAD_DIM
DIL_BLOCK = 128
FFN_HIDDEN = 2816
ROPE_THETA = 10000.0
LN_EPS = 1e-5
DEPTH = 2
ALPHA = (2 * DEPTH) ** 0.25
PAST_LEN = 8192
NEG_BIG = -1e30

LANES = 128
VMEM_LIMIT = 56 * 1024 * 1024
FFN_COL_CHUNK = 256

LOG_G = tuple(math.log1p(-(2.0 ** (-5.0 - h))) for h in range(RET_HEADS))


def _dot(a, b):
    return jnp.dot(a, b, preferred_element_type=F32)


def _dot_nt(a, b):
    return lax.dot_general(a, b, (((1,), (1,)), ((), ())), preferred_element_type=F32)


def _dot_tn(a, b):
    return lax.dot_general(a, b, (((0,), (0,)), ((), ())), preferred_element_type=F32)


def _silu(x):
    return x / (1.0 + jnp.exp(-x))


def _layer_norm(z, g, b):
    mu = jnp.mean(z, axis=-1, keepdims=True)
    zc = z - mu
    var = jnp.mean(zc * zc, axis=-1, keepdims=True)
    return zc * lax.rsqrt(var + LN_EPS) * g + b


def _rope_lanes(y, cos, sin_signed, half):
    if 2 * half == LANES:
        partner = pltpu.roll(y, half, 1)
    else:
        lane = lax.broadcasted_iota(jnp.int32, y.shape, 1)
        first = (lane & (2 * half - 1)) < half
        partner = jnp.where(first, pltpu.roll(y, LANES - half, 1), pltpu.roll(y, half, 1))
    return y * cos + partner * sin_signed


def _rope_cols(y, cos, sin_signed, half):
    parts = [_rope_lanes(y[:, j:j + LANES], cos, sin_signed, half) for j in range(0, y.shape[1], LANES)]
    return parts[0] if len(parts) == 1 else jnp.concatenate(parts, axis=1)


def _head_mask(shape, h, width):
    lane = lax.broadcasted_iota(jnp.int32, shape, len(shape) - 1)
    return (lane >= h * width) & (lane < (h + 1) * width)


def _const_spec(shape):
    n = len(shape)
    return pl.BlockSpec(shape, lambda *_: (0,) * n, pipeline_mode=pl.Buffered(1))


def _params(sem):
    return pltpu.CompilerParams(dimension_semantics=sem, vmem_limit_bytes=VMEM_LIMIT)


def _mem_proj_kernel(m_ref, w_ref, o_ref):
    o_ref[0] = _dot(m_ref[...].astype(BF16), w_ref[0])


def _mem_proj(mem2d, w_bf):
    rows = mem2d.shape[0]
    depth, d, n = w_bf.shape
    return pl.pallas_call(
        _mem_proj_kernel,
        grid=(depth,),
        in_specs=[pl.BlockSpec((rows, d), lambda l: (0, 0)),
                  pl.BlockSpec((1, d, n), lambda l: (l, 0, 0))],
        out_specs=pl.BlockSpec((1, rows, n), lambda l: (l, 0, 0)),
        out_shape=jax.ShapeDtypeStruct((depth, rows, n), F32),
        compiler_params=_params(("arbitrary",)),
        name="mem_proj",
    )(mem2d, w_bf)


def _in_a_kernel(x_ref, w_ref, cos_ref, sin_ref, q_ref, k_ref, v_ref, g_ref, qm_ref):
    xb = x_ref[...].astype(BF16)
    cos = cos_ref[...]
    sin = sin_ref[...]
    mw = MAIN_WIDTH
    half = RET_HEAD_DIM // 2
    q = _rope_cols(_dot(xb, w_ref[:, 0:mw]), cos, sin, half)
    q_ref[...] = q.astype(q_ref.dtype)
    k = _rope_cols(_dot(xb, w_ref[:, mw:2 * mw]), cos, sin, half) * (RET_HEAD_DIM ** -0.5)
    k_ref[...] = k.astype(k_ref.dtype)
    v_ref[...] = _dot(xb, w_ref[:, 2 * mw:3 * mw]).astype(v_ref.dtype)
    g_ref[...] = _dot(xb, w_ref[:, 3 * mw:4 * mw]).astype(g_ref.dtype)
    qm_ref[...] = _dot(xb, w_ref[:, 4 * mw:4 * mw + MEM_WIDTH]).astype(qm_ref.dtype)


def _in_a(x2d, w_bf, cos, sin, tm, out_dtype):
    t, d = x2d.shape
    n = w_bf.shape[1]
    ntab = cos.shape[0] // tm
    row = lambda i: (i, 0)
    tab = lambda i: (i % ntab, 0)
    widths = (MAIN_WIDTH,) * 4 + (MEM_WIDTH,)
    return pl.pallas_call(
        _in_a_kernel,
        grid=(t // tm,),
        in_specs=[pl.BlockSpec((tm, d), row), _const_spec((d, n)),
                  pl.BlockSpec((tm, LANES), tab), pl.BlockSpec((tm, LANES), tab)],
        out_specs=[pl.BlockSpec((tm, w), row) for w in widths],
        out_shape=[jax.ShapeDtypeStruct((t, w), out_dtype) for w in widths],
        compiler_params=_params(("parallel",)),
        name="in_proj_a",
    )(x2d, w_bf, cos, sin)


def _mem_attn_tile(qm, kv):
    kb = kv[:, :MEM_WIDTH].astype(BF16)
    vb = kv[:, MEM_WIDTH:].astype(BF16)
    zero = jnp.zeros((), BF16)
    out = None
    for h in range(MEM_HEADS):
        qh = jnp.where(_head_mask(qm.shape, h, MEM_HEAD_DIM), qm, zero)
        s = _dot_nt(qh, kb) * (MEM_HEAD_DIM ** -0.5)
        m = jnp.max(s, axis=-1, keepdims=True)
        e = jnp.exp(s - m)
        p = (e / jnp.sum(e, axis=-1, keepdims=True)).astype(BF16)
        vh = jnp.where(_head_mask(vb.shape, h, MEM_HEAD_DIM), vb, zero)
        o = _dot(p, vh)
        out = o if out is None else out + o
    return out


def _mem_attn_kernel(q_ref, kv_ref, o_ref):
    o_ref[...] = _mem_attn_tile(q_ref[...], kv_ref[0]).astype(o_ref.dtype)


def _mem_attn(qm, memkv, tq, rows_per_batch):
    t = qm.shape[0]
    per = rows_per_batch // tq
    return pl.pallas_call(
        _mem_attn_kernel,
        grid=(t // tq,),
        in_specs=[pl.BlockSpec((tq, MEM_WIDTH), lambda i: (i, 0)),
                  pl.BlockSpec((1,) + memkv.shape[1:], lambda i: (i // per, 0, 0))],
        out_specs=pl.BlockSpec((tq, MEM_WIDTH), lambda i: (i, 0)),
        out_shape=jax.ShapeDtypeStruct((t, MEM_WIDTH), BF16),
        compiler_params=_params(("parallel",)),
        name="mem_attn",
    )(qm, memkv)


def _retention_kernel(q_ref, k_ref, v_ref, g_ref, qm_ref, mkv_ref, mix_ref, st_ref,
                      dec_ref, rdec_ref, kdec_ref):
    c = pl.program_id(1)
    cs = RET_CHUNK

    @pl.when(c == 0)
    def _():
        st_ref[...] = jnp.zeros_like(st_ref)
        row = lax.broadcasted_iota(jnp.int32, (cs, cs), 0).astype(F32)
        col = lax.broadcasted_iota(jnp.int32, (cs, cs), 1).astype(F32)
        diff = row - col
        for h in range(RET_HEADS):
            lg = LOG_G[h]
            dec_ref[h] = jnp.where(diff >= 0, jnp.exp(jnp.maximum(diff, 0.0) * lg), 0.0)
            rdec_ref[h] = jnp.exp((row + 1.0) * lg)
            kdec_ref[h] = jnp.exp((cs - 1.0 - row) * lg)

    for h in range(RET_HEADS):
        hs = slice(h * RET_HEAD_DIM, (h + 1) * RET_HEAD_DIM)
        qh = q_ref[:, hs]
        kh = k_ref[:, hs]
        vh = v_ref[:, hs]
        st = st_ref[0, h]
        s = _dot_nt(qh, kh) * dec_ref[h]
        inner = _dot(s.astype(BF16), vh)
        cross = _dot(qh, st.astype(BF16)) * rdec_ref[h]
        kd = (kh.astype(F32) * kdec_ref[h]).astype(BF16)
        st_ref[0, h] = math.exp(cs * LOG_G[h]) * st + _dot_tn(kd, vh)
        o = inner + cross
        mu = jnp.mean(o, axis=-1, keepdims=True)
        oc = o - mu
        var = jnp.mean(oc * oc, axis=-1, keepdims=True)
        on = oc * lax.rsqrt(var + LN_EPS)
        mix_ref[:, hs] = (_silu(g_ref[:, hs].astype(F32)) * on).astype(mix_ref.dtype)

    mix_ref[:, MAIN_WIDTH:] = _mem_attn_tile(qm_ref[...], mkv_ref[0]).astype(mix_ref.dtype)


def _retention(q, k, v, g, qm, memkv, batch, seq):
    t = q.shape[0]
    nc = seq // RET_CHUNK
    tok = lambda b, c: (b * nc + c, 0)
    sq = (RET_HEADS, RET_CHUNK, RET_CHUNK)
    return pl.pallas_call(
        _retention_kernel,
        grid=(batch, nc),
        in_specs=[pl.BlockSpec((RET_CHUNK, MAIN_WIDTH), tok)] * 4
                 + [pl.BlockSpec((RET_CHUNK, MEM_WIDTH), tok),
                    pl.BlockSpec((1,) + memkv.shape[1:], lambda b, c: (b, 0, 0))],
        out_specs=[pl.BlockSpec((RET_CHUNK, D_MODEL), tok),
                   pl.BlockSpec((1, RET_HEADS, RET_HEAD_DIM, RET_HEAD_DIM), lambda b, c: (b, 0, 0, 0))],
        out_shape=[jax.ShapeDtypeStruct((t, D_MODEL), BF16),
                   jax.ShapeDtypeStruct((batch, RET_HEADS, RET_HEAD_DIM, RET_HEAD_DIM), F32)],
        scratch_shapes=[pltpu.VMEM(sq, F32), pltpu.VMEM(sq, F32), pltpu.VMEM(sq, F32)],
        compiler_params=_params(("arbitrary", "arbitrary")),
        name="retention",
    )(q, k, v, g, qm, memkv)


def _retention_sample_kernel(q_ref, k_ref, v_ref, g_ref, qm_ref, mkv_ref, st_ref, mix_ref, nst_ref):
    t = q_ref.shape[0]
    pad = jnp.zeros((LANES - t, RET_HEAD_DIM), F32)
    row = lax.broadcasted_iota(jnp.int32, (t, LANES), 0).astype(F32)
    col = lax.broadcasted_iota(jnp.int32, (t, LANES), 1).astype(F32)
    prow = lax.broadcasted_iota(jnp.int32, (LANES, RET_HEAD_DIM), 0).astype(F32)
    diff = row - col
    for h in range(RET_HEADS):
        lg = LOG_G[h]
        hs = slice(h * RET_HEAD_DIM, (h + 1) * RET_HEAD_DIM)
        qh = q_ref[:, hs]
        kp = jnp.concatenate([k_ref[:, hs], pad], axis=0)
        vp = jnp.concatenate([v_ref[:, hs], pad], axis=0)
        st = st_ref[0, h]
        dec = jnp.where(diff >= 0, jnp.exp(jnp.maximum(diff, 0.0) * lg), 0.0)
        inner = _dot(_dot_nt(qh, kp) * dec, vp)
        cross = _dot(qh, st) * jnp.exp((row + 1.0) * lg)
        kd = kp * jnp.exp((t - 1.0 - prow) * lg)
        nst_ref[0, h] = math.exp(t * lg) * st + _dot_tn(kd, vp)
        o = inner + cross
        mu = jnp.mean(o, axis=-1, keepdims=True)
        oc = o - mu
        var = jnp.mean(oc * oc, axis=-1, keepdims=True)
        on = oc * lax.rsqrt(var + LN_EPS)
        mix_ref[:, hs] = _silu(g_ref[:, hs]) * on
    mix_ref[:, MAIN_WIDTH:] = _mem_attn_sample_tile(qm_ref[...], mkv_ref[0])


def _retention_sample(q, k, v, g, qm, memkv, state, batch, t):
    tok = lambda b: (b, 0)
    st_spec = pl.BlockSpec((1, RET_HEADS, RET_HEAD_DIM, RET_HEAD_DIM), lambda b: (b, 0, 0, 0))
    return pl.pallas_call(
        _retention_sample_kernel,
        grid=(batch,),
        in_specs=[pl.BlockSpec((t, MAIN_WIDTH), tok)] * 4
                 + [pl.BlockSpec((t, MEM_WIDTH), tok),
                    pl.BlockSpec((1,) + memkv.shape[1:], lambda b: (b, 0, 0)), st_spec],
        out_specs=[pl.BlockSpec((t, D_MODEL), tok), st_spec],
        out_shape=[jax.ShapeDtypeStruct((batch * t, D_MODEL), F32),
                   jax.ShapeDtypeStruct(state.shape, F32)],
        compiler_params=_params(("parallel",)),
        name="retention_sample",
    )(q, k, v, g, qm, memkv, state)


def _query_rows(q, width):
    t = q.shape[0]
    parts = [jnp.where(_head_mask(q.shape, h, width), q, 0.0) for h in range(q.shape[1] // width)]
    parts.append(jnp.zeros((LANES - len(parts) * t, q.shape[1]), F32))
    return jnp.concatenate(parts, axis=0).astype(BF16)


def _pick_heads(full, t, width):
    out = None
    for h in range(full.shape[1] // width):
        blk = full[h * t:(h + 1) * t, :]
        piece = jnp.where(_head_mask(blk.shape, h, width), blk, 0.0)
        out = piece if out is None else out + piece
    return out


def _mem_attn_sample_tile(qm, kv):
    t = qm.shape[0]
    qr = _query_rows(qm, MEM_HEAD_DIM)
    kb = kv[:, :MEM_WIDTH].astype(BF16)
    vb = kv[:, MEM_WIDTH:].astype(BF16)
    s = _dot_nt(kb, qr) * (MEM_HEAD_DIM ** -0.5)
    m = jnp.max(s, axis=0, keepdims=True)
    e = jnp.exp(s - m)
    p = (e / jnp.sum(e, axis=0, keepdims=True)).astype(BF16)
    return _pick_heads(_dot_tn(p, vb), t, MEM_HEAD_DIM)


def _mem_attn_sample_kernel(q_ref, kv_ref, o_ref):
    o_ref[...] = _mem_attn_sample_tile(q_ref[...], kv_ref[0])


def _mem_attn_sample(qm, memkv, batch, t):
    return pl.pallas_call(
        _mem_attn_sample_kernel,
        grid=(batch,),
        in_specs=[pl.BlockSpec((t, MEM_WIDTH), lambda b: (b, 0)),
                  pl.BlockSpec((1,) + memkv.shape[1:], lambda b: (b, 0, 0))],
        out_specs=pl.BlockSpec((t, MEM_WIDTH), lambda b: (b, 0)),
        out_shape=jax.ShapeDtypeStruct((batch * t, MEM_WIDTH), F32),
        compiler_params=_params(("parallel",)),
        name="mem_attn_sample",
    )(qm, memkv)


def _dilated_sample_kernel(q_ref, cache_ref, new_ref, o_ref, lse_ref, win_ref, *, window, dil):
    t = q_ref.shape[0]
    n_buf = cache_ref.shape[1]
    gw = GROUP_WIDTH
    qr = _query_rows(q_ref[...], DIL_HEAD_DIM)
    new = new_ref[...]
    cache = cache_ref[0]
    scale = DIL_HEAD_DIM ** -0.5

    def scores(kv_rows, n_rows, first_index):
        kb = kv_rows[:, :gw].astype(BF16)
        s = _dot_nt(kb, qr) * scale
        key = lax.broadcasted_iota(jnp.int32, (n_rows, LANES), 0) + first_index
        tok = lax.broadcasted_iota(jnp.int32, (n_rows, LANES), 1) & (t - 1)
        delta = n_buf + tok - key
        valid = (delta >= 0) & (delta <= window) & ((delta & (dil - 1)) == 0)
        return jnp.where(valid, s, NEG_BIG)

    newp = jnp.concatenate([new, jnp.zeros((LANES - t, new.shape[1]), F32)], axis=0)
    s_c = scores(cache, n_buf, 0)
    s_n = scores(newp, LANES, n_buf)
    pad_row = lax.broadcasted_iota(jnp.int32, (LANES, LANES), 0) >= t
    s_n = jnp.where(pad_row, NEG_BIG, s_n)
    m = jnp.maximum(jnp.max(s_c, axis=0, keepdims=True), jnp.max(s_n, axis=0, keepdims=True))
    e_c = jnp.exp(s_c - m)
    e_n = jnp.exp(s_n - m)
    l = jnp.sum(e_c, axis=0, keepdims=True) + jnp.sum(e_n, axis=0, keepdims=True)
    inv = 1.0 / l
    full = (_dot_tn((e_c * inv).astype(BF16), cache[:, gw:].astype(BF16))
            + _dot_tn((e_n * inv).astype(BF16), newp[:, gw:].astype(BF16)))
    o_ref[...] = _pick_heads(full, t, DIL_HEAD_DIM)
    lse_col = jnp.transpose(jnp.broadcast_to(m + jnp.log(l), (LANES, LANES)))
    lse_full = jnp.concatenate([lse_col, lse_col], axis=1)
    lse_ref[...] = _pick_heads(lse_full, t, DIL_HEAD_DIM)
    win_ref[0, 0:n_buf - t, :] = cache[t:, :]
    win_ref[0, n_buf - t:, :] = new


def _dilated_sample(q, cache2d, new_kv, window, dil, batch, t):
    n_buf = cache2d.shape[1]
    tok = lambda b: (b, 0)
    big = lambda b: (b, 0, 0)
    return pl.pallas_call(
        functools.partial(_dilated_sample_kernel, window=window, dil=dil),
        grid=(batch,),
        in_specs=[pl.BlockSpec((t, GROUP_WIDTH), tok),
                  pl.BlockSpec((1, n_buf, 2 * GROUP_WIDTH), big),
                  pl.BlockSpec((t, 2 * GROUP_WIDTH), tok)],
        out_specs=[pl.BlockSpec((t, GROUP_WIDTH), tok), pl.BlockSpec((t, GROUP_WIDTH), tok),
                   pl.BlockSpec((1, n_buf, 2 * GROUP_WIDTH), big)],
        out_shape=[jax.ShapeDtypeStruct((batch * t, GROUP_WIDTH), F32),
                   jax.ShapeDtypeStruct((batch * t, GROUP_WIDTH), F32),
                   jax.ShapeDtypeStruct(cache2d.shape, F32)],
        compiler_params=_params(("parallel",)),
        name=f"dilated_sample_w{window}",
    )(q, cache2d, new_kv)


def _post_kernel(*refs, combine):
    if combine:
        o_refs, l_refs, mem_ref = refs[0:3], refs[3:6], refs[6]
        rest = refs[7:]
    else:
        mix_ref = refs[0]
        rest = refs[1:]
    x_ref, wo_ref, g1_ref, b1_ref, wi_ref, w2_ref, g2_ref, b2_ref, out_ref, act_ref = rest

    if combine:
        ls = [r[...] for r in l_refs]
        m = jnp.maximum(jnp.maximum(ls[0], ls[1]), ls[2])
        es = [jnp.exp(v - m) for v in ls]
        inv = 1.0 / (es[0] + es[1] + es[2])
        parts = [(o_refs[g][...].astype(F32) * (es[g] * inv)).astype(BF16) for g in range(3)]
        parts.append(mem_ref[...].astype(BF16))
        mix = jnp.concatenate(parts, axis=1)
    else:
        mix = mix_ref[...].astype(BF16)

    x1 = _layer_norm(ALPHA * x_ref[...] + _dot(mix, wo_ref[...]), g1_ref[...], b1_ref[...])
    x1b = x1.astype(BF16)
    for c in range(0, FFN_HIDDEN, FFN_COL_CHUNK):
        gate = _dot(x1b, wi_ref[:, c:c + FFN_COL_CHUNK])
        up = _dot(x1b, wi_ref[:, FFN_HIDDEN + c:FFN_HIDDEN + c + FFN_COL_CHUNK])
        act_ref[:, c:c + FFN_COL_CHUNK] = (_silu(gate) * up).astype(BF16)
    y = _dot(act_ref[...], w2_ref[...])
    out_ref[...] = _layer_norm(ALPHA * x1 + y, g2_ref[...], b2_ref[...])


def _post(mix_inputs, x2d, wo, g1, b1, wi, w2, g2, b2, tm, combine):
    t, d = x2d.shape
    row = lambda i: (i, 0)
    mix_specs = [pl.BlockSpec((tm, a.shape[1]), row) for a in mix_inputs]
    vec = _const_spec((1, d))
    return pl.pallas_call(
        functools.partial(_post_kernel, combine=combine),
        grid=(t // tm,),
        in_specs=mix_specs + [pl.BlockSpec((tm, d), row), _const_spec(wo.shape), vec, vec,
                              _const_spec(wi.shape), _const_spec(w2.shape), vec, vec],
        out_specs=pl.BlockSpec((tm, d), row),
        out_shape=jax.ShapeDtypeStruct((t, d), F32),
        scratch_shapes=[pltpu.VMEM((tm, FFN_HIDDEN), BF16)],
        compiler_params=_params(("parallel",)),
        name="post_combine" if combine else "post",
    )(*mix_inputs, x2d, wo, g1, b1, wi, w2, g2, b2)


def _in_b_kernel(x_ref, w_ref, cos_ref, sin_ref, *out_refs, n_win):
    xb = x_ref[...].astype(BF16)
    cos = cos_ref[...]
    sin = sin_ref[...]
    gw = GROUP_WIDTH
    half = DIL_HEAD_DIM // 2
    q_refs, kv_refs, qm_ref = out_refs[0:3], out_refs[3:6], out_refs[6]
    win_refs = out_refs[7:7 + n_win]
    tm = x_ref.shape[0]
    for g in range(3):
        k = _rope_cols(_dot(xb, w_ref[:, 2 * g * gw:(2 * g + 1) * gw]), cos, sin, half)
        v = _dot(xb, w_ref[:, (2 * g + 1) * gw:(2 * g + 2) * gw])
        kv = jnp.concatenate([k, v], axis=1)
        kv_refs[g][...] = kv.astype(kv_refs[g].dtype)
        if n_win:
            rows = win_refs[g].shape[0]
            win_refs[g][...] = kv[tm - rows:, :]
        q = _rope_cols(_dot(xb, w_ref[:, (6 + g) * gw:(7 + g) * gw]), cos, sin, half)
        q_refs[g][...] = q.astype(q_refs[g].dtype)
    qm_ref[...] = _dot(xb, w_ref[:, 9 * gw:10 * gw]).astype(qm_ref.dtype)


def _in_b(x2d, w_bf, cos, sin, tm, out_dtype, batch, seq, with_windows):
    t, d = x2d.shape
    gw = GROUP_WIDTH
    ntab = cos.shape[0] // tm
    row = lambda i: (i, 0)
    tab = lambda i: (i % ntab, 0)
    out_specs = ([pl.BlockSpec((tm, gw), row)] * 3 + [pl.BlockSpec((tm, 2 * gw), row)] * 3
                 + [pl.BlockSpec((tm, MEM_WIDTH), row)])
    out_shape = ([jax.ShapeDtypeStruct((t, gw), out_dtype)] * 3
                 + [jax.ShapeDtypeStruct((t, 2 * gw), out_dtype)] * 3
                 + [jax.ShapeDtypeStruct((t, MEM_WIDTH), out_dtype)])
    if with_windows:
        nt = seq // tm
        for window, _ in DIL_PAIRS:
            w = min(window, seq)
            rb = min(w, tm)
            nblk = w // rb
            out_specs.append(pl.BlockSpec(
                (rb, 2 * gw),
                lambda i, nblk=nblk: ((i // nt) * nblk + jnp.maximum(i % nt - (nt - nblk), 0), 0)))
            out_shape.append(jax.ShapeDtypeStruct((batch * w, 2 * gw), F32))
    return pl.pallas_call(
        functools.partial(_in_b_kernel, n_win=3 if with_windows else 0),
        grid=(t // tm,),
        in_specs=[pl.BlockSpec((tm, d), row), _const_spec(w_bf.shape),
                  pl.BlockSpec((tm, LANES), tab), pl.BlockSpec((tm, LANES), tab)],
        out_specs=out_specs,
        out_shape=out_shape,
        compiler_params=_params(("arbitrary",)),
        name="in_proj_b",
    )(x2d, w_bf, cos, sin)


def _dilated_kernel(q_ref, kv_ref, o_ref, lse_ref, ring_ref):
    i = pl.program_id(2)
    blk = DIL_BLOCK
    gw = GROUP_WIDTH
    slot = i & 1

    @pl.when(i == 0)
    def _():
        ring_ref[1] = jnp.zeros(ring_ref.shape[1:], ring_ref.dtype)

    q = q_ref[0]
    cur = kv_ref[0]
    ring_ref[slot] = cur
    prev = ring_ref[1 - slot]
    kcat = jnp.concatenate([prev[:, :gw], cur[:, :gw]], axis=0)
    vcat = jnp.concatenate([prev[:, gw:], cur[:, gw:]], axis=0)
    qi = lax.broadcasted_iota(jnp.int32, (blk, 2 * blk), 0)
    kj = lax.broadcasted_iota(jnp.int32, (blk, 2 * blk), 1)
    delta = qi + blk - kj
    valid = (delta >= 0) & (delta <= blk) & ((i > 0) | (kj >= blk))
    zero = jnp.zeros((), BF16)
    out = None
    lse_full = jnp.zeros((blk, gw), F32)
    for h in range(GROUP_HEADS):
        hm = _head_mask((blk, gw), h, DIL_HEAD_DIM)
        qh = jnp.where(hm, q, zero)
        s = jnp.where(valid, _dot_nt(qh, kcat) * (DIL_HEAD_DIM ** -0.5), NEG_BIG)
        m = jnp.max(s, axis=-1, keepdims=True)
        e = jnp.exp(s - m)
        l = jnp.sum(e, axis=-1, keepdims=True)
        p = (e / l).astype(BF16)
        vh = jnp.where(_head_mask(vcat.shape, h, DIL_HEAD_DIM), vcat, zero)
        o = _dot(p, vh)
        out = o if out is None else out + o
        lse_full = jnp.where(hm, m + jnp.log(l), lse_full)
    o_ref[0] = out.astype(o_ref.dtype)
    lse_ref[0] = lse_full


def _dilated(q, kv, dil, batch, seq):
    gw = GROUP_WIDTH
    m = seq // dil
    nb = m // DIL_BLOCK
    qv = q.reshape(batch, m, dil * gw)
    kvv = kv.reshape(batch, m, dil * 2 * gw)
    idx = lambda b, r, i: (b, i, r)
    o, lse = pl.pallas_call(
        _dilated_kernel,
        grid=(batch, dil, nb),
        in_specs=[pl.BlockSpec((1, DIL_BLOCK, gw), idx), pl.BlockSpec((1, DIL_BLOCK, 2 * gw), idx)],
        out_specs=[pl.BlockSpec((1, DIL_BLOCK, gw), idx)] * 2,
        out_shape=[jax.ShapeDtypeStruct((batch, m, dil * gw), BF16),
                   jax.ShapeDtypeStruct((batch, m, dil * gw), F32)],
        scratch_shapes=[pltpu.VMEM((2, DIL_BLOCK, 2 * gw), BF16)],
        compiler_params=_params(("arbitrary", "arbitrary", "arbitrary")),
        name=f"dilated_d{dil}",
    )(qv, kvv)
    return o.reshape(batch * seq, gw), lse.reshape(batch * seq, gw)


def _rope_tables(pos, head_dim):
    inv = ROPE_THETA ** (-jnp.arange(0, head_dim, 2, dtype=F32) / head_dim)
    ang = pos[:, None] * inv[None, :]
    cos, sin = jnp.cos(ang), jnp.sin(ang)
    reps = LANES // head_dim
    return (jnp.tile(jnp.concatenate([cos, cos], axis=-1), (1, reps)),
            jnp.tile(jnp.concatenate([-sin, sin], axis=-1), (1, reps)))


def kernel(x_prompt, x_sample, mem_prompt, cache_mem_kv, state_ret, cache_win_kv_g1, cache_win_kv_g2, cache_win_kv_g3, w_in_a, w_in_b, w_out, w_kv_shared, w_mem_kv, ln_mix_g, ln_mix_b, ln_ffn_g, ln_ffn_b, w_ffn_in, w_ffn_out):
    batch, seq, d = x_prompt.shape
    dec_batch, dec_seq, _ = x_sample.shape
    n_mem = mem_prompt.shape[1]
    gw = GROUP_WIDTH
    tm_p = 512
    tm_s = dec_batch * dec_seq
    win_caches = (cache_win_kv_g1, cache_win_kv_g2, cache_win_kv_g3)

    pos_p = jnp.arange(seq, dtype=F32)
    pos_s = jnp.tile(PAST_LEN + jnp.arange(dec_seq, dtype=F32), dec_batch)
    tabs_p = {hd: _rope_tables(pos_p, hd) for hd in (RET_HEAD_DIM, DIL_HEAD_DIM)}
    tabs_s = {hd: _rope_tables(pos_s, hd) for hd in (RET_HEAD_DIM, DIL_HEAD_DIM)}

    w_a = w_in_a[0].astype(BF16)
    mw = MAIN_WIDTH
    kv_cols = [w_kv_shared[:, c0:c0 + gw] for g in range(3) for c0 in (g * gw, mw + g * gw)]
    w_b = jnp.concatenate(kv_cols + [w_in_b[0]], axis=1).astype(BF16)
    w_o = w_out.astype(BF16)
    w_fi = w_ffn_in.astype(BF16)
    w_fo = w_ffn_out.astype(BF16)
    vec = lambda a, l: a[l].reshape(1, d)

    memkv_p = _mem_proj(mem_prompt.reshape(batch * n_mem, d), w_mem_kv.astype(BF16))
    memkv_p = memkv_p.reshape(DEPTH, batch, n_mem, 2 * MEM_WIDTH)
    memkv_s = cache_mem_kv.reshape(DEPTH, dec_batch, n_mem, 2 * MEM_WIDTH)

    xp = x_prompt.reshape(batch * seq, d)
    xs = x_sample.reshape(tm_s, d)

    def post(l, mix_inputs, x2d, tm, combine):
        return _post(mix_inputs, x2d, w_o[l], vec(ln_mix_g, l), vec(ln_mix_b, l), w_fi[l], w_fo[l],
                     vec(ln_ffn_g, l), vec(ln_ffn_b, l), tm, combine)

    q, k, v, g, qm = _in_a(xp, w_a, *tabs_p[RET_HEAD_DIM], tm_p, BF16)
    mix_p, state_p = _retention(q, k, v, g, qm, memkv_p[0], batch, seq)
    xp = post(0, [mix_p], xp, tm_p, False)

    q, k, v, g, qm = _in_a(xs, w_a, *tabs_s[RET_HEAD_DIM], tm_s, F32)
    mix_s, state_s = _retention_sample(q, k, v, g, qm, memkv_s[0], state_ret[0], dec_batch, dec_seq)
    xs = post(0, [mix_s], xs, tm_s, False)

    outs = _in_b(xp, w_b, *tabs_p[DIL_HEAD_DIM], tm_p, BF16, batch, seq, True)
    q_g, kv_g, qm, win_p = outs[0:3], outs[3:6], outs[6], outs[7:10]
    att = [_dilated(q_g[i], kv_g[i], DIL_PAIRS[i][1], batch, seq) for i in range(3)]
    mem_o = _mem_attn(qm, memkv_p[1], tm_p, seq)
    xp = post(1, [a[0] for a in att] + [a[1] for a in att] + [mem_o], xp, tm_p, True)

    outs = _in_b(xs, w_b, *tabs_s[DIL_HEAD_DIM], tm_s, F32, dec_batch, dec_seq, False)
    q_g, kv_new, qm = outs[0:3], outs[3:6], outs[6]
    att = []
    for i, (window, dil) in enumerate(DIL_PAIRS):
        cache2d = win_caches[i].reshape(dec_batch, win_caches[i].shape[1], 2 * gw)
        att.append(_dilated_sample(q_g[i], cache2d, kv_new[i], window, dil, dec_batch, dec_seq))
    mem_o = _mem_attn_sample(qm, memkv_s[1], dec_batch, dec_seq)
    xs = post(1, [a[0] for a in att] + [a[1] for a in att] + [mem_o], xs, tm_s, True)

    kv5 = lambda a, b_: a.reshape(b_, -1, 2, GROUP_HEADS, DIL_HEAD_DIM)
    return (xp.reshape(batch, seq, d), xs.reshape(dec_batch, dec_seq, d),
            state_p[None], state_s[None],
            memkv_p.reshape(DEPTH, batch, n_mem, 2, MEM_HEADS, MEM_HEAD_DIM),
            kv5(win_p[0], batch), kv5(win_p[1], batch), kv5(win_p[2], batch),
            kv5(att[0][2], dec_batch), kv5(att[1][2], dec_batch), kv5(att[2][2], dec_batch))
```

```python
import functools
import math

import jax
import jax.numpy as jnp
from jax import lax
from jax.experimental import pallas as pl
from jax.experimental.pallas import tpu as pltpu

F32 = jnp.float32
BF16 = jnp.bfloat16

D_MODEL = 1024
MEM_HEADS = 4
MEM_HEAD_DIM = 64
MEM_WIDTH = MEM_HEADS * MEM_HEAD_DIM
MAIN_WIDTH = D_MODEL - MEM_WIDTH
RET_HEADS = 6
RET_HEAD_DIM = MAIN_WIDTH // RET_HEADS
RET_CHUNK = 128
DIL_PAIRS = ((128, 1), (512, 4), (2048, 16))
GROUP_HEADS = 4
DIL_HEAD_DIM = 64
GROUP_WIDTH = GROUP_HEADS * DIL_HEAD_DIM
DIL_BLOCK = 128
FFN_HIDDEN = 2816
ROPE_THETA = 10000.0
LN_EPS = 1e-5
DEPTH = 2
ALPHA = (2 * DEPTH) ** 0.25
PAST_LEN = 8192
NEG_BIG = -1e30

LANES = 128
VMEM_LIMIT = 56 * 1024 * 1024
FFN_COL_CHUNK = 256

LOG_G = tuple(math.log1p(-(2.0 ** (-5.0 - h))) for h in range(RET_HEADS))


def _dot(a, b):
    return jnp.dot(a, b, preferred_element_type=F32)


def _dot_nt(a, b):
    return lax.dot_general(a, b, (((1,), (1,)), ((), ())), preferred_element_type=F32)


def _dot_tn(a, b):
    return lax.dot_general(a, b, (((0,), (0,)), ((), ())), preferred_element_type=F32)


def _silu(x):
    return x / (1.0 + jnp.exp(-x))


def _layer_norm(z, g, b):
    mu = jnp.mean(z, axis=-1, keepdims=True)
    zc = z - mu
    var = jnp.mean(zc * zc, axis=-1, keepdims=True)
    return zc * lax.rsqrt(var + LN_EPS) * g + b


def _rope_lanes(y, cos, sin_signed, half):
    if 2 * half == LANES:
        partner = pltpu.roll(y, half, 1)
    else:
        lane = lax.broadcasted_iota(jnp.int32, y.shape, 1)
        first = (lane & (2 * half - 1)) < half
        partner = jnp.where(first, pltpu.roll(y, LANES - half, 1), pltpu.roll(y, half, 1))
    return y * cos + partner * sin_signed


def _rope_cols(y, cos, sin_signed, half):
    parts = [_rope_lanes(y[:, j:j + LANES], cos, sin_signed, half) for j in range(0, y.shape[1], LANES)]
    return parts[0] if len(parts) == 1 else jnp.concatenate(parts, axis=1)


def _rope_rows(yt, cos_t, sin_t, head_dim):
    half = head_dim // 2
    parts = []
    for r0 in range(0, yt.shape[0], head_dim):
        blk = yt[r0:r0 + head_dim]
        swapped = jnp.concatenate([blk[half:], blk[:half]], axis=0)
        parts.append(blk * cos_t + swapped * sin_t)
    return jnp.concatenate(parts, axis=0)


def _head_mask(shape, h, width):
    lane = lax.broadcasted_iota(jnp.int32, shape, len(shape) - 1)
    return (lane >= h * width) & (lane < (h + 1) * width)


def _stack_heads(q, width):
    parts = [jnp.where(_head_mask(q.shape, h, width), q, 0.0) for h in range(q.shape[1] // width)]
    return jnp.concatenate(parts, axis=0).astype(BF16)


def _pick_heads(full, t, width):
    out = None
    for h in range(full.shape[1] // width):
        blk = full[h * t:(h + 1) * t, :]
        piece = jnp.where(_head_mask(blk.shape, h, width), blk, 0.0)
        out = piece if out is None else out + piece
    return out


def _const_spec(block, index):
    return pl.BlockSpec(block, lambda *_: index, pipeline_mode=pl.Buffered(1))


def _params(sem):
    return pltpu.CompilerParams(dimension_semantics=sem, vmem_limit_bytes=VMEM_LIMIT)


def _mem_proj_kernel(m_ref, w_ref, o_ref):
    o_ref[0, 0] = _dot_nt(w_ref[0], m_ref[0].astype(BF16))


def _mem_proj(mem, w_t):
    batch, n_mem, d = mem.shape
    depth, n, _ = w_t.shape
    return pl.pallas_call(
        _mem_proj_kernel,
        grid=(depth, batch),
        in_specs=[pl.BlockSpec((1, n_mem, d), lambda l, b: (b, 0, 0)),
                  pl.BlockSpec((1, n, d), lambda l, b: (l, 0, 0))],
        out_specs=pl.BlockSpec((1, 1, n, n_mem), lambda l, b: (l, b, 0, 0)),
        out_shape=jax.ShapeDtypeStruct((depth, batch, n, n_mem), F32),
        compiler_params=_params(("arbitrary", "arbitrary")),
        name="mem_proj",
    )(mem, w_t)


def _mem_attn_tile(qm, kv_t):
    t = qm.shape[0]
    q4 = _stack_heads(qm.astype(F32), MEM_HEAD_DIM)
    k_t = kv_t[:MEM_WIDTH].astype(BF16)
    v_t = kv_t[MEM_WIDTH:].astype(BF16)
    s = _dot(q4, k_t) * (MEM_HEAD_DIM ** -0.5)
    m = jnp.max(s, axis=-1, keepdims=True)
    e = jnp.exp(s - m)
    p = (e / jnp.sum(e, axis=-1, keepdims=True)).astype(BF16)
    return _pick_heads(_dot_nt(p, v_t), t, MEM_HEAD_DIM)


def _mem_attn_kernel(q_ref, kv_ref, o_ref):
    o_ref[...] = _mem_attn_tile(q_ref[...], kv_ref[0, 0]).astype(o_ref.dtype)


def _mem_attn(qm, memkv_t, layer, tq, rows_per_batch, out_dtype):
    t = qm.shape[0]
    per = rows_per_batch // tq
    return pl.pallas_call(
        _mem_attn_kernel,
        grid=(t // tq,),
        in_specs=[pl.BlockSpec((tq, MEM_WIDTH), lambda i: (i, 0)),
                  pl.BlockSpec((1, 1) + memkv_t.shape[2:], lambda i: (layer, i // per, 0, 0))],
        out_specs=pl.BlockSpec((tq, MEM_WIDTH), lambda i: (i, 0)),
        out_shape=jax.ShapeDtypeStruct((t, MEM_WIDTH), out_dtype),
        compiler_params=_params(("parallel",)),
        name="mem_attn",
    )(qm, memkv_t)


def _in_a_kernel(x_ref, w_ref, cos_ref, sin_ref, q_ref, k_ref, v_ref, g_ref, qm_ref):
    xb = x_ref[...].astype(BF16)
    cos = cos_ref[...]
    sin = sin_ref[...]
    mw = MAIN_WIDTH
    half = RET_HEAD_DIM // 2
    q = _rope_cols(_dot(xb, w_ref[0, :, 0:mw]), cos, sin, half)
    q_ref[...] = q.astype(q_ref.dtype)
    k = _rope_cols(_dot(xb, w_ref[0, :, mw:2 * mw]), cos, sin, half) * (RET_HEAD_DIM ** -0.5)
    k_ref[...] = k.astype(k_ref.dtype)
    v_ref[...] = _dot(xb, w_ref[0, :, 2 * mw:3 * mw]).astype(v_ref.dtype)
    g_ref[...] = _dot(xb, w_ref[0, :, 3 * mw:4 * mw]).astype(g_ref.dtype)
    qm_ref[...] = _dot(xb, w_ref[0, :, 4 * mw:4 * mw + MEM_WIDTH]).astype(qm_ref.dtype)


def _in_a(x2d, w_bf, cos, sin, tm, out_dtype):
    t, d = x2d.shape
    ntab = cos.shape[0] // tm
    row = lambda i: (i, 0)
    tab = lambda i: (i % ntab, 0)
    widths = (MAIN_WIDTH,) * 4 + (MEM_WIDTH,)
    return pl.pallas_call(
        _in_a_kernel,
        grid=(t // tm,),
        in_specs=[pl.BlockSpec((tm, d), row), _const_spec((1,) + w_bf.shape[1:], (0, 0, 0)),
                  pl.BlockSpec((tm, LANES), tab), pl.BlockSpec((tm, LANES), tab)],
        out_specs=[pl.BlockSpec((tm, w), row) for w in widths],
        out_shape=[jax.ShapeDtypeStruct((t, w), out_dtype) for w in widths],
        compiler_params=_params(("parallel",)),
        name="in_proj_a",
    )(x2d, w_bf, cos, sin)


def _retention_kernel(q_ref, k_ref, v_ref, g_ref, qm_ref, mkv_ref, mix_ref, st_ref,
                      dec_ref, rdec_ref, kdec_ref):
    c = pl.program_id(1)
    cs = RET_CHUNK

    @pl.when(c == 0)
    def _():
        st_ref[...] = jnp.zeros_like(st_ref)
        row = lax.broadcasted_iota(jnp.int32, (cs, cs), 0).astype(F32)
        col = lax.broadcasted_iota(jnp.int32, (cs, cs), 1).astype(F32)
        diff = row - col
        for h in range(RET_HEADS):
            lg = LOG_G[h]
            dec_ref[h] = jnp.where(diff >= 0, jnp.exp(jnp.maximum(diff, 0.0) * lg), 0.0)
            rdec_ref[h] = jnp.exp((row + 1.0) * lg)
            kdec_ref[h] = jnp.exp((cs - 1.0 - row) * lg)

    for h in range(RET_HEADS):
        hs = slice(h * RET_HEAD_DIM, (h + 1) * RET_HEAD_DIM)
        qh = q_ref[:, hs]
        kh = k_ref[:, hs]
        vh = v_ref[:, hs]
        st = st_ref[0, h]
        s = _dot_nt(qh, kh) * dec_ref[h]
        inner = _dot(s.astype(BF16), vh)
        cross = _dot(qh, st.astype(BF16)) * rdec_ref[h]
        kd = (kh.astype(F32) * kdec_ref[h]).astype(BF16)
        st_ref[0, h] = math.exp(cs * LOG_G[h]) * st + _dot_tn(kd, vh)
        o = inner + cross
        mu = jnp.mean(o, axis=-1, keepdims=True)
        oc = o - mu
        var = jnp.mean(oc * oc, axis=-1, keepdims=True)
        on = oc * lax.rsqrt(var + LN_EPS)
        mix_ref[:, hs] = (_silu(g_ref[:, hs].astype(F32)) * on).astype(mix_ref.dtype)

    mix_ref[:, MAIN_WIDTH:] = _mem_attn_tile(qm_ref[...], mkv_ref[0, 0]).astype(mix_ref.dtype)


def _retention(q, k, v, g, qm, memkv_t, batch, seq):
    t = q.shape[0]
    nc = seq // RET_CHUNK
    tok = lambda b, c: (b * nc + c, 0)
    sq = (RET_HEADS, RET_CHUNK, RET_CHUNK)
    return pl.pallas_call(
        _retention_kernel,
        grid=(batch, nc),
        in_specs=[pl.BlockSpec((RET_CHUNK, MAIN_WIDTH), tok)] * 4
                 + [pl.BlockSpec((RET_CHUNK, MEM_WIDTH), tok),
                    pl.BlockSpec((1, 1) + memkv_t.shape[2:], lambda b, c: (0, b, 0, 0))],
        out_specs=[pl.BlockSpec((RET_CHUNK, D_MODEL), tok),
                   pl.BlockSpec((1, RET_HEADS, RET_HEAD_DIM, RET_HEAD_DIM), lambda b, c: (b, 0, 0, 0))],
        out_shape=[jax.ShapeDtypeStruct((t, D_MODEL), BF16),
                   jax.ShapeDtypeStruct((batch, RET_HEADS, RET_HEAD_DIM, RET_HEAD_DIM), F32)],
        scratch_shapes=[pltpu.VMEM(sq, F32), pltpu.VMEM(sq, F32), pltpu.VMEM(sq, F32)],
        compiler_params=_params(("arbitrary", "arbitrary")),
        name="retention",
    )(q, k, v, g, qm, memkv_t)


def _retention_sample_kernel(q_ref, k_ref, v_ref, g_ref, qm_ref, mkv_ref, st_ref, mix_ref, nst_ref):
    t = q_ref.shape[0]
    pad = jnp.zeros((LANES - t, RET_HEAD_DIM), F32)
    row = lax.broadcasted_iota(jnp.int32, (t, LANES), 0).astype(F32)
    col = lax.broadcasted_iota(jnp.int32, (t, LANES), 1).astype(F32)
    prow = lax.broadcasted_iota(jnp.int32, (LANES, RET_HEAD_DIM), 0).astype(F32)
    diff = row - col
    for h in range(RET_HEADS):
        lg = LOG_G[h]
        hs = slice(h * RET_HEAD_DIM, (h + 1) * RET_HEAD_DIM)
        qh = q_ref[:, hs]
        kp = jnp.concatenate([k_ref[:, hs], pad], axis=0)
        vp = jnp.concatenate([v_ref[:, hs], pad], axis=0)
        st = st_ref[0, 0, h]
        dec = jnp.where(diff >= 0, jnp.exp(jnp.maximum(diff, 0.0) * lg), 0.0)
        inner = _dot(_dot_nt(qh, kp) * dec, vp)
        cross = _dot(qh, st) * jnp.exp((row + 1.0) * lg)
        kd = kp * jnp.exp((t - 1.0 - prow) * lg)
        nst_ref[0, h] = math.exp(t * lg) * st + _dot_tn(kd, vp)
        o = inner + cross
        mu = jnp.mean(o, axis=-1, keepdims=True)
        oc = o - mu
        var = jnp.mean(oc * oc, axis=-1, keepdims=True)
        on = oc * lax.rsqrt(var + LN_EPS)
        mix_ref[:, hs] = _silu(g_ref[:, hs]) * on
    mix_ref[:, MAIN_WIDTH:] = _mem_attn_tile(qm_ref[...], mkv_ref[0, 0])


def _retention_sample(q, k, v, g, qm, memkv_t, state, batch, t):
    tok = lambda b: (b, 0)
    hd = (RET_HEADS, RET_HEAD_DIM, RET_HEAD_DIM)
    return pl.pallas_call(
        _retention_sample_kernel,
        grid=(batch,),
        in_specs=[pl.BlockSpec((t, MAIN_WIDTH), tok)] * 4
                 + [pl.BlockSpec((t, MEM_WIDTH), tok),
                    pl.BlockSpec((1, 1) + memkv_t.shape[2:], lambda b: (0, b, 0, 0)),
                    pl.BlockSpec((1, 1) + hd, lambda b: (0, b, 0, 0, 0))],
        out_specs=[pl.BlockSpec((t, D_MODEL), tok), pl.BlockSpec((1,) + hd, lambda b: (b, 0, 0, 0))],
        out_shape=[jax.ShapeDtypeStruct((batch * t, D_MODEL), F32),
                   jax.ShapeDtypeStruct((batch,) + hd, F32)],
        compiler_params=_params(("parallel",)),
        name="retention_sample",
    )(q, k, v, g, qm, memkv_t, state)


def _dilated_sample_kernel(q_ref, cache_ref, new_ref, new_t_ref, o_ref, lse_ref, win_ref, *, window, dil):
    t = q_ref.shape[0]
    n_buf = cache_ref.shape[2]
    gw = GROUP_WIDTH
    q4 = _stack_heads(q_ref[...], DIL_HEAD_DIM)
    rows = q4.shape[0]
    cache = cache_ref[0]
    new = new_ref[...]
    newp = jnp.concatenate([new, jnp.zeros((LANES - t, new.shape[1]), F32)], axis=0).astype(BF16)
    scale = DIL_HEAD_DIM ** -0.5

    def masked(s, first_index):
        key = lax.broadcasted_iota(jnp.int32, s.shape, 1) + first_index
        tok = lax.broadcasted_iota(jnp.int32, s.shape, 0) & (t - 1)
        delta = n_buf + tok - key
        valid = (delta >= 0) & (delta <= window) & ((delta & (dil - 1)) == 0)
        return jnp.where(valid, s, NEG_BIG)

    s_c = masked(_dot(q4, cache[:gw].astype(BF16)) * scale, 0)
    s_n = masked(_dot_nt(q4, newp[:, :gw]) * scale, n_buf)
    m = jnp.maximum(jnp.max(s_c, axis=-1, keepdims=True), jnp.max(s_n, axis=-1, keepdims=True))
    e_c = jnp.exp(s_c - m)
    e_n = jnp.exp(s_n - m)
    l = jnp.sum(e_c, axis=-1, keepdims=True) + jnp.sum(e_n, axis=-1, keepdims=True)
    inv = 1.0 / l
    full = (_dot_nt((e_c * inv).astype(BF16), cache[gw:].astype(BF16))
            + _dot((e_n * inv).astype(BF16), newp[:, gw:]))
    o_ref[...] = _pick_heads(full, t, DIL_HEAD_DIM)
    lse_ref[...] = _pick_heads(jnp.broadcast_to(m + jnp.log(l), (rows, gw)), t, DIL_HEAD_DIM)
    win_ref[0, :, 0:n_buf - t] = cache[:, t:]
    win_ref[0, :, n_buf - t:] = new_t_ref[0]


def _dilated_sample(q, cache_t, new_kv, new_kv_t, window, dil, batch, t):
    n_buf = cache_t.shape[2]
    tok = lambda b: (b, 0)
    big = lambda b: (b, 0, 0)
    return pl.pallas_call(
        functools.partial(_dilated_sample_kernel, window=window, dil=dil),
        grid=(batch,),
        in_specs=[pl.BlockSpec((t, GROUP_WIDTH), tok),
                  pl.BlockSpec((1, 2 * GROUP_WIDTH, n_buf), big),
                  pl.BlockSpec((t, 2 * GROUP_WIDTH), tok),
                  pl.BlockSpec((1, 2 * GROUP_WIDTH, t), big)],
        out_specs=[pl.BlockSpec((t, GROUP_WIDTH), tok), pl.BlockSpec((t, GROUP_WIDTH), tok),
                   pl.BlockSpec((1, 2 * GROUP_WIDTH, n_buf), big)],
        out_shape=[jax.ShapeDtypeStruct((batch * t, GROUP_WIDTH), F32),
                   jax.ShapeDtypeStruct((batch * t, GROUP_WIDTH), F32),
                   jax.ShapeDtypeStruct(cache_t.shape, F32)],
        compiler_params=_params(("parallel",)),
        name=f"dilated_sample_w{window}",
    )(q, cache_t, new_kv, new_kv_t)


def _natural_rows(ref, dil, scr):
    if dil == 1:
        return ref[0, 0].astype(F32)
    n = ref.shape[2]
    for r in range(dil):
        v = ref[0, r].astype(F32)
        for c in range(v.shape[1] // LANES):
            scr[c, pl.ds(r, n, stride=dil), :] = v[:, c * LANES:(c + 1) * LANES]
    return jnp.concatenate([scr[c] for c in range(scr.shape[0])], axis=1)


def _post_kernel(*refs, dils):
    combine = dils is not None
    if combine:
        o_refs, l_refs, mem_ref = refs[0:3], refs[3:6], refs[6]
        rest = refs[7:]
    else:
        mix_ref = refs[0]
        rest = refs[1:]
    x_ref, wo_ref, g1_ref, b1_ref, wi_ref, w2_ref, g2_ref, b2_ref, out_ref, act_ref = rest[:10]
    scrs = list(rest[10:])

    if combine:
        def nat(r, d):
            return _natural_rows(r, d, scrs.pop(0) if d > 1 else None)
        os_ = [nat(o_refs[g], dils[g]) for g in range(3)]
        ls = [nat(l_refs[g], dils[g]) for g in range(3)]
        m = jnp.maximum(jnp.maximum(ls[0], ls[1]), ls[2])
        es = [jnp.exp(v - m) for v in ls]
        inv = 1.0 / (es[0] + es[1] + es[2])
        parts = [(os_[g] * (es[g] * inv)).astype(BF16) for g in range(3)]
        parts.append(mem_ref[...].astype(BF16))
        mix = jnp.concatenate(parts, axis=1)
    else:
        mix = mix_ref[...].astype(BF16)

    x1 = _layer_norm(ALPHA * x_ref[...] + _dot(mix, wo_ref[0]), g1_ref[0], b1_ref[0])
    x1b = x1.astype(BF16)
    for c in range(0, FFN_HIDDEN, FFN_COL_CHUNK):
        gate = _dot(x1b, wi_ref[0, :, c:c + FFN_COL_CHUNK])
        up = _dot(x1b, wi_ref[0, :, FFN_HIDDEN + c:FFN_HIDDEN + c + FFN_COL_CHUNK])
        act_ref[:, c:c + FFN_COL_CHUNK] = (_silu(gate) * up).astype(BF16)
    y = _dot(act_ref[...], w2_ref[0])
    out_ref[...] = _layer_norm(ALPHA * x1 + y, g2_ref[0], b2_ref[0])


def _post(layer, mix_inputs, x2d, wo, g1, b1, wi, w2, g2, b2, tm, dils, rows_per_batch):
    t, d = x2d.shape
    row = lambda i: (i, 0)
    nt = rows_per_batch // tm
    if dils is None:
        mix_specs = [pl.BlockSpec((tm, mix_inputs[0].shape[1]), row)]
        scratch = []
    else:
        res = lambda i: (i // nt, 0, i % nt, 0)
        mix_specs = [pl.BlockSpec((1, dl, tm // dl, GROUP_WIDTH), res) for dl in dils] * 2
        mix_specs.append(pl.BlockSpec((tm, MEM_WIDTH), row))
        scratch = [pltpu.VMEM((GROUP_WIDTH // LANES, tm, LANES), F32) for dl in dils * 2 if dl > 1]
    lsel = (layer, 0, 0)
    vec = _const_spec((1, 1, d), lsel)
    return pl.pallas_call(
        functools.partial(_post_kernel, dils=dils),
        grid=(t // tm,),
        in_specs=mix_specs + [pl.BlockSpec((tm, d), row), _const_spec((1,) + wo.shape[1:], lsel), vec, vec,
                              _const_spec((1,) + wi.shape[1:], lsel), _const_spec((1,) + w2.shape[1:], lsel),
                              vec, vec],
        out_specs=pl.BlockSpec((tm, d), row),
        out_shape=jax.ShapeDtypeStruct((t, d), F32),
        scratch_shapes=[pltpu.VMEM((tm, FFN_HIDDEN), BF16)] + scratch,
        compiler_params=_params(("parallel",)),
        name="post" if dils is None else "post_combine",
    )(*mix_inputs, x2d, wo, g1, b1, wi, w2, g2, b2)


def _residue_major(ref, y, dil, scr):
    if dil == 1:
        ref[0, 0] = y.astype(ref.dtype)
        return
    n = ref.shape[2]
    ncol = y.shape[1] // LANES
    for c in range(ncol):
        scr[c] = y[:, c * LANES:(c + 1) * LANES]
    for r in range(dil):
        rows = [scr[c, pl.ds(r, n, stride=dil), :] for c in range(ncol)]
        ref[0, r] = jnp.concatenate(rows, axis=1).astype(ref.dtype)


def _in_b_kernel(x_ref, w_ref, wt_ref, cos_ref, sin_ref, cos_t_ref, sin_t_ref, *refs, dils, win_tiles, nt):
    xb = x_ref[...].astype(BF16)
    cos = cos_ref[...]
    sin = sin_ref[...]
    gw = GROUP_WIDTH
    half = DIL_HEAD_DIM // 2
    q_refs, kv_refs, qm_ref, win_refs = refs[0:3], refs[3:6], refs[6], refs[7:10]
    scrs = list(refs[10:])
    tm = x_ref.shape[0]
    j = pl.program_id(0) % nt
    for g in range(3):
        k = _rope_cols(_dot(xb, w_ref[:, 2 * g * gw:(2 * g + 1) * gw]), cos, sin, half)
        v = _dot(xb, w_ref[:, (2 * g + 1) * gw:(2 * g + 2) * gw])
        _residue_major(kv_refs[g], jnp.concatenate([k, v], axis=1), dils[g], scrs.pop(0) if dils[g] > 1 else None)
        q = _rope_cols(_dot(xb, w_ref[:, (6 + g) * gw:(7 + g) * gw]), cos, sin, half)
        _residue_major(q_refs[g], q, dils[g], scrs.pop(0) if dils[g] > 1 else None)

        @pl.when(j >= nt - win_tiles[g])
        def _(g=g):
            kv_t = _dot_nt(wt_ref[2 * g * gw:(2 * g + 2) * gw, :], xb)
            k_t = _rope_rows(kv_t[:gw], cos_t_ref[...], sin_t_ref[...], DIL_HEAD_DIM)
            full = jnp.concatenate([k_t, kv_t[gw:]], axis=0)
            win_refs[g][0] = full[:, tm - win_refs[g].shape[2]:]
    qm_ref[...] = _dot(xb, w_ref[:, 9 * gw:10 * gw]).astype(qm_ref.dtype)


def _in_b(x2d, w_bf, wt_bf, tabs, tabs_t, tm, out_dtype, batch, seq, dils, windows):
    t, d = x2d.shape
    gw = GROUP_WIDTH
    nt = seq // tm
    ntab = tabs[0].shape[0] // tm
    row = lambda i: (i, 0)
    tab = lambda i: (i % ntab, 0)
    tab_t = lambda i: (0, i % ntab)
    res = lambda i: (i // nt, 0, i % nt, 0)
    out_specs = ([pl.BlockSpec((1, dl, tm // dl, gw), res) for dl in dils]
                 + [pl.BlockSpec((1, dl, tm // dl, 2 * gw), res) for dl in dils]
                 + [pl.BlockSpec((tm, MEM_WIDTH), row)])
    out_shape = ([jax.ShapeDtypeStruct((batch, dl, seq // dl, gw), out_dtype) for dl in dils]
                 + [jax.ShapeDtypeStruct((batch, dl, seq // dl, 2 * gw), out_dtype) for dl in dils]
                 + [jax.ShapeDtypeStruct((t, MEM_WIDTH), out_dtype)])
    win_tiles = []
    for w in windows:
        rb = min(w, tm)
        nblk = w // rb
        win_tiles.append(nblk)
        out_specs.append(pl.BlockSpec(
            (1, 2 * gw, rb), lambda i, nblk=nblk: (i // nt, 0, jnp.maximum(i % nt - (nt - nblk), 0))))
        out_shape.append(jax.ShapeDtypeStruct((batch, 2 * gw, w), F32))
    scratch = []
    for dl in dils:
        if dl > 1:
            scratch += [pltpu.VMEM((2 * gw // LANES, tm, LANES), F32), pltpu.VMEM((gw // LANES, tm, LANES), F32)]
    return pl.pallas_call(
        functools.partial(_in_b_kernel, dils=tuple(dils), win_tiles=tuple(win_tiles), nt=nt),
        grid=(t // tm,),
        in_specs=[pl.BlockSpec((tm, d), row), _const_spec(w_bf.shape, (0, 0)), _const_spec(wt_bf.shape, (0, 0)),
                  pl.BlockSpec((tm, LANES), tab), pl.BlockSpec((tm, LANES), tab),
                  pl.BlockSpec((DIL_HEAD_DIM, tm), tab_t), pl.BlockSpec((DIL_HEAD_DIM, tm), tab_t)],
        out_specs=out_specs,
        out_shape=out_shape,
        scratch_shapes=scratch,
        compiler_params=_params(("arbitrary",)),
        name="in_proj_b",
    )(x2d, w_bf, wt_bf, *tabs, *tabs_t)


def _dilated_kernel(q_ref, kv_ref, o_ref, lse_ref, ring_ref):
    i = pl.program_id(2)
    blk = DIL_BLOCK
    gw = GROUP_WIDTH
    slot = i & 1

    @pl.when(i == 0)
    def _():
        ring_ref[1] = jnp.zeros(ring_ref.shape[1:], ring_ref.dtype)

    q = q_ref[0, 0]
    cur = kv_ref[0, 0]
    ring_ref[slot] = cur
    prev = ring_ref[1 - slot]
    kcat = jnp.concatenate([prev[:, :gw], cur[:, :gw]], axis=0)
    vcat = jnp.concatenate([prev[:, gw:], cur[:, gw:]], axis=0)
    qi = lax.broadcasted_iota(jnp.int32, (blk, 2 * blk), 0)
    kj = lax.broadcasted_iota(jnp.int32, (blk, 2 * blk), 1)
    delta = qi + blk - kj
    valid = (delta >= 0) & (delta <= blk) & ((i > 0) | (kj >= blk))
    zero = jnp.zeros((), BF16)
    out = None
    lse_full = jnp.zeros((blk, gw), F32)
    for h in range(GROUP_HEADS):
        hm = _head_mask((blk, gw), h, DIL_HEAD_DIM)
        qh = jnp.where(hm, q, zero)
        s = jnp.where(valid, _dot_nt(qh, kcat) * (DIL_HEAD_DIM ** -0.5), NEG_BIG)
        m = jnp.max(s, axis=-1, keepdims=True)
        e = jnp.exp(s - m)
        l = jnp.sum(e, axis=-1, keepdims=True)
        p = (e / l).astype(BF16)
        vh = jnp.where(_head_mask(vcat.shape, h, DIL_HEAD_DIM), vcat, zero)
        o = _dot(p, vh)
        out = o if out is None else out + o
        lse_full = jnp.where(hm, m + jnp.log(l), lse_full)
    o_ref[0, 0] = out.astype(o_ref.dtype)
    lse_ref[0, 0] = lse_full


def _dilated(q, kv, dil):
    batch, _, m, gw = q.shape
    nb = m // DIL_BLOCK
    idx = lambda b, r, i: (b, r, i, 0)
    return pl.pallas_call(
        _dilated_kernel,
        grid=(batch, dil, nb),
        in_specs=[pl.BlockSpec((1, 1, DIL_BLOCK, gw), idx), pl.BlockSpec((1, 1, DIL_BLOCK, 2 * gw), idx)],
        out_specs=[pl.BlockSpec((1, 1, DIL_BLOCK, gw), idx)] * 2,
        out_shape=[jax.ShapeDtypeStruct(q.shape, BF16), jax.ShapeDtypeStruct(q.shape, F32)],
        scratch_shapes=[pltpu.VMEM((2, DIL_BLOCK, 2 * gw), BF16)],
        compiler_params=_params(("arbitrary", "arbitrary", "arbitrary")),
        name=f"dilated_d{dil}",
    )(q, kv)


def _rope_tables(pos, head_dim):
    inv = ROPE_THETA ** (-jnp.arange(0, head_dim, 2, dtype=F32) / head_dim)
    ang = pos[:, None] * inv[None, :]
    cos, sin = jnp.cos(ang), jnp.sin(ang)
    cos_f = jnp.concatenate([cos, cos], axis=-1)
    sin_f = jnp.concatenate([-sin, sin], axis=-1)
    reps = LANES // head_dim
    return (jnp.tile(cos_f, (1, reps)), jnp.tile(sin_f, (1, reps))), (cos_f.T, sin_f.T)


def _to_feature_major(x5):
    b, w = x5.shape[0], x5.shape[1]
    return jnp.transpose(x5, (0, 2, 3, 4, 1)).reshape(b, 2 * GROUP_WIDTH, w)


def _from_feature_major(xt):
    b, _, w = xt.shape
    return jnp.transpose(xt.reshape(b, 2, GROUP_HEADS, DIL_HEAD_DIM, w), (0, 4, 1, 2, 3))


def kernel(x_prompt, x_sample, mem_prompt, cache_mem_kv, state_ret, cache_win_kv_g1, cache_win_kv_g2, cache_win_kv_g3, w_in_a, w_in_b, w_out, w_kv_shared, w_mem_kv, ln_mix_g, ln_mix_b, ln_ffn_g, ln_ffn_b, w_ffn_in, w_ffn_out):
    batch, seq, d = x_prompt.shape
    dec_batch, dec_seq, _ = x_sample.shape
    n_mem = mem_prompt.shape[1]
    gw = GROUP_WIDTH
    mw = MAIN_WIDTH
    tm_p = 512
    tm_s = dec_batch * dec_seq
    win_caches = (cache_win_kv_g1, cache_win_kv_g2, cache_win_kv_g3)
    dils = tuple(dl for _, dl in DIL_PAIRS)

    pos_p = jnp.arange(seq, dtype=F32)
    pos_s = jnp.tile(PAST_LEN + jnp.arange(dec_seq, dtype=F32), dec_batch)
    tab_a_p, _ = _rope_tables(pos_p, RET_HEAD_DIM)
    tab_a_s, _ = _rope_tables(pos_s, RET_HEAD_DIM)
    tab_b_p, tab_bt_p = _rope_tables(pos_p, DIL_HEAD_DIM)
    tab_b_s, tab_bt_s = _rope_tables(pos_s, DIL_HEAD_DIM)

    w_a = w_in_a.astype(BF16)
    kv_cols = jnp.concatenate([w_kv_shared[:, c0:c0 + gw] for g in range(3) for c0 in (g * gw, mw + g * gw)], axis=1)
    w_b = jnp.concatenate([kv_cols, w_in_b[0]], axis=1).astype(BF16)
    w_bt = kv_cols.T.astype(BF16)
    w_o = w_out.astype(BF16)
    w_fi = w_ffn_in.astype(BF16)
    w_fo = w_ffn_out.astype(BF16)
    w_mem_t = jnp.transpose(w_mem_kv, (0, 2, 1)).astype(BF16)
    lnv = lambda a: a.reshape(DEPTH, 1, d)
    ln = (lnv(ln_mix_g), lnv(ln_mix_b), lnv(ln_ffn_g), lnv(ln_ffn_b))

    memkv_p = _mem_proj(mem_prompt, w_mem_t)
    memkv_s = jnp.transpose(cache_mem_kv, (0, 1, 3, 4, 5, 2)).reshape(DEPTH, dec_batch, 2 * MEM_WIDTH, n_mem)

    xp = x_prompt.reshape(batch * seq, d)
    xs = x_sample.reshape(tm_s, d)

    def post(l, mix_inputs, x2d, tm, dls, rows):
        return _post(l, mix_inputs, x2d, w_o, ln[0], ln[1], w_fi, w_fo, ln[2], ln[3], tm, dls, rows)

    q, k, v, g, qm = _in_a(xp, w_a, *tab_a_p, tm_p, BF16)
    mix_p, state_p = _retention(q, k, v, g, qm, memkv_p, batch, seq)
    xp = post(0, [mix_p], xp, tm_p, None, seq)

    q, k, v, g, qm = _in_a(xs, w_a, *tab_a_s, tm_s, F32)
    mix_s, state_s = _retention_sample(q, k, v, g, qm, memkv_s, state_ret, dec_batch, dec_seq)
    xs = post(0, [mix_s], xs, tm_s, None, tm_s)

    windows_p = tuple(min(w, seq) for w, _ in DIL_PAIRS)
    outs = _in_b(xp, w_b, w_bt, tab_b_p, tab_bt_p, tm_p, BF16, batch, seq, dils, windows_p)
    q_g, kv_g, qm, win_p = outs[0:3], outs[3:6], outs[6], outs[7:10]
    att = [_dilated(q_g[i], kv_g[i], dils[i]) for i in range(3)]
    mem_o = _mem_attn(qm, memkv_p, 1, tm_p, seq, BF16)
    xp = post(1, [a[0] for a in att] + [a[1] for a in att] + [mem_o], xp, tm_p, dils, seq)

    ones = (1, 1, 1)
    outs = _in_b(xs, w_b, w_bt, tab_b_s, tab_bt_s, tm_s, F32, 1, tm_s, ones, (tm_s,) * 3)
    q_g, kv_new, qm, new_t = outs[0:3], outs[3:6], outs[6], outs[7:10]
    att = []
    for i, (window, dil) in enumerate(DIL_PAIRS):
        nt_i = jnp.transpose(new_t[i].reshape(2 * gw, dec_batch, dec_seq), (1, 0, 2))
        att.append(_dilated_sample(q_g[i].reshape(tm_s, gw), _to_feature_major(win_caches[i]),
                                   kv_new[i].reshape(tm_s, 2 * gw), nt_i, window, dil, dec_batch, dec_seq))
    mem_o = _mem_attn(qm, memkv_s, 1, dec_seq, dec_seq, F32)
    r4 = lambda a: a.reshape(1, 1, tm_s, gw)
    xs = post(1, [r4(a[0]) for a in att] + [r4(a[1]) for a in att] + [mem_o], xs, tm_s, ones, tm_s)

    memkv_out = jnp.transpose(memkv_p.reshape(DEPTH, batch, 2, MEM_HEADS, MEM_HEAD_DIM, n_mem), (0, 1, 5, 2, 3, 4))
    return (xp.reshape(batch, seq, d), xs.reshape(dec_batch, dec_seq, d),
            state_p[None], state_s[None], memkv_out,
            _from_feature_major(win_p[0]), _from_feature_major(win_p[1]), _from_feature_major(win_p[2]),
            _from_feature_major(att[0][2]), _from_feature_major(att[1][2]), _from_feature_major(att[2][2]))
```

```python
import functools
import math

import jax
import jax.numpy as jnp
from jax import lax
from jax.experimental import pallas as pl
from jax.experimental.pallas import tpu as pltpu

F32 = jnp.float32
BF16 = jnp.bfloat16

D_MODEL = 1024
MEM_HEADS = 4
MEM_HEAD_DIM = 64
MEM_WIDTH = MEM_HEADS * MEM_HEAD_DIM
MAIN_WIDTH = D_MODEL - MEM_WIDTH
RET_HEADS = 6
RET_HEAD_DIM = MAIN_WIDTH // RET_HEADS
RET_CHUNK = 128
RET_CHUNKS_PER_STEP = 4
DIL_PAIRS = ((128, 1), (512, 4), (2048, 16))
GROUP_HEADS = 4
DIL_HEAD_DIM = 64
GROUP_WIDTH = GROUP_HEADS * DIL_HEAD_DIM
DIL_BLOCK = 128
DIL_BLOCKS_PER_STEP = 4
FFN_HIDDEN = 2816
ROPE_THETA = 10000.0
LN_EPS = 1e-5
DEPTH = 2
ALPHA = (2 * DEPTH) ** 0.25
PAST_LEN = 8192
NEG_BIG = -1e30
ATTN_SCALE = DIL_HEAD_DIM ** -0.5

LANES = 128
VMEM_LIMIT = 56 * 1024 * 1024
FFN_COL_CHUNK = 256

LOG_G = tuple(math.log1p(-(2.0 ** (-5.0 - h))) for h in range(RET_HEADS))


def _dot(a, b):
    return jnp.dot(a, b, preferred_element_type=F32)


def _dot_nt(a, b):
    return lax.dot_general(a, b, (((1,), (1,)), ((), ())), preferred_element_type=F32)


def _dot_tn(a, b):
    return lax.dot_general(a, b, (((0,), (0,)), ((), ())), preferred_element_type=F32)


def _silu(x):
    return x / (1.0 + jnp.exp(-x))


def _layer_norm(z, g, b):
    mu = jnp.mean(z, axis=-1, keepdims=True)
    zc = z - mu
    var = jnp.mean(zc * zc, axis=-1, keepdims=True)
    return zc * lax.rsqrt(var + LN_EPS) * g + b


def _rope_lanes(y, cos, sin_signed, half):
    if 2 * half == LANES:
        partner = pltpu.roll(y, half, 1)
    else:
        lane = lax.broadcasted_iota(jnp.int32, y.shape, 1)
        first = (lane & (2 * half - 1)) < half
        partner = jnp.where(first, pltpu.roll(y, LANES - half, 1), pltpu.roll(y, half, 1))
    return y * cos + partner * sin_signed


def _rope_cols(y, cos, sin_signed, half):
    parts = [_rope_lanes(y[:, j:j + LANES], cos, sin_signed, half) for j in range(0, y.shape[1], LANES)]
    return parts[0] if len(parts) == 1 else jnp.concatenate(parts, axis=1)


def _rope_rows(yt, cos_t, sin_t, head_dim):
    half = head_dim // 2
    parts = []
    for r0 in range(0, yt.shape[0], head_dim):
        blk = yt[r0:r0 + head_dim]
        swapped = jnp.concatenate([blk[half:], blk[:half]], axis=0)
        parts.append(blk * cos_t + swapped * sin_t)
    return jnp.concatenate(parts, axis=0)


def _head_mask(shape, h, width):
    lane = lax.broadcasted_iota(jnp.int32, shape, len(shape) - 1)
    return (lane >= h * width) & (lane < (h + 1) * width)


def _stack_heads(q, width):
    zero = jnp.zeros((), q.dtype)
    parts = [jnp.where(_head_mask(q.shape, h, width), q, zero) for h in range(q.shape[1] // width)]
    return jnp.concatenate(parts, axis=0).astype(BF16)


def _pick_heads(full, t, width):
    out = None
    for h in range(full.shape[1] // width):
        blk = full[h * t:(h + 1) * t, :]
        piece = jnp.where(_head_mask(blk.shape, h, width), blk, 0.0)
        out = piece if out is None else out + piece
    return out


def _const_spec(block, index):
    return pl.BlockSpec(block, lambda *_: index, pipeline_mode=pl.Buffered(1))


def _params(sem):
    return pltpu.CompilerParams(dimension_semantics=sem, vmem_limit_bytes=VMEM_LIMIT)


def _mem_proj_kernel(m_ref, w_ref, o_ref):
    o_ref[0, 0] = _dot_nt(w_ref[0], m_ref[0].astype(BF16))


def _mem_proj(mem, w_t):
    batch, n_mem, d = mem.shape
    depth, n, _ = w_t.shape
    return pl.pallas_call(
        _mem_proj_kernel,
        grid=(depth, batch),
        in_specs=[pl.BlockSpec((1, n_mem, d), lambda l, b: (b, 0, 0)),
                  pl.BlockSpec((1, n, d), lambda l, b: (l, 0, 0))],
        out_specs=pl.BlockSpec((1, 1, n, n_mem), lambda l, b: (l, b, 0, 0)),
        out_shape=jax.ShapeDtypeStruct((depth, batch, n, n_mem), F32),
        compiler_params=_params(("arbitrary", "arbitrary")),
        name="mem_proj",
    )(mem, w_t)


def _mem_attn_tile(qm, kv_t):
    t = qm.shape[0]
    q4 = _stack_heads(qm, MEM_HEAD_DIM)
    k_t = kv_t[:MEM_WIDTH].astype(BF16)
    v_t = kv_t[MEM_WIDTH:].astype(BF16)
    s = _dot(q4, k_t)
    m = jnp.max(s, axis=-1, keepdims=True)
    e = jnp.exp(s - m)
    p = (e / jnp.sum(e, axis=-1, keepdims=True)).astype(BF16)
    return _pick_heads(_dot_nt(p, v_t), t, MEM_HEAD_DIM)


def _mem_attn_kernel(q_ref, kv_ref, o_ref):
    o_ref[...] = _mem_attn_tile(q_ref[...], kv_ref[0, 0]).astype(o_ref.dtype)


def _mem_attn(qm, memkv_t, layer, tq, rows_per_batch, out_dtype):
    t = qm.shape[0]
    per = rows_per_batch // tq
    return pl.pallas_call(
        _mem_attn_kernel,
        grid=(t // tq,),
        in_specs=[pl.BlockSpec((tq, MEM_WIDTH), lambda i: (i, 0)),
                  pl.BlockSpec((1, 1) + memkv_t.shape[2:], lambda i: (layer, i // per, 0, 0))],
        out_specs=pl.BlockSpec((tq, MEM_WIDTH), lambda i: (i, 0)),
        out_shape=jax.ShapeDtypeStruct((t, MEM_WIDTH), out_dtype),
        compiler_params=_params(("parallel",)),
        name="mem_attn",
    )(qm, memkv_t)


def _in_a_kernel(x_ref, w_ref, cos_ref, sin_ref, *refs, with_mem):
    if with_mem:
        mkv_ref, q_ref, k_ref, v_ref, g_ref, m_ref = refs
    else:
        q_ref, k_ref, v_ref, g_ref, m_ref = refs
    xb = x_ref[...].astype(BF16)
    cos = cos_ref[...]
    sin = sin_ref[...]
    mw = MAIN_WIDTH
    half = RET_HEAD_DIM // 2
    q = _rope_cols(_dot(xb, w_ref[0, :, 0:mw]), cos, sin, half)
    q_ref[...] = q.astype(q_ref.dtype)
    k = _rope_cols(_dot(xb, w_ref[0, :, mw:2 * mw]), cos, sin, half) * (RET_HEAD_DIM ** -0.5)
    k_ref[...] = k.astype(k_ref.dtype)
    v_ref[...] = _dot(xb, w_ref[0, :, 2 * mw:3 * mw]).astype(v_ref.dtype)
    g_ref[...] = _silu(_dot(xb, w_ref[0, :, 3 * mw:4 * mw])).astype(g_ref.dtype)
    qm = _dot(xb, w_ref[0, :, 4 * mw:4 * mw + MEM_WIDTH]) * ATTN_SCALE
    if with_mem:
        m_ref[...] = _mem_attn_tile(qm.astype(BF16), mkv_ref[0, 0]).astype(m_ref.dtype)
    else:
        m_ref[...] = qm.astype(m_ref.dtype)


def _in_a(x2d, w_bf, cos, sin, tm, out_dtype, memkv_t=None, rows_per_batch=None):
    t, d = x2d.shape
    ntab = cos.shape[0] // tm
    row = lambda i: (i, 0)
    tab = lambda i: (i % ntab, 0)
    widths = (MAIN_WIDTH,) * 4 + (MEM_WIDTH,)
    in_specs = [pl.BlockSpec((tm, d), row), _const_spec((1,) + w_bf.shape[1:], (0, 0, 0)),
                pl.BlockSpec((tm, LANES), tab), pl.BlockSpec((tm, LANES), tab)]
    args = [x2d, w_bf, cos, sin]
    if memkv_t is not None:
        per = rows_per_batch // tm
        in_specs.append(pl.BlockSpec((1, 1) + memkv_t.shape[2:], lambda i: (0, i // per, 0, 0)))
        args.append(memkv_t)
    return pl.pallas_call(
        functools.partial(_in_a_kernel, with_mem=memkv_t is not None),
        grid=(t // tm,),
        in_specs=in_specs,
        out_specs=[pl.BlockSpec((tm, w), row) for w in widths],
        out_shape=[jax.ShapeDtypeStruct((t, w), out_dtype) for w in widths],
        compiler_params=_params(("parallel",)),
        name="in_proj_a",
    )(*args)


def _retention_kernel(q_ref, k_ref, v_ref, g_ref, mix_ref, st_ref, dec_ref, rdec_ref, kdec_ref):
    c = pl.program_id(1)
    cs = RET_CHUNK

    @pl.when(c == 0)
    def _():
        st_ref[...] = jnp.zeros_like(st_ref)
        row = lax.broadcasted_iota(jnp.int32, (cs, cs), 0).astype(F32)
        col = lax.broadcasted_iota(jnp.int32, (cs, cs), 1).astype(F32)
        diff = row - col
        for h in range(RET_HEADS):
            lg = LOG_G[h]
            dec_ref[h] = jnp.where(diff >= 0, jnp.exp(jnp.maximum(diff, 0.0) * lg), 0.0)
            rdec_ref[h] = jnp.exp((row + 1.0) * lg)
            kdec_ref[h] = jnp.exp((cs - 1.0 - row) * lg)

    for j in range(q_ref.shape[0] // cs):
        rs = slice(j * cs, (j + 1) * cs)
        for h in range(RET_HEADS):
            hs = slice(h * RET_HEAD_DIM, (h + 1) * RET_HEAD_DIM)
            qh = q_ref[rs, hs]
            kh = k_ref[rs, hs]
            vh = v_ref[rs, hs]
            st = st_ref[0, h]
            s = _dot_nt(qh, kh) * dec_ref[h]
            inner = _dot(s.astype(BF16), vh)
            cross = _dot(qh, st.astype(BF16)) * rdec_ref[h]
            kd = (kh.astype(F32) * kdec_ref[h]).astype(BF16)
            st_ref[0, h] = math.exp(cs * LOG_G[h]) * st + _dot_tn(kd, vh)
            o = inner + cross
            mu = jnp.mean(o, axis=-1, keepdims=True)
            oc = o - mu
            var = jnp.mean(oc * oc, axis=-1, keepdims=True)
            on = oc * lax.rsqrt(var + LN_EPS)
            mix_ref[rs, hs] = (g_ref[rs, hs].astype(F32) * on).astype(mix_ref.dtype)


def _retention(q, k, v, g, batch, seq):
    t = q.shape[0]
    rows = RET_CHUNK * RET_CHUNKS_PER_STEP
    ns = seq // rows
    tok = lambda b, c: (b * ns + c, 0)
    sq = (RET_HEADS, RET_CHUNK, RET_CHUNK)
    return pl.pallas_call(
        _retention_kernel,
        grid=(batch, ns),
        in_specs=[pl.BlockSpec((rows, MAIN_WIDTH), tok)] * 4,
        out_specs=[pl.BlockSpec((rows, MAIN_WIDTH), tok),
                   pl.BlockSpec((1, RET_HEADS, RET_HEAD_DIM, RET_HEAD_DIM), lambda b, c: (b, 0, 0, 0))],
        out_shape=[jax.ShapeDtypeStruct((t, MAIN_WIDTH), BF16),
                   jax.ShapeDtypeStruct((batch, RET_HEADS, RET_HEAD_DIM, RET_HEAD_DIM), F32)],
        scratch_shapes=[pltpu.VMEM(sq, F32), pltpu.VMEM(sq, F32), pltpu.VMEM(sq, F32)],
        compiler_params=_params(("arbitrary", "arbitrary")),
        name="retention",
    )(q, k, v, g)


def _retention_sample_kernel(q_ref, k_ref, v_ref, g_ref, qm_ref, mkv_ref, st_ref, mix_ref, nst_ref):
    t = q_ref.shape[0]
    pad = jnp.zeros((LANES - t, RET_HEAD_DIM), F32)
    row = lax.broadcasted_iota(jnp.int32, (t, LANES), 0).astype(F32)
    col = lax.broadcasted_iota(jnp.int32, (t, LANES), 1).astype(F32)
    prow = lax.broadcasted_iota(jnp.int32, (LANES, RET_HEAD_DIM), 0).astype(F32)
    diff = row - col
    for h in range(RET_HEADS):
        lg = LOG_G[h]
        hs = slice(h * RET_HEAD_DIM, (h + 1) * RET_HEAD_DIM)
        qh = q_ref[:, hs]
        kp = jnp.concatenate([k_ref[:, hs], pad], axis=0)
        vp = jnp.concatenate([v_ref[:, hs], pad], axis=0)
        st = st_ref[0, 0, h]
        dec = jnp.where(diff >= 0, jnp.exp(jnp.maximum(diff, 0.0) * lg), 0.0)
        inner = _dot(_dot_nt(qh, kp) * dec, vp)
        cross = _dot(qh, st) * jnp.exp((row + 1.0) * lg)
        kd = kp * jnp.exp((t - 1.0 - prow) * lg)
        nst_ref[0, h] = math.exp(t * lg) * st + _dot_tn(kd, vp)
        o = inner + cross
        mu = jnp.mean(o, axis=-1, keepdims=True)
        oc = o - mu
        var = jnp.mean(oc * oc, axis=-1, keepdims=True)
        on = oc * lax.rsqrt(var + LN_EPS)
        mix_ref[:, hs] = g_ref[:, hs] * on
    mix_ref[:, MAIN_WIDTH:] = _mem_attn_tile(qm_ref[...], mkv_ref[0, 0])


def _retention_sample(q, k, v, g, qm, memkv_t, state, batch, t):
    tok = lambda b: (b, 0)
    hd = (RET_HEADS, RET_HEAD_DIM, RET_HEAD_DIM)
    return pl.pallas_call(
        _retention_sample_kernel,
        grid=(batch,),
        in_specs=[pl.BlockSpec((t, MAIN_WIDTH), tok)] * 4
                 + [pl.BlockSpec((t, MEM_WIDTH), tok),
                    pl.BlockSpec((1, 1) + memkv_t.shape[2:], lambda b: (0, b, 0, 0)),
                    pl.BlockSpec((1, 1) + hd, lambda b: (0, b, 0, 0, 0))],
        out_specs=[pl.BlockSpec((t, D_MODEL), tok), pl.BlockSpec((1,) + hd, lambda b: (b, 0, 0, 0))],
        out_shape=[jax.ShapeDtypeStruct((batch * t, D_MODEL), F32),
                   jax.ShapeDtypeStruct((batch,) + hd, F32)],
        compiler_params=_params(("parallel",)),
        name="retention_sample",
    )(q, k, v, g, qm, memkv_t, state)


def _dilated_sample_kernel(q_ref, cache_ref, new_ref, new_t_ref, o_ref, lse_ref, win_ref, *, window, dil):
    t = q_ref.shape[0]
    n_buf = cache_ref.shape[2]
    gw = GROUP_WIDTH
    q4 = _stack_heads(q_ref[...], DIL_HEAD_DIM)
    rows = q4.shape[0]
    cache = cache_ref[0]
    new = new_ref[...]
    newp = jnp.concatenate([new, jnp.zeros((LANES - t, new.shape[1]), F32)], axis=0).astype(BF16)

    def masked(s, first_index):
        key = lax.broadcasted_iota(jnp.int32, s.shape, 1) + first_index
        tok = lax.broadcasted_iota(jnp.int32, s.shape, 0) & (t - 1)
        delta = n_buf + tok - key
        valid = (delta >= 0) & (delta <= window) & ((delta & (dil - 1)) == 0)
        return jnp.where(valid, s, NEG_BIG)

    s_c = masked(_dot(q4, cache[:gw].astype(BF16)), 0)
    s_n = masked(_dot_nt(q4, newp[:, :gw]), n_buf)
    m = jnp.maximum(jnp.max(s_c, axis=-1, keepdims=True), jnp.max(s_n, axis=-1, keepdims=True))
    e_c = jnp.exp(s_c - m)
    e_n = jnp.exp(s_n - m)
    l = jnp.sum(e_c, axis=-1, keepdims=True) + jnp.sum(e_n, axis=-1, keepdims=True)
    inv = 1.0 / l
    full = (_dot_nt((e_c * inv).astype(BF16), cache[gw:].astype(BF16))
            + _dot((e_n * inv).astype(BF16), newp[:, gw:]))
    o_ref[...] = _pick_heads(full, t, DIL_HEAD_DIM)
    lse_ref[...] = _pick_heads(jnp.broadcast_to(m + jnp.log(l), (rows, gw)), t, DIL_HEAD_DIM)
    win_ref[0, :, 0:n_buf - t] = cache[:, t:]
    win_ref[0, :, n_buf - t:] = new_t_ref[0]


def _dilated_sample(q, cache_t, new_kv, new_kv_t, window, dil, batch, t):
    n_buf = cache_t.shape[2]
    tok = lambda b: (b, 0)
    big = lambda b: (b, 0, 0)
    return pl.pallas_call(
        functools.partial(_dilated_sample_kernel, window=window, dil=dil),
        grid=(batch,),
        in_specs=[pl.BlockSpec((t, GROUP_WIDTH), tok),
                  pl.BlockSpec((1, 2 * GROUP_WIDTH, n_buf), big),
                  pl.BlockSpec((t, 2 * GROUP_WIDTH), tok),
                  pl.BlockSpec((1, 2 * GROUP_WIDTH, t), big)],
        out_specs=[pl.BlockSpec((t, GROUP_WIDTH), tok), pl.BlockSpec((t, GROUP_WIDTH), tok),
                   pl.BlockSpec((1, 2 * GROUP_WIDTH, n_buf), big)],
        out_shape=[jax.ShapeDtypeStruct((batch * t, GROUP_WIDTH), F32),
                   jax.ShapeDtypeStruct((batch * t, GROUP_WIDTH), F32),
                   jax.ShapeDtypeStruct(cache_t.shape, F32)],
        compiler_params=_params(("parallel",)),
        name=f"dilated_sample_w{window}",
    )(q, cache_t, new_kv, new_kv_t)


def _natural_rows(ref, dil, scr):
    if dil == 1:
        return ref[0, 0].astype(F32)
    n = ref.shape[2]
    for r in range(dil):
        v = ref[0, r].astype(F32)
        for c in range(v.shape[1] // LANES):
            scr[c, pl.ds(r, n, stride=dil), :] = v[:, c * LANES:(c + 1) * LANES]
    return jnp.concatenate([scr[c] for c in range(scr.shape[0])], axis=1)


def _post_kernel(*refs, dils, n_mix):
    combine = dils is not None
    if combine:
        o_refs, l_refs, mem_ref = refs[0:3], refs[3:6], refs[6]
        rest = refs[7:]
    else:
        mix_refs = refs[0:n_mix]
        rest = refs[n_mix:]
    x_ref, wo_ref, g1_ref, b1_ref, wi_ref, w2_ref, g2_ref, b2_ref, out_ref, act_ref = rest[:10]
    scrs = list(rest[10:])

    if combine:
        def nat(r, d):
            return _natural_rows(r, d, scrs.pop(0) if d > 1 else None)
        os_ = [nat(o_refs[g], dils[g]) for g in range(3)]
        ls = [nat(l_refs[g], dils[g]) for g in range(3)]
        m = jnp.maximum(jnp.maximum(ls[0], ls[1]), ls[2])
        es = [jnp.exp(v - m) for v in ls]
        inv = 1.0 / (es[0] + es[1] + es[2])
        parts = [(os_[g] * (es[g] * inv)).astype(BF16) for g in range(3)]
        parts.append(mem_ref[...].astype(BF16))
    else:
        parts = [r[...].astype(BF16) for r in mix_refs]
    mix = parts[0] if len(parts) == 1 else jnp.concatenate(parts, axis=1)

    x1 = _layer_norm(ALPHA * x_ref[...] + _dot(mix, wo_ref[0]), g1_ref[0], b1_ref[0])
    x1b = x1.astype(BF16)
    for c in range(0, FFN_HIDDEN, FFN_COL_CHUNK):
        gate = _dot(x1b, wi_ref[0, :, c:c + FFN_COL_CHUNK])
        up = _dot(x1b, wi_ref[0, :, FFN_HIDDEN + c:FFN_HIDDEN + c + FFN_COL_CHUNK])
        act_ref[:, c:c + FFN_COL_CHUNK] = (_silu(gate) * up).astype(BF16)
    y = _dot(act_ref[...], w2_ref[0])
    out_ref[...] = _layer_norm(ALPHA * x1 + y, g2_ref[0], b2_ref[0])


def _post(layer, mix_inputs, x2d, wo, g1, b1, wi, w2, g2, b2, tm, dils, rows_per_batch):
    t, d = x2d.shape
    row = lambda i: (i, 0)
    nt = rows_per_batch // tm
    if dils is None:
        mix_specs = [pl.BlockSpec((tm, a.shape[1]), row) for a in mix_inputs]
        scratch = []
    else:
        res = lambda i: (i // nt, 0, i % nt, 0)
        mix_specs = [pl.BlockSpec((1, dl, tm // dl, GROUP_WIDTH), res) for dl in dils] * 2
        mix_specs.append(pl.BlockSpec((tm, MEM_WIDTH), row))
        scratch = [pltpu.VMEM((GROUP_WIDTH // LANES, tm, LANES), F32) for dl in dils * 2 if dl > 1]
    lsel = (layer, 0, 0)
    vec = _const_spec((1, 1, d), lsel)
    return pl.pallas_call(
        functools.partial(_post_kernel, dils=dils, n_mix=len(mix_inputs)),
        grid=(t // tm,),
        in_specs=mix_specs + [pl.BlockSpec((tm, d), row), _const_spec((1,) + wo.shape[1:], lsel), vec, vec,
                              _const_spec((1,) + wi.shape[1:], lsel), _const_spec((1,) + w2.shape[1:], lsel),
                              vec, vec],
        out_specs=pl.BlockSpec((tm, d), row),
        out_shape=jax.ShapeDtypeStruct((t, d), F32),
        scratch_shapes=[pltpu.VMEM((tm, FFN_HIDDEN), BF16)] + scratch,
        compiler_params=_params(("parallel",)),
        name="post" if dils is None else "post_combine",
    )(*mix_inputs, x2d, wo, g1, b1, wi, w2, g2, b2)


def _residue_major(ref, y, dil, scr):
    if dil == 1:
        ref[0, 0] = y.astype(ref.dtype)
        return
    n = ref.shape[2]
    ncol = y.shape[1] // LANES
    for c in range(ncol):
        scr[c] = y[:, c * LANES:(c + 1) * LANES]
    for r in range(dil):
        rows = [scr[c, pl.ds(r, n, stride=dil), :] for c in range(ncol)]
        ref[0, r] = jnp.concatenate(rows, axis=1).astype(ref.dtype)


def _in_b_kernel(x_ref, w_ref, wt_ref, cos_ref, sin_ref, cos_t_ref, sin_t_ref, *refs,
                 dils, win_tiles, nt, with_mem):
    if with_mem:
        mkv_ref, refs = refs[0], refs[1:]
    xb = x_ref[...].astype(BF16)
    cos = cos_ref[...]
    sin = sin_ref[...]
    gw = GROUP_WIDTH
    half = DIL_HEAD_DIM // 2
    q_refs, kv_refs, m_ref, win_refs = refs[0:3], refs[3:6], refs[6], refs[7:10]
    scrs = list(refs[10:])
    tm = x_ref.shape[0]
    j = pl.program_id(0) % nt
    for g in range(3):
        k = _rope_cols(_dot(xb, w_ref[:, 2 * g * gw:(2 * g + 1) * gw]), cos, sin, half)
        v = _dot(xb, w_ref[:, (2 * g + 1) * gw:(2 * g + 2) * gw])
        _residue_major(kv_refs[g], jnp.concatenate([k, v], axis=1), dils[g], scrs.pop(0) if dils[g] > 1 else None)
        q = _rope_cols(_dot(xb, w_ref[:, (6 + g) * gw:(7 + g) * gw]), cos, sin, half) * ATTN_SCALE
        _residue_major(q_refs[g], q, dils[g], scrs.pop(0) if dils[g] > 1 else None)

        @pl.when(j >= nt - win_tiles[g])
        def _(g=g):
            kv_t = _dot_nt(wt_ref[2 * g * gw:(2 * g + 2) * gw, :], xb)
            k_t = _rope_rows(kv_t[:gw], cos_t_ref[...], sin_t_ref[...], DIL_HEAD_DIM)
            full = jnp.concatenate([k_t, kv_t[gw:]], axis=0)
            win_refs[g][0] = full[:, tm - win_refs[g].shape[2]:]
    qm = _dot(xb, w_ref[:, 9 * gw:10 * gw]) * ATTN_SCALE
    if with_mem:
        m_ref[...] = _mem_attn_tile(qm.astype(BF16), mkv_ref[0, 0]).astype(m_ref.dtype)
    else:
        m_ref[...] = qm.astype(m_ref.dtype)


def _in_b(x2d, w_bf, wt_bf, tabs, tabs_t, tm, out_dtype, batch, seq, dils, windows, memkv_t=None):
    t, d = x2d.shape
    gw = GROUP_WIDTH
    nt = seq // tm
    ntab = tabs[0].shape[0] // tm
    row = lambda i: (i, 0)
    tab = lambda i: (i % ntab, 0)
    tab_t = lambda i: (0, i % ntab)
    res = lambda i: (i // nt, 0, i % nt, 0)
    out_specs = ([pl.BlockSpec((1, dl, tm // dl, gw), res) for dl in dils]
                 + [pl.BlockSpec((1, dl, tm // dl, 2 * gw), res) for dl in dils]
                 + [pl.BlockSpec((tm, MEM_WIDTH), row)])
    out_shape = ([jax.ShapeDtypeStruct((batch, dl, seq // dl, gw), out_dtype) for dl in dils]
                 + [jax.ShapeDtypeStruct((batch, dl, seq // dl, 2 * gw), out_dtype) for dl in dils]
                 + [jax.ShapeDtypeStruct((t, MEM_WIDTH), out_dtype)])
    win_tiles = []
    for w in windows:
        rb = min(w, tm)
        nblk = w // rb
        win_tiles.append(nblk)
        out_specs.append(pl.BlockSpec(
            (1, 2 * gw, rb), lambda i, nblk=nblk: (i // nt, 0, jnp.maximum(i % nt - (nt - nblk), 0))))
        out_shape.append(jax.ShapeDtypeStruct((batch, 2 * gw, w), F32))
    scratch = []
    for dl in dils:
        if dl > 1:
            scratch += [pltpu.VMEM((2 * gw // LANES, tm, LANES), F32), pltpu.VMEM((gw // LANES, tm, LANES), F32)]
    in_specs = [pl.BlockSpec((tm, d), row), _const_spec(w_bf.shape, (0, 0)), _const_spec(wt_bf.shape, (0, 0)),
                pl.BlockSpec((tm, LANES), tab), pl.BlockSpec((tm, LANES), tab),
                pl.BlockSpec((DIL_HEAD_DIM, tm), tab_t), pl.BlockSpec((DIL_HEAD_DIM, tm), tab_t)]
    args = [x2d, w_bf, wt_bf, *tabs, *tabs_t]
    if memkv_t is not None:
        in_specs.append(pl.BlockSpec((1, 1) + memkv_t.shape[2:], lambda i: (1, i // nt, 0, 0)))
        args.append(memkv_t)
    return pl.pallas_call(
        functools.partial(_in_b_kernel, dils=tuple(dils), win_tiles=tuple(win_tiles), nt=nt,
                          with_mem=memkv_t is not None),
        grid=(t // tm,),
        in_specs=in_specs,
        out_specs=out_specs,
        out_shape=out_shape,
        scratch_shapes=scratch,
        compiler_params=_params(("arbitrary",)),
        name="in_proj_b",
    )(*args)


def _dilated_kernel(q_ref, kv_ref, o_ref, lse_ref, ring_ref):
    i = pl.program_id(2)
    blk = DIL_BLOCK
    gw = GROUP_WIDTH
    nsub = q_ref.shape[2] // blk
    slot = i & 1

    @pl.when(i == 0)
    def _():
        ring_ref[1] = jnp.zeros(ring_ref.shape[1:], ring_ref.dtype)

    cur = kv_ref[0, 0]
    ring_ref[slot] = cur[(nsub - 1) * blk:]
    kext = jnp.concatenate([ring_ref[1 - slot], cur], axis=0)
    rows = GROUP_HEADS * blk
    qi = lax.broadcasted_iota(jnp.int32, (rows, 2 * blk), 0) & (blk - 1)
    kj = lax.broadcasted_iota(jnp.int32, (rows, 2 * blk), 1)
    delta = qi + blk - kj
    band = (delta >= 0) & (delta <= blk)
    for j in range(nsub):
        rs = slice(j * blk, (j + 1) * blk)
        q4 = _stack_heads(q_ref[0, 0, rs, :], DIL_HEAD_DIM)
        kv = kext[j * blk:(j + 2) * blk]
        valid = band if j > 0 else band & ((i > 0) | (kj >= blk))
        s = jnp.where(valid, _dot_nt(q4, kv[:, :gw]), NEG_BIG)
        m = jnp.max(s, axis=-1, keepdims=True)
        e = jnp.exp(s - m)
        l = jnp.sum(e, axis=-1, keepdims=True)
        p = (e * (1.0 / l)).astype(BF16)
        full = _dot(p, kv[:, gw:])
        o_ref[0, 0, rs, :] = _pick_heads(full, blk, DIL_HEAD_DIM).astype(o_ref.dtype)
        lse_ref[0, 0, rs, :] = _pick_heads(jnp.broadcast_to(m + jnp.log(l), (rows, gw)), blk, DIL_HEAD_DIM)


def _dilated(q, kv, dil):
    batch, _, m, gw = q.shape
    rows = DIL_BLOCK * min(DIL_BLOCKS_PER_STEP, m // DIL_BLOCK)
    idx = lambda b, r, i: (b, r, i, 0)
    return pl.pallas_call(
        _dilated_kernel,
        grid=(batch, dil, m // rows),
        in_specs=[pl.BlockSpec((1, 1, rows, gw), idx), pl.BlockSpec((1, 1, rows, 2 * gw), idx)],
        out_specs=[pl.BlockSpec((1, 1, rows, gw), idx)] * 2,
        out_shape=[jax.ShapeDtypeStruct(q.shape, BF16), jax.ShapeDtypeStruct(q.shape, F32)],
        scratch_shapes=[pltpu.VMEM((2, DIL_BLOCK, 2 * gw), BF16)],
        compiler_params=_params(("arbitrary", "arbitrary", "arbitrary")),
        name=f"dilated_d{dil}",
    )(q, kv)


def _rope_tables(pos, head_dim):
    inv = ROPE_THETA ** (-jnp.arange(0, head_dim, 2, dtype=F32) / head_dim)
    ang = pos[:, None] * inv[None, :]
    cos, sin = jnp.cos(ang), jnp.sin(ang)
    cos_f = jnp.concatenate([cos, cos], axis=-1)
    sin_f = jnp.concatenate([-sin, sin], axis=-1)
    reps = LANES // head_dim
    return (jnp.tile(cos_f, (1, reps)), jnp.tile(sin_f, (1, reps))), (cos_f.T, sin_f.T)


def _to_feature_major(x5):
    b, w = x5.shape[0], x5.shape[1]
    return jnp.transpose(x5, (0, 2, 3, 4, 1)).reshape(b, 2 * GROUP_WIDTH, w)


def _from_feature_major(xt):
    b, _, w = xt.shape
    return jnp.transpose(xt.reshape(b, 2, GROUP_HEADS, DIL_HEAD_DIM, w), (0, 4, 1, 2, 3))


def kernel(x_prompt, x_sample, mem_prompt, cache_mem_kv, state_ret, cache_win_kv_g1, cache_win_kv_g2, cache_win_kv_g3, w_in_a, w_in_b, w_out, w_kv_shared, w_mem_kv, ln_mix_g, ln_mix_b, ln_ffn_g, ln_ffn_b, w_ffn_in, w_ffn_out):
    batch, seq, d = x_prompt.shape
    dec_batch, dec_seq, _ = x_sample.shape
    n_mem = mem_prompt.shape[1]
    gw = GROUP_WIDTH
    mw = MAIN_WIDTH
    tm_p = 512
    tm_s = dec_batch * dec_seq
    win_caches = (cache_win_kv_g1, cache_win_kv_g2, cache_win_kv_g3)
    dils = tuple(dl for _, dl in DIL_PAIRS)

    pos_p = jnp.arange(seq, dtype=F32)
    pos_s = jnp.tile(PAST_LEN + jnp.arange(dec_seq, dtype=F32), dec_batch)
    tab_a_p, _ = _rope_tables(pos_p, RET_HEAD_DIM)
    tab_a_s, _ = _rope_tables(pos_s, RET_HEAD_DIM)
    tab_b_p, tab_bt_p = _rope_tables(pos_p, DIL_HEAD_DIM)
    tab_b_s, tab_bt_s = _rope_tables(pos_s, DIL_HEAD_DIM)

    w_a = w_in_a.astype(BF16)
    kv_cols = jnp.concatenate([w_kv_shared[:, c0:c0 + gw] for g in range(3) for c0 in (g * gw, mw + g * gw)], axis=1)
    w_b = jnp.concatenate([kv_cols, w_in_b[0]], axis=1).astype(BF16)
    w_bt = kv_cols.T.astype(BF16)
    w_o = w_out.astype(BF16)
    w_fi = w_ffn_in.astype(BF16)
    w_fo = w_ffn_out.astype(BF16)
    w_mem_t = jnp.transpose(w_mem_kv, (0, 2, 1)).astype(BF16)
    lnv = lambda a: a.reshape(DEPTH, 1, d)
    ln = (lnv(ln_mix_g), lnv(ln_mix_b), lnv(ln_ffn_g), lnv(ln_ffn_b))

    memkv_p = _mem_proj(mem_prompt, w_mem_t)
    memkv_s = jnp.transpose(cache_mem_kv, (0, 1, 3, 4, 5, 2)).reshape(DEPTH, dec_batch, 2 * MEM_WIDTH, n_mem)

    xp = x_prompt.reshape(batch * seq, d)
    xs = x_sample.reshape(tm_s, d)

    def post(l, mix_inputs, x2d, tm, dls, rows):
        return _post(l, mix_inputs, x2d, w_o, ln[0], ln[1], w_fi, w_fo, ln[2], ln[3], tm, dls, rows)

    q, k, v, g, mem_o = _in_a(xp, w_a, *tab_a_p, tm_p, BF16, memkv_p, seq)
    mix_p, state_p = _retention(q, k, v, g, batch, seq)
    xp = post(0, [mix_p, mem_o], xp, tm_p, None, seq)

    q, k, v, g, qm = _in_a(xs, w_a, *tab_a_s, tm_s, F32)
    mix_s, state_s = _retention_sample(q, k, v, g, qm, memkv_s, state_ret, dec_batch, dec_seq)
    xs = post(0, [mix_s], xs, tm_s, None, tm_s)

    windows_p = tuple(min(w, seq) for w, _ in DIL_PAIRS)
    outs = _in_b(xp, w_b, w_bt, tab_b_p, tab_bt_p, tm_p, BF16, batch, seq, dils, windows_p, memkv_p)
    q_g, kv_g, mem_o, win_p = outs[0:3], outs[3:6], outs[6], outs[7:10]
    att = [_dilated(q_g[i], kv_g[i], dils[i]) for i in range(3)]
    xp = post(1, [a[0] for a in att] + [a[1] for a in att] + [mem_o], xp, tm_p, dils, seq)

    ones = (1, 1, 1)
    outs = _in_b(xs, w_b, w_bt, tab_b_s, tab_bt_s, tm_s, F32, 1, tm_s, ones, (tm_s,) * 3)
    q_g, kv_new, qm, new_t = outs[0:3], outs[3:6], outs[6], outs[7:10]
    att = []
    for i, (window, dil) in enumerate(DIL_PAIRS):
        nt_i = jnp.transpose(new_t[i].reshape(2 * gw, dec_batch, dec_seq), (1, 0, 2))
        att.append(_dilated_sample(q_g[i].reshape(tm_s, gw), _to_feature_major(win_caches[i]),
                                   kv_new[i].reshape(tm_s, 2 * gw), nt_i, window, dil, dec_batch, dec_seq))
    mem_o = _mem_attn(qm, memkv_s, 1, dec_seq, dec_seq, F32)
    r4 = lambda a: a.reshape(1, 1, tm_s, gw)
    xs = post(1, [r4(a[0]) for a in att] + [r4(a[1]) for a in att] + [mem_o], xs, tm_s, ones, tm_s)

    memkv_out = jnp.transpose(memkv_p.reshape(DEPTH, batch, 2, MEM_HEADS, MEM_HEAD_DIM, n_mem), (0, 1, 5, 2, 3, 4))
    return (xp.reshape(batch, seq, d), xs.reshape(dec_batch, dec_seq, d),
            state_p[None], state_s[None], memkv_out,
            _from_feature_major(win_p[0]), _from_feature_major(win_p[1]), _from_feature_major(win_p[2]),
            _from_feature_major(att[0][2]), _from_feature_major(att[1][2]), _from_feature_major(att[2][2]))
```

```python
import functools
import math

import jax
import jax.numpy as jnp
from jax import lax
from jax.experimental import pallas as pl
from jax.experimental.pallas import tpu as pltpu

F32 = jnp.float32
BF16 = jnp.bfloat16

D_MODEL = 1024
MEM_HEADS = 4
MEM_HEAD_DIM = 64
MEM_WIDTH = MEM_HEADS * MEM_HEAD_DIM
MAIN_WIDTH = D_MODEL - MEM_WIDTH
RET_HEADS = 6
RET_HEAD_DIM = MAIN_WIDTH // RET_HEADS
RET_CHUNK = 128
RET_CHUNKS_PER_STEP = 4
DIL_PAIRS = ((128, 1), (512, 4), (2048, 16))
GROUP_HEADS = 4
DIL_HEAD_DIM = 64
GROUP_WIDTH = GROUP_HEADS * DIL_HEAD_DIM
DIL_BLOCK = 128
DIL_BLOCKS_PER_STEP = 4
FFN_HIDDEN = 2816
ROPE_THETA = 10000.0
LN_EPS = 1e-5
DEPTH = 2
ALPHA = (2 * DEPTH) ** 0.25
PAST_LEN = 8192
NEG_BIG = -1e30
ATTN_SCALE = DIL_HEAD_DIM ** -0.5

LANES = 128
VMEM_LIMIT = 56 * 1024 * 1024
FFN_COL_CHUNK = 256
TOKEN_TILE = 512

LOG_G = tuple(math.log1p(-(2.0 ** (-5.0 - h))) for h in range(RET_HEADS))


def _dot(a, b):
    return jnp.dot(a, b, preferred_element_type=F32)


def _dot_nt(a, b):
    return lax.dot_general(a, b, (((1,), (1,)), ((), ())), preferred_element_type=F32)


def _dot_tn(a, b):
    return lax.dot_general(a, b, (((0,), (0,)), ((), ())), preferred_element_type=F32)


def _silu(x):
    return x / (1.0 + jnp.exp(-x))


def _layer_norm(z, g, b):
    mu = jnp.mean(z, axis=-1, keepdims=True)
    zc = z - mu
    var = jnp.mean(zc * zc, axis=-1, keepdims=True)
    return zc * lax.rsqrt(var + LN_EPS) * g + b


def _rope_lanes(y, cos, sin_signed, half):
    if 2 * half == LANES:
        partner = pltpu.roll(y, half, 1)
    else:
        lane = lax.broadcasted_iota(jnp.int32, y.shape, 1)
        first = (lane & (2 * half - 1)) < half
        partner = jnp.where(first, pltpu.roll(y, LANES - half, 1), pltpu.roll(y, half, 1))
    return y * cos + partner * sin_signed


def _rope_cols(y, cos, sin_signed, half):
    parts = [_rope_lanes(y[:, j:j + LANES], cos, sin_signed, half) for j in range(0, y.shape[1], LANES)]
    return parts[0] if len(parts) == 1 else jnp.concatenate(parts, axis=1)


def _rope_rows(yt, cos_t, sin_t, head_dim):
    half = head_dim // 2
    parts = []
    for r0 in range(0, yt.shape[0], head_dim):
        blk = yt[r0:r0 + head_dim]
        swapped = jnp.concatenate([blk[half:], blk[:half]], axis=0)
        parts.append(blk * cos_t + swapped * sin_t)
    return jnp.concatenate(parts, axis=0)


def _head_mask(shape, h, width):
    lane = lax.broadcasted_iota(jnp.int32, shape, len(shape) - 1)
    return (lane >= h * width) & (lane < (h + 1) * width)


def _stack_heads(q, width):
    zero = jnp.zeros((), q.dtype)
    parts = [jnp.where(_head_mask(q.shape, h, width), q, zero) for h in range(q.shape[1] // width)]
    return jnp.concatenate(parts, axis=0).astype(BF16)


def _pick_heads(full, t, width):
    out = None
    for h in range(full.shape[1] // width):
        blk = full[h * t:(h + 1) * t, :]
        piece = jnp.where(_head_mask(blk.shape, h, width), blk, 0.0)
        out = piece if out is None else out + piece
    return out


def _const_spec(block, index):
    return pl.BlockSpec(block, lambda *_: index, pipeline_mode=pl.Buffered(1))


def _full_spec(a):
    return pl.BlockSpec(a.shape, lambda *_: (0,) * a.ndim)


def _params(sem):
    return pltpu.CompilerParams(dimension_semantics=sem, vmem_limit_bytes=VMEM_LIMIT)


def _mem_proj_kernel(m_ref, w_ref, o_ref):
    o_ref[0, 0] = _dot_nt(w_ref[0], m_ref[0].astype(BF16))


def _mem_proj(mem, w_t):
    batch, n_mem, d = mem.shape
    depth, n, _ = w_t.shape
    return pl.pallas_call(
        _mem_proj_kernel,
        grid=(depth, batch),
        in_specs=[pl.BlockSpec((1, n_mem, d), lambda l, b: (b, 0, 0)),
                  pl.BlockSpec((1, n, d), lambda l, b: (l, 0, 0))],
        out_specs=pl.BlockSpec((1, 1, n, n_mem), lambda l, b: (l, b, 0, 0)),
        out_shape=jax.ShapeDtypeStruct((depth, batch, n, n_mem), F32),
        compiler_params=_params(("arbitrary", "arbitrary")),
        name="mem_proj",
    )(mem, w_t)


def _mem_attn_tile(qm, kv_t):
    t = qm.shape[0]
    q4 = _stack_heads(qm, MEM_HEAD_DIM)
    k_t = kv_t[:MEM_WIDTH].astype(BF16)
    v_t = kv_t[MEM_WIDTH:].astype(BF16)
    s = _dot(q4, k_t)
    m = jnp.max(s, axis=-1, keepdims=True)
    e = jnp.exp(s - m)
    p = (e / jnp.sum(e, axis=-1, keepdims=True)).astype(BF16)
    return _pick_heads(_dot_nt(p, v_t), t, MEM_HEAD_DIM)


def _in_a_body(x_ref, w_ref, cos_ref, sin_ref, mkv_ref, q_ref, k_ref, v_ref, g_ref, m_ref):
    xb = x_ref[...].astype(BF16)
    cos = cos_ref[...]
    sin = sin_ref[...]
    mw = MAIN_WIDTH
    half = RET_HEAD_DIM // 2
    q = _rope_cols(_dot(xb, w_ref[0, :, 0:mw]), cos, sin, half)
    q_ref[...] = q.astype(q_ref.dtype)
    k = _rope_cols(_dot(xb, w_ref[0, :, mw:2 * mw]), cos, sin, half) * (RET_HEAD_DIM ** -0.5)
    k_ref[...] = k.astype(k_ref.dtype)
    v_ref[...] = _dot(xb, w_ref[0, :, 2 * mw:3 * mw]).astype(v_ref.dtype)
    g_ref[...] = _silu(_dot(xb, w_ref[0, :, 3 * mw:4 * mw])).astype(g_ref.dtype)
    qm = _dot(xb, w_ref[0, :, 4 * mw:4 * mw + MEM_WIDTH]) * ATTN_SCALE
    if mkv_ref is not None:
        m_ref[...] = _mem_attn_tile(qm.astype(BF16), mkv_ref[0, 0]).astype(m_ref.dtype)
    else:
        m_ref[...] = qm.astype(m_ref.dtype)


def _in_a_kernel(xp_ref, xs_ref, w_ref, cosp_ref, sinp_ref, coss_ref, sins_ref, mkv_ref, *outs, n_p):
    i = pl.program_id(0)

    @pl.when(i < n_p)
    def _():
        _in_a_body(xp_ref, w_ref, cosp_ref, sinp_ref, mkv_ref, *outs[:5])

    @pl.when(i == n_p)
    def _():
        _in_a_body(xs_ref, w_ref, coss_ref, sins_ref, None, *outs[5:])


def _in_a(xp, xs, w_bf, tab_p, tab_s, memkv_t, seq):
    tp, d = xp.shape
    ts = xs.shape[0]
    tm = TOKEN_TILE
    n_p = tp // tm
    nt = seq // tm
    cl = lambda i: jnp.minimum(i, n_p - 1)
    row = lambda i: (cl(i), 0)
    tab = lambda i: (cl(i) % nt, 0)
    widths = (MAIN_WIDTH,) * 4 + (MEM_WIDTH,)
    return pl.pallas_call(
        functools.partial(_in_a_kernel, n_p=n_p),
        grid=(n_p + 1,),
        in_specs=[pl.BlockSpec((tm, d), row), _full_spec(xs), _const_spec((1,) + w_bf.shape[1:], (0, 0, 0)),
                  pl.BlockSpec((tm, LANES), tab), pl.BlockSpec((tm, LANES), tab),
                  _full_spec(tab_s[0]), _full_spec(tab_s[1]),
                  pl.BlockSpec((1, 1) + memkv_t.shape[2:], lambda i: (0, cl(i) // nt, 0, 0))],
        out_specs=[pl.BlockSpec((tm, w), row) for w in widths]
                  + [pl.BlockSpec((ts, w), lambda i: (0, 0)) for w in widths],
        out_shape=[jax.ShapeDtypeStruct((tp, w), BF16) for w in widths]
                  + [jax.ShapeDtypeStruct((ts, w), F32) for w in widths],
        compiler_params=_params(("arbitrary",)),
        name="in_proj_a",
    )(xp, xs, w_bf, *tab_p, *tab_s, memkv_t)


def _retention_kernel(q_ref, k_ref, v_ref, g_ref, mix_ref, st_ref, dec_ref, rdec_ref, kdec_ref):
    c = pl.program_id(1)
    cs = RET_CHUNK

    @pl.when(c == 0)
    def _():
        st_ref[...] = jnp.zeros_like(st_ref)
        row = lax.broadcasted_iota(jnp.int32, (cs, cs), 0).astype(F32)
        col = lax.broadcasted_iota(jnp.int32, (cs, cs), 1).astype(F32)
        diff = row - col
        for h in range(RET_HEADS):
            lg = LOG_G[h]
            dec_ref[h] = jnp.where(diff >= 0, jnp.exp(jnp.maximum(diff, 0.0) * lg), 0.0)
            rdec_ref[h] = jnp.exp((row + 1.0) * lg)
            kdec_ref[h] = jnp.exp((cs - 1.0 - row) * lg)

    for j in range(q_ref.shape[0] // cs):
        rs = slice(j * cs, (j + 1) * cs)
        for h in range(RET_HEADS):
            hs = slice(h * RET_HEAD_DIM, (h + 1) * RET_HEAD_DIM)
            qh = q_ref[rs, hs]
            kh = k_ref[rs, hs]
            vh = v_ref[rs, hs]
            st = st_ref[0, h]
            s = _dot_nt(qh, kh) * dec_ref[h]
            inner = _dot(s.astype(BF16), vh)
            cross = _dot(qh, st.astype(BF16)) * rdec_ref[h]
            kd = (kh.astype(F32) * kdec_ref[h]).astype(BF16)
            st_ref[0, h] = math.exp(cs * LOG_G[h]) * st + _dot_tn(kd, vh)
            o = inner + cross
            mu = jnp.mean(o, axis=-1, keepdims=True)
            oc = o - mu
            var = jnp.mean(oc * oc, axis=-1, keepdims=True)
            on = oc * lax.rsqrt(var + LN_EPS)
            mix_ref[rs, hs] = (g_ref[rs, hs].astype(F32) * on).astype(mix_ref.dtype)


def _retention(q, k, v, g, batch, seq):
    t = q.shape[0]
    rows = RET_CHUNK * RET_CHUNKS_PER_STEP
    ns = seq // rows
    tok = lambda b, c: (b * ns + c, 0)
    sq = (RET_HEADS, RET_CHUNK, RET_CHUNK)
    return pl.pallas_call(
        _retention_kernel,
        grid=(batch, ns),
        in_specs=[pl.BlockSpec((rows, MAIN_WIDTH), tok)] * 4,
        out_specs=[pl.BlockSpec((rows, MAIN_WIDTH), tok),
                   pl.BlockSpec((1, RET_HEADS, RET_HEAD_DIM, RET_HEAD_DIM), lambda b, c: (b, 0, 0, 0))],
        out_shape=[jax.ShapeDtypeStruct((t, MAIN_WIDTH), BF16),
                   jax.ShapeDtypeStruct((batch, RET_HEADS, RET_HEAD_DIM, RET_HEAD_DIM), F32)],
        scratch_shapes=[pltpu.VMEM(sq, F32), pltpu.VMEM(sq, F32), pltpu.VMEM(sq, F32)],
        compiler_params=_params(("arbitrary", "arbitrary")),
        name="retention",
    )(q, k, v, g)


def _retention_sample_kernel(q_ref, k_ref, v_ref, g_ref, qm_ref, mkv_ref, st_ref, mix_ref, nst_ref):
    t = q_ref.shape[0]
    pad = jnp.zeros((LANES - t, RET_HEAD_DIM), F32)
    row = lax.broadcasted_iota(jnp.int32, (t, LANES), 0).astype(F32)
    col = lax.broadcasted_iota(jnp.int32, (t, LANES), 1).astype(F32)
    prow = lax.broadcasted_iota(jnp.int32, (LANES, RET_HEAD_DIM), 0).astype(F32)
    diff = row - col
    for h in range(RET_HEADS):
        lg = LOG_G[h]
        hs = slice(h * RET_HEAD_DIM, (h + 1) * RET_HEAD_DIM)
        qh = q_ref[:, hs]
        kp = jnp.concatenate([k_ref[:, hs], pad], axis=0)
        vp = jnp.concatenate([v_ref[:, hs], pad], axis=0)
        st = st_ref[0, 0, h]
        dec = jnp.where(diff >= 0, jnp.exp(jnp.maximum(diff, 0.0) * lg), 0.0)
        inner = _dot(_dot_nt(qh, kp) * dec, vp)
        cross = _dot(qh, st) * jnp.exp((row + 1.0) * lg)
        kd = kp * jnp.exp((t - 1.0 - prow) * lg)
        nst_ref[0, h] = math.exp(t * lg) * st + _dot_tn(kd, vp)
        o = inner + cross
        mu = jnp.mean(o, axis=-1, keepdims=True)
        oc = o - mu
        var = jnp.mean(oc * oc, axis=-1, keepdims=True)
        on = oc * lax.rsqrt(var + LN_EPS)
        mix_ref[:, hs] = g_ref[:, hs] * on
    mix_ref[:, MAIN_WIDTH:] = _mem_attn_tile(qm_ref[...], mkv_ref[0, 0])


def _retention_sample(q, k, v, g, qm, memkv_t, state, batch, t):
    tok = lambda b: (b, 0)
    hd = (RET_HEADS, RET_HEAD_DIM, RET_HEAD_DIM)
    return pl.pallas_call(
        _retention_sample_kernel,
        grid=(batch,),
        in_specs=[pl.BlockSpec((t, MAIN_WIDTH), tok)] * 4
                 + [pl.BlockSpec((t, MEM_WIDTH), tok),
                    pl.BlockSpec((1, 1) + memkv_t.shape[2:], lambda b: (0, b, 0, 0)),
                    pl.BlockSpec((1, 1) + hd, lambda b: (0, b, 0, 0, 0))],
        out_specs=[pl.BlockSpec((t, D_MODEL), tok), pl.BlockSpec((1,) + hd, lambda b: (b, 0, 0, 0))],
        out_shape=[jax.ShapeDtypeStruct((batch * t, D_MODEL), F32),
                   jax.ShapeDtypeStruct((batch,) + hd, F32)],
        compiler_params=_params(("parallel",)),
        name="retention_sample",
    )(q, k, v, g, qm, memkv_t, state)


def _dilated_sample_group(q, cache, new, window, dil):
    t = q.shape[0]
    n_buf = cache.shape[1]
    gw = GROUP_WIDTH
    q4 = _stack_heads(q, DIL_HEAD_DIM)
    rows = q4.shape[0]
    newp = jnp.concatenate([new, jnp.zeros((LANES - t, new.shape[1]), F32)], axis=0).astype(BF16)

    def masked(s, first_index):
        key = lax.broadcasted_iota(jnp.int32, s.shape, 1) + first_index
        tok = lax.broadcasted_iota(jnp.int32, s.shape, 0) & (t - 1)
        delta = n_buf + tok - key
        valid = (delta >= 0) & (delta <= window) & ((delta & (dil - 1)) == 0)
        return jnp.where(valid, s, NEG_BIG)

    s_c = masked(_dot(q4, cache[:gw].astype(BF16)), 0)
    s_n = masked(_dot_nt(q4, newp[:, :gw]), n_buf)
    m = jnp.maximum(jnp.max(s_c, axis=-1, keepdims=True), jnp.max(s_n, axis=-1, keepdims=True))
    e_c = jnp.exp(s_c - m)
    e_n = jnp.exp(s_n - m)
    l = jnp.sum(e_c, axis=-1, keepdims=True) + jnp.sum(e_n, axis=-1, keepdims=True)
    inv = 1.0 / l
    full = (_dot_nt((e_c * inv).astype(BF16), cache[gw:].astype(BF16))
            + _dot((e_n * inv).astype(BF16), newp[:, gw:]))
    o = _pick_heads(full, t, DIL_HEAD_DIM)
    lse = _pick_heads(jnp.broadcast_to(m + jnp.log(l), (rows, gw)), t, DIL_HEAD_DIM)
    return o, lse


def _sample_mixer_kernel(*refs):
    q_refs, cache_refs, new_refs, new_t_refs = refs[0:3], refs[3:6], refs[6:9], refs[9:12]
    qm_ref, mkv_ref, mix_ref = refs[12], refs[13], refs[14]
    win_refs = refs[15:18]
    gw = GROUP_WIDTH
    outs, lses = [], []
    for g, (window, dil) in enumerate(DIL_PAIRS):
        cache = cache_refs[g][0]
        t = q_refs[g].shape[0]
        n_buf = cache.shape[1]
        o, lse = _dilated_sample_group(q_refs[g][...], cache, new_refs[g][...], window, dil)
        outs.append(o)
        lses.append(lse)
        win_refs[g][0, :, 0:n_buf - t] = cache[:, t:]
        win_refs[g][0, :, n_buf - t:] = new_t_refs[g][0]
    m = jnp.maximum(jnp.maximum(lses[0], lses[1]), lses[2])
    es = [jnp.exp(v - m) for v in lses]
    inv = 1.0 / (es[0] + es[1] + es[2])
    for g in range(3):
        mix_ref[:, g * gw:(g + 1) * gw] = outs[g] * (es[g] * inv)
    mix_ref[:, MAIN_WIDTH:] = _mem_attn_tile(qm_ref[...], mkv_ref[0, 0])


def _sample_mixer(q_g, caches_t, new_kv, new_kv_t, qm, memkv_t, batch, t):
    gw = GROUP_WIDTH
    tok = lambda b: (b, 0)
    big = lambda b: (b, 0, 0)
    cache_specs = [pl.BlockSpec((1,) + c.shape[1:], big) for c in caches_t]
    return pl.pallas_call(
        _sample_mixer_kernel,
        grid=(batch,),
        in_specs=[pl.BlockSpec((t, gw), tok)] * 3 + cache_specs + [pl.BlockSpec((t, 2 * gw), tok)] * 3
                 + [pl.BlockSpec((1, 2 * gw, t), big)] * 3
                 + [pl.BlockSpec((t, MEM_WIDTH), tok),
                    pl.BlockSpec((1, 1) + memkv_t.shape[2:], lambda b: (1, b, 0, 0))],
        out_specs=[pl.BlockSpec((t, D_MODEL), tok)] + cache_specs,
        out_shape=[jax.ShapeDtypeStruct((batch * t, D_MODEL), F32)]
                  + [jax.ShapeDtypeStruct(c.shape, F32) for c in caches_t],
        compiler_params=_params(("parallel",)),
        name="sample_mixer",
    )(*q_g, *caches_t, *new_kv, *new_kv_t, qm, memkv_t)


def _natural_rows(ref, dil, scr):
    if dil == 1:
        return ref[0, 0].astype(F32)
    n = ref.shape[2]
    for r in range(dil):
        v = ref[0, r].astype(F32)
        for c in range(v.shape[1] // LANES):
            scr[c, pl.ds(r, n, stride=dil), :] = v[:, c * LANES:(c + 1) * LANES]
    return jnp.concatenate([scr[c] for c in range(scr.shape[0])], axis=1)


def _post_body(mix, x_ref, wo_ref, g1_ref, b1_ref, wi_ref, w2_ref, g2_ref, b2_ref, out_ref, act_ref):
    rows = x_ref.shape[0]
    x1 = _layer_norm(ALPHA * x_ref[...] + _dot(mix, wo_ref[0]), g1_ref[0], b1_ref[0])
    x1b = x1.astype(BF16)
    for c in range(0, FFN_HIDDEN, FFN_COL_CHUNK):
        gate = _dot(x1b, wi_ref[0, :, c:c + FFN_COL_CHUNK])
        up = _dot(x1b, wi_ref[0, :, FFN_HIDDEN + c:FFN_HIDDEN + c + FFN_COL_CHUNK])
        act_ref[0:rows, c:c + FFN_COL_CHUNK] = (_silu(gate) * up).astype(BF16)
    y = _dot(act_ref[0:rows, :], w2_ref[0])
    out_ref[...] = _layer_norm(ALPHA * x1 + y, g2_ref[0], b2_ref[0])


def _post_kernel(*refs, dils, n_mix_p, n_mix_s, n_p):
    mixp_refs, refs = refs[:n_mix_p], refs[n_mix_p:]
    xp_ref, refs = refs[0], refs[1:]
    mixs_refs, refs = refs[:n_mix_s], refs[n_mix_s:]
    xs_ref, refs = refs[0], refs[1:]
    weights, refs = refs[:7], refs[7:]
    outp_ref, outs_ref, act_ref = refs[0:3]
    scrs = list(refs[3:])
    i = pl.program_id(0)

    @pl.when(i < n_p)
    def _():
        if dils is not None:
            o_refs, l_refs, mem_ref = mixp_refs[0:3], mixp_refs[3:6], mixp_refs[6]
            pool = list(scrs)
            nat = lambda r, d: _natural_rows(r, d, pool.pop(0) if d > 1 else None)
            os_ = [nat(o_refs[g], dils[g]) for g in range(3)]
            ls = [nat(l_refs[g], dils[g]) for g in range(3)]
            m = jnp.maximum(jnp.maximum(ls[0], ls[1]), ls[2])
            es = [jnp.exp(v - m) for v in ls]
            inv = 1.0 / (es[0] + es[1] + es[2])
            parts = [(os_[g] * (es[g] * inv)).astype(BF16) for g in range(3)]
            parts.append(mem_ref[...].astype(BF16))
        else:
            parts = [r[...].astype(BF16) for r in mixp_refs]
        mix = parts[0] if len(parts) == 1 else jnp.concatenate(parts, axis=1)
        _post_body(mix, xp_ref, *weights, outp_ref, act_ref)

    @pl.when(i == n_p)
    def _():
        parts = [r[...].astype(BF16) for r in mixs_refs]
        mix = parts[0] if len(parts) == 1 else jnp.concatenate(parts, axis=1)
        _post_body(mix, xs_ref, *weights, outs_ref, act_ref)


def _post(layer, mix_p, xp, mix_s, xs, wo, g1, b1, wi, w2, g2, b2, dils, seq):
    tp, d = xp.shape
    tm = TOKEN_TILE
    n_p = tp // tm
    nt = seq // tm
    cl = lambda i: jnp.minimum(i, n_p - 1)
    row = lambda i: (cl(i), 0)
    if dils is None:
        mixp_specs = [pl.BlockSpec((tm, a.shape[1]), row) for a in mix_p]
        scratch = []
    else:
        res = lambda i: (cl(i) // nt, 0, cl(i) % nt, 0)
        mixp_specs = [pl.BlockSpec((1, dl, tm // dl, GROUP_WIDTH), res) for dl in dils] * 2
        mixp_specs.append(pl.BlockSpec((tm, MEM_WIDTH), row))
        scratch = [pltpu.VMEM((GROUP_WIDTH // LANES, tm, LANES), F32) for dl in dils * 2 if dl > 1]
    lsel = (layer, 0, 0)
    vec = _const_spec((1, 1, d), lsel)
    return pl.pallas_call(
        functools.partial(_post_kernel, dils=dils, n_mix_p=len(mix_p), n_mix_s=len(mix_s), n_p=n_p),
        grid=(n_p + 1,),
        in_specs=mixp_specs + [pl.BlockSpec((tm, d), row)] + [_full_spec(a) for a in mix_s] + [_full_spec(xs)]
                 + [_const_spec((1,) + wo.shape[1:], lsel), vec, vec,
                    _const_spec((1,) + wi.shape[1:], lsel), _const_spec((1,) + w2.shape[1:], lsel), vec, vec],
        out_specs=[pl.BlockSpec((tm, d), row), pl.BlockSpec(xs.shape, lambda i: (0, 0))],
        out_shape=[jax.ShapeDtypeStruct((tp, d), F32), jax.ShapeDtypeStruct(xs.shape, F32)],
        scratch_shapes=[pltpu.VMEM((tm, FFN_HIDDEN), BF16)] + scratch,
        compiler_params=_params(("arbitrary",)),
        name="post" if dils is None else "post_combine",
    )(*mix_p, xp, *mix_s, xs, wo, g1, b1, wi, w2, g2, b2)


def _residue_major(ref, y, dil, scr):
    if dil == 1:
        ref[...] = y.astype(ref.dtype).reshape(ref.shape)
        return
    n = ref.shape[2]
    ncol = y.shape[1] // LANES
    for c in range(ncol):
        scr[c] = y[:, c * LANES:(c + 1) * LANES]
    for r in range(dil):
        rows = [scr[c, pl.ds(r, n, stride=dil), :] for c in range(ncol)]
        ref[0, r] = jnp.concatenate(rows, axis=1).astype(ref.dtype)


def _in_b_body(x_ref, wkv_ref, wq_ref, wkvt_ref, cos_ref, sin_ref, cos_t_ref, sin_t_ref, mkv_ref,
               q_refs, kv_refs, m_ref, win_refs, scrs, dils, win_preds):
    xb = x_ref[...].astype(BF16)
    cos = cos_ref[...]
    sin = sin_ref[...]
    gw = GROUP_WIDTH
    mw = MAIN_WIDTH
    half = DIL_HEAD_DIM // 2
    tm = x_ref.shape[0]
    pool = list(scrs)
    for g in range(3):
        ks = slice(g * gw, (g + 1) * gw)
        vs = slice(mw + g * gw, mw + (g + 1) * gw)
        k = _rope_cols(_dot(xb, wkv_ref[:, ks]), cos, sin, half)
        v = _dot(xb, wkv_ref[:, vs])
        _residue_major(kv_refs[g], jnp.concatenate([k, v], axis=1), dils[g], pool.pop(0) if dils[g] > 1 else None)
        q = _rope_cols(_dot(xb, wq_ref[0, :, ks]), cos, sin, half) * ATTN_SCALE
        _residue_major(q_refs[g], q, dils[g], pool.pop(0) if dils[g] > 1 else None)

        def window(g=g, ks=ks, vs=vs):
            k_t = _rope_rows(_dot_nt(wkvt_ref[ks, :], xb), cos_t_ref[...], sin_t_ref[...], DIL_HEAD_DIM)
            v_t = _dot_nt(wkvt_ref[vs, :], xb)
            full = jnp.concatenate([k_t, v_t], axis=0)
            win_refs[g][0] = full[:, tm - win_refs[g].shape[2]:]

        if win_preds[g] is None:
            window()
        else:
            pl.when(win_preds[g])(window)
    qm = _dot(xb, wq_ref[0, :, mw:mw + MEM_WIDTH]) * ATTN_SCALE
    if mkv_ref is not None:
        m_ref[...] = _mem_attn_tile(qm.astype(BF16), mkv_ref[0, 0]).astype(m_ref.dtype)
    else:
        m_ref[...] = qm.astype(m_ref.dtype)


def _in_b_kernel(xp_ref, xs_ref, wkv_ref, wq_ref, wkvt_ref, cosp_ref, sinp_ref, cospt_ref, sinpt_ref,
                 coss_ref, sins_ref, cosst_ref, sinst_ref, mkv_ref, *refs, dils, win_tiles, nt, n_p):
    outp, outs, scrs = refs[0:10], refs[10:20], refs[20:]
    i = pl.program_id(0)

    @pl.when(i < n_p)
    def _():
        j = i % nt
        _in_b_body(xp_ref, wkv_ref, wq_ref, wkvt_ref, cosp_ref, sinp_ref, cospt_ref, sinpt_ref, mkv_ref,
                   outp[0:3], outp[3:6], outp[6], outp[7:10], scrs, dils,
                   [j >= nt - w for w in win_tiles])

    @pl.when(i == n_p)
    def _():
        _in_b_body(xs_ref, wkv_ref, wq_ref, wkvt_ref, coss_ref, sins_ref, cosst_ref, sinst_ref, None,
                   outs[0:3], outs[3:6], outs[6], outs[7:10], (), (1, 1, 1), [None] * 3)


def _in_b(xp, xs, w_kv, w_q, w_kvt, tabs_p, tabs_pt, tabs_s, tabs_st, memkv_t, batch, seq, dils, windows):
    tp, d = xp.shape
    ts = xs.shape[0]
    gw = GROUP_WIDTH
    tm = TOKEN_TILE
    n_p = tp // tm
    nt = seq // tm
    cl = lambda i: jnp.minimum(i, n_p - 1)
    row = lambda i: (cl(i), 0)
    tab = lambda i: (cl(i) % nt, 0)
    tab_t = lambda i: (0, cl(i) % nt)
    res = lambda i: (cl(i) // nt, 0, cl(i) % nt, 0)
    fix2 = lambda i: (0, 0)
    out_specs = ([pl.BlockSpec((1, dl, tm // dl, gw), res) for dl in dils]
                 + [pl.BlockSpec((1, dl, tm // dl, 2 * gw), res) for dl in dils]
                 + [pl.BlockSpec((tm, MEM_WIDTH), row)])
    out_shape = ([jax.ShapeDtypeStruct((batch, dl, seq // dl, gw), BF16) for dl in dils]
                 + [jax.ShapeDtypeStruct((batch, dl, seq // dl, 2 * gw), BF16) for dl in dils]
                 + [jax.ShapeDtypeStruct((tp, MEM_WIDTH), BF16)])
    win_tiles = []
    for w in windows:
        rb = min(w, tm)
        nblk = w // rb
        win_tiles.append(nblk)
        out_specs.append(pl.BlockSpec(
            (1, 2 * gw, rb), lambda i, nblk=nblk: (cl(i) // nt, 0, jnp.maximum(cl(i) % nt - (nt - nblk), 0))))
        out_shape.append(jax.ShapeDtypeStruct((batch, 2 * gw, w), F32))
    out_specs += ([pl.BlockSpec((ts, gw), fix2)] * 3 + [pl.BlockSpec((ts, 2 * gw), fix2)] * 3
                  + [pl.BlockSpec((ts, MEM_WIDTH), fix2)] + [pl.BlockSpec((1, 2 * gw, ts), lambda i: (0, 0, 0))] * 3)
    out_shape += ([jax.ShapeDtypeStruct((ts, gw), F32)] * 3 + [jax.ShapeDtypeStruct((ts, 2 * gw), F32)] * 3
                  + [jax.ShapeDtypeStruct((ts, MEM_WIDTH), F32)] + [jax.ShapeDtypeStruct((1, 2 * gw, ts), F32)] * 3)
    scratch = []
    for dl in dils:
        if dl > 1:
            scratch += [pltpu.VMEM((2 * gw // LANES, tm, LANES), F32), pltpu.VMEM((gw // LANES, tm, LANES), F32)]
    return pl.pallas_call(
        functools.partial(_in_b_kernel, dils=tuple(dils), win_tiles=tuple(win_tiles), nt=nt, n_p=n_p),
        grid=(n_p + 1,),
        in_specs=[pl.BlockSpec((tm, d), row), _full_spec(xs), _const_spec(w_kv.shape, (0, 0)),
                  _const_spec((1,) + w_q.shape[1:], (0, 0, 0)), _const_spec(w_kvt.shape, (0, 0)),
                  pl.BlockSpec((tm, LANES), tab), pl.BlockSpec((tm, LANES), tab),
                  pl.BlockSpec((DIL_HEAD_DIM, tm), tab_t), pl.BlockSpec((DIL_HEAD_DIM, tm), tab_t),
                  _full_spec(tabs_s[0]), _full_spec(tabs_s[1]), _full_spec(tabs_st[0]), _full_spec(tabs_st[1]),
                  pl.BlockSpec((1, 1) + memkv_t.shape[2:], lambda i: (1, cl(i) // nt, 0, 0))],
        out_specs=out_specs,
        out_shape=out_shape,
        scratch_shapes=scratch,
        compiler_params=_params(("arbitrary",)),
        name="in_proj_b",
    )(xp, xs, w_kv, w_q, w_kvt, *tabs_p, *tabs_pt, *tabs_s, *tabs_st, memkv_t)


def _dilated_kernel(q_ref, kv_ref, o_ref, lse_ref, ring_ref):
    i = pl.program_id(2)
    blk = DIL_BLOCK
    gw = GROUP_WIDTH
    nsub = q_ref.shape[2] // blk
    slot = i & 1

    @pl.when(i == 0)
    def _():
        ring_ref[1] = jnp.zeros(ring_ref.shape[1:], ring_ref.dtype)

    cur = kv_ref[0, 0]
    ring_ref[slot] = cur[(nsub - 1) * blk:]
    kext = jnp.concatenate([ring_ref[1 - slot], cur], axis=0)
    rows = GROUP_HEADS * blk
    qi = lax.broadcasted_iota(jnp.int32, (rows, 2 * blk), 0) & (blk - 1)
    kj = lax.broadcasted_iota(jnp.int32, (rows, 2 * blk), 1)
    delta = qi + blk - kj
    band = (delta >= 0) & (delta <= blk)
    for j in range(nsub):
        rs = slice(j * blk, (j + 1) * blk)
        q4 = _stack_heads(q_ref[0, 0, rs, :], DIL_HEAD_DIM)
        kv = kext[j * blk:(j + 2) * blk]
        valid = band if j > 0 else band & ((i > 0) | (kj >= blk))
        s = jnp.where(valid, _dot_nt(q4, kv[:, :gw]), NEG_BIG)
        m = jnp.max(s, axis=-1, keepdims=True)
        e = jnp.exp(s - m)
        l = jnp.sum(e, axis=-1, keepdims=True)
        p = (e * (1.0 / l)).astype(BF16)
        full = _dot(p, kv[:, gw:])
        o_ref[0, 0, rs, :] = _pick_heads(full, blk, DIL_HEAD_DIM).astype(o_ref.dtype)
        lse_ref[0, 0, rs, :] = _pick_heads(jnp.broadcast_to(m + jnp.log(l), (rows, gw)), blk, DIL_HEAD_DIM)


def _dilated(q, kv, dil):
    batch, _, m, gw = q.shape
    rows = DIL_BLOCK * min(DIL_BLOCKS_PER_STEP, m // DIL_BLOCK)
    idx = lambda b, r, i: (b, r, i, 0)
    return pl.pallas_call(
        _dilated_kernel,
        grid=(batch, dil, m // rows),
        in_specs=[pl.BlockSpec((1, 1, rows, gw), idx), pl.BlockSpec((1, 1, rows, 2 * gw), idx)],
        out_specs=[pl.BlockSpec((1, 1, rows, gw), idx)] * 2,
        out_shape=[jax.ShapeDtypeStruct(q.shape, BF16), jax.ShapeDtypeStruct(q.shape, F32)],
        scratch_shapes=[pltpu.VMEM((2, DIL_BLOCK, 2 * gw), BF16)],
        compiler_params=_params(("arbitrary", "arbitrary", "arbitrary")),
        name=f"dilated_d{dil}",
    )(q, kv)


def _rope_tables(pos):
    hd = RET_HEAD_DIM
    inv = ROPE_THETA ** (-jnp.arange(0, hd, 2, dtype=F32) / hd)
    ang = pos[:, None] * inv[None, :]
    cos, sin = jnp.cos(ang), jnp.sin(ang)
    big = (jnp.concatenate([cos, cos], axis=-1), jnp.concatenate([-sin, sin], axis=-1))
    cos_h, sin_h = cos[:, 0::2], sin[:, 0::2]
    cos_f = jnp.concatenate([cos_h, cos_h], axis=-1)
    sin_f = jnp.concatenate([-sin_h, sin_h], axis=-1)
    return big, (jnp.tile(cos_f, (1, 2)), jnp.tile(sin_f, (1, 2))), (cos_f.T, sin_f.T)


def _to_feature_major(x5):
    b, w = x5.shape[0], x5.shape[1]
    return jnp.transpose(x5, (0, 2, 3, 4, 1)).reshape(b, 2 * GROUP_WIDTH, w)


def _from_feature_major(xt):
    b, _, w = xt.shape
    return jnp.transpose(xt.reshape(b, 2, GROUP_HEADS, DIL_HEAD_DIM, w), (0, 4, 1, 2, 3))


def kernel(x_prompt, x_sample, mem_prompt, cache_mem_kv, state_ret, cache_win_kv_g1, cache_win_kv_g2, cache_win_kv_g3, w_in_a, w_in_b, w_out, w_kv_shared, w_mem_kv, ln_mix_g, ln_mix_b, ln_ffn_g, ln_ffn_b, w_ffn_in, w_ffn_out):
    batch, seq, d = x_prompt.shape
    dec_batch, dec_seq, _ = x_sample.shape
    n_mem = mem_prompt.shape[1]
    gw = GROUP_WIDTH
    ts = dec_batch * dec_seq
    win_caches = (cache_win_kv_g1, cache_win_kv_g2, cache_win_kv_g3)
    dils = tuple(dl for _, dl in DIL_PAIRS)

    pos_p = jnp.arange(seq, dtype=F32)
    pos_s = jnp.tile(PAST_LEN + jnp.arange(dec_seq, dtype=F32), dec_batch)
    tab_a_p, tab_b_p, tab_bt_p = _rope_tables(pos_p)
    tab_a_s, tab_b_s, tab_bt_s = _rope_tables(pos_s)

    w_a = w_in_a.astype(BF16)
    w_kv = w_kv_shared.astype(BF16)
    w_kvt = w_kv_shared.T.astype(BF16)
    w_q = w_in_b.astype(BF16)
    w_o = w_out.astype(BF16)
    w_fi = w_ffn_in.astype(BF16)
    w_fo = w_ffn_out.astype(BF16)
    w_mem_t = jnp.transpose(w_mem_kv, (0, 2, 1)).astype(BF16)
    lnv = lambda a: a.reshape(DEPTH, 1, d)
    ln = (lnv(ln_mix_g), lnv(ln_mix_b), lnv(ln_ffn_g), lnv(ln_ffn_b))

    memkv_p = _mem_proj(mem_prompt, w_mem_t)
    memkv_s = jnp.transpose(cache_mem_kv, (0, 1, 3, 4, 5, 2)).reshape(DEPTH, dec_batch, 2 * MEM_WIDTH, n_mem)

    xp = x_prompt.reshape(batch * seq, d)
    xs = x_sample.reshape(ts, d)

    def post(l, mix_p, mix_s, dls):
        return _post(l, mix_p, xp, mix_s, xs, w_o, ln[0], ln[1], w_fi, w_fo, ln[2], ln[3], dls, seq)

    outs = _in_a(xp, xs, w_a, tab_a_p, tab_a_s, memkv_p, seq)
    q, k, v, g, mem_o = outs[:5]
    mix_p, state_p = _retention(q, k, v, g, batch, seq)
    q, k, v, g, qm = outs[5:]
    mix_s, state_s = _retention_sample(q, k, v, g, qm, memkv_s, state_ret, dec_batch, dec_seq)
    xp, xs = post(0, [mix_p, mem_o], [mix_s], None)

    windows_p = tuple(min(w, seq) for w, _ in DIL_PAIRS)
    outs = _in_b(xp, xs, w_kv, w_q, w_kvt, tab_b_p, tab_bt_p, tab_b_s, tab_bt_s, memkv_p,
                 batch, seq, dils, windows_p)
    q_g, kv_g, mem_o, win_p = outs[0:3], outs[3:6], outs[6], outs[7:10]
    att = [_dilated(q_g[i], kv_g[i], dils[i]) for i in range(3)]

    q_s, kv_new, qm, new_t = outs[10:13], outs[13:16], outs[16], outs[17:20]
    new_t = [jnp.transpose(a.reshape(2 * gw, dec_batch, dec_seq), (1, 0, 2)) for a in new_t]
    caches_t = [_to_feature_major(c) for c in win_caches]
    mix_s, *win_s = _sample_mixer(q_s, caches_t, kv_new, new_t, qm, memkv_s, dec_batch, dec_seq)

    xp, xs = post(1, [a[0] for a in att] + [a[1] for a in att] + [mem_o], [mix_s], dils)

    memkv_out = jnp.transpose(memkv_p.reshape(DEPTH, batch, 2, MEM_HEADS, MEM_HEAD_DIM, n_mem), (0, 1, 5, 2, 3, 4))
    return (xp.reshape(batch, seq, d), xs.reshape(dec_batch, dec_seq, d),
            state_p[None], state_s[None], memkv_out,
            _from_feature_major(win_p[0]), _from_feature_major(win_p[1]), _from_feature_major(win_p[2]),
            _from_feature_major(win_s[0]), _from_feature_major(win_s[1]), _from_feature_major(win_s[2]))
```

```python
import functools
import math

import jax
import jax.numpy as jnp
from jax import lax
from jax.experimental import pallas as pl
from jax.experimental.pallas import tpu as pltpu

F32 = jnp.float32
BF16 = jnp.bfloat16

D_MODEL = 1024
MEM_HEADS = 4
MEM_HEAD_DIM = 64
MEM_WIDTH = MEM_HEADS * MEM_HEAD_DIM
MAIN_WIDTH = D_MODEL - MEM_WIDTH
RET_HEADS = 6
RET_HEAD_DIM = MAIN_WIDTH // RET_HEADS
RET_CHUNK = 128
RET_CHUNKS_PER_STEP = 4
DIL_PAIRS = ((128, 1), (512, 4), (2048, 16))
GROUP_HEADS = 4
DIL_HEAD_DIM = 64
GROUP_WIDTH = GROUP_HEADS * DIL_HEAD_DIM
DIL_BLOCK = 128
DIL_BLOCKS_PER_STEP = 4
FFN_HIDDEN = 2816
ROPE_THETA = 10000.0
ROPE_SPLIT = 64
LN_EPS = 1e-5
DEPTH = 2
ALPHA = (2 * DEPTH) ** 0.25
PAST_LEN = 8192
NEG_BIG = -1e30
ATTN_SCALE = DIL_HEAD_DIM ** -0.5

LANES = 128
VMEM_LIMIT = 56 * 1024 * 1024
FFN_COL_CHUNK = 256
TOKEN_TILE = 512

LOG_G = tuple(math.log1p(-(2.0 ** (-5.0 - h))) for h in range(RET_HEADS))


def _dot(a, b):
    return jnp.dot(a, b, preferred_element_type=F32)


def _dot_nt(a, b):
    return lax.dot_general(a, b, (((1,), (1,)), ((), ())), preferred_element_type=F32)


def _dot_tn(a, b):
    return lax.dot_general(a, b, (((0,), (0,)), ((), ())), preferred_element_type=F32)


def _silu(x):
    return x / (1.0 + jnp.exp(-x))


def _layer_norm(z, g, b):
    mu = jnp.mean(z, axis=-1, keepdims=True)
    zc = z - mu
    var = jnp.mean(zc * zc, axis=-1, keepdims=True)
    return zc * lax.rsqrt(var + LN_EPS) * g + b


def _rope_lanes(y, cos, sin_signed, half):
    if 2 * half == LANES:
        partner = pltpu.roll(y, half, 1)
    else:
        lane = lax.broadcasted_iota(jnp.int32, y.shape, 1)
        first = (lane & (2 * half - 1)) < half
        partner = jnp.where(first, pltpu.roll(y, LANES - half, 1), pltpu.roll(y, half, 1))
    return y * cos + partner * sin_signed


def _rope_cols(y, cos, sin_signed, half):
    parts = [_rope_lanes(y[:, j:j + LANES], cos, sin_signed, half) for j in range(0, y.shape[1], LANES)]
    return parts[0] if len(parts) == 1 else jnp.concatenate(parts, axis=1)


def _rope_rows(yt, cos_t, sin_t, head_dim):
    half = head_dim // 2
    parts = []
    for r0 in range(0, yt.shape[0], head_dim):
        blk = yt[r0:r0 + head_dim]
        swapped = jnp.concatenate([blk[half:], blk[:half]], axis=0)
        parts.append(blk * cos_t + swapped * sin_t)
    return jnp.concatenate(parts, axis=0)


def _head_mask(shape, h, width):
    lane = lax.broadcasted_iota(jnp.int32, shape, len(shape) - 1)
    return (lane >= h * width) & (lane < (h + 1) * width)


def _stack_heads(q, width):
    zero = jnp.zeros((), q.dtype)
    parts = [jnp.where(_head_mask(q.shape, h, width), q, zero) for h in range(q.shape[1] // width)]
    return jnp.concatenate(parts, axis=0).astype(BF16)


def _pick_heads(full, t, width):
    out = None
    for h in range(full.shape[1] // width):
        blk = full[h * t:(h + 1) * t, :]
        piece = jnp.where(_head_mask(blk.shape, h, width), blk, 0.0)
        out = piece if out is None else out + piece
    return out


def _const_spec(block, index):
    return pl.BlockSpec(block, lambda *_: index, pipeline_mode=pl.Buffered(1))


def _full_spec(a):
    return pl.BlockSpec(a.shape, lambda *_: (0,) * a.ndim)


def _params(sem):
    return pltpu.CompilerParams(dimension_semantics=sem, vmem_limit_bytes=VMEM_LIMIT)


def _mem_proj_kernel(m_ref, w_ref, o_ref):
    o_ref[0, 0] = _dot_nt(w_ref[0], m_ref[0].astype(BF16))


def _mem_proj(mem, w_t):
    batch, n_mem, d = mem.shape
    depth, n, _ = w_t.shape
    return pl.pallas_call(
        _mem_proj_kernel,
        grid=(depth, batch),
        in_specs=[pl.BlockSpec((1, n_mem, d), lambda l, b: (b, 0, 0)),
                  pl.BlockSpec((1, n, d), lambda l, b: (l, 0, 0))],
        out_specs=pl.BlockSpec((1, 1, n, n_mem), lambda l, b: (l, b, 0, 0)),
        out_shape=jax.ShapeDtypeStruct((depth, batch, n, n_mem), F32),
        compiler_params=_params(("arbitrary", "arbitrary")),
        name="mem_proj",
    )(mem, w_t)


def _mem_attn_tile(qm, kv_t):
    t = qm.shape[0]
    q4 = _stack_heads(qm, MEM_HEAD_DIM)
    k_t = kv_t[:MEM_WIDTH].astype(BF16)
    v_t = kv_t[MEM_WIDTH:].astype(BF16)
    s = _dot(q4, k_t)
    m = jnp.max(s, axis=-1, keepdims=True)
    e = jnp.exp(s - m)
    p = (e / jnp.sum(e, axis=-1, keepdims=True)).astype(BF16)
    return _pick_heads(_dot_nt(p, v_t), t, MEM_HEAD_DIM)


def _in_a_body(x_ref, w_ref, cos_ref, sin_ref, mkv_ref, q_ref, k_ref, v_ref, g_ref, m_ref):
    xb = x_ref[...].astype(BF16)
    cos = cos_ref[...]
    sin = sin_ref[...]
    mw = MAIN_WIDTH
    half = RET_HEAD_DIM // 2
    q = _rope_cols(_dot(xb, w_ref[0, :, 0:mw]), cos, sin, half)
    q_ref[...] = q.astype(q_ref.dtype)
    k = _rope_cols(_dot(xb, w_ref[0, :, mw:2 * mw]), cos, sin, half) * (RET_HEAD_DIM ** -0.5)
    k_ref[...] = k.astype(k_ref.dtype)
    v_ref[...] = _dot(xb, w_ref[0, :, 2 * mw:3 * mw]).astype(v_ref.dtype)
    g_ref[...] = _silu(_dot(xb, w_ref[0, :, 3 * mw:4 * mw])).astype(g_ref.dtype)
    qm = _dot(xb, w_ref[0, :, 4 * mw:4 * mw + MEM_WIDTH]) * ATTN_SCALE
    if mkv_ref is not None:
        m_ref[...] = _mem_attn_tile(qm.astype(BF16), mkv_ref[0, 0]).astype(m_ref.dtype)
    else:
        m_ref[...] = qm.astype(m_ref.dtype)


def _shift_window(cache_ref, win_ref, t):
    n = cache_ref.shape[2]
    win_ref[0, :, 0:n - t] = cache_ref[0, :, t:n]
    win_ref[0, :, n - t:n] = jnp.zeros((cache_ref.shape[1], t), F32)


def _shift_specs(caches_t, n_steps):
    last = caches_t[0].shape[0] - 1 if caches_t else 0
    assert not caches_t or n_steps > last
    return [pl.BlockSpec((1,) + c.shape[1:], lambda i: (jnp.minimum(i, last), 0, 0)) for c in caches_t]


def _in_a_kernel(xp_ref, xs_ref, w_ref, cosp_ref, sinp_ref, coss_ref, sins_ref, mkv_ref, *refs,
                 n_p, n_shift, dec_seq):
    caches, outs, wins = refs[:n_shift], refs[n_shift:n_shift + 10], refs[n_shift + 10:]
    i = pl.program_id(0)

    @pl.when(i < n_p)
    def _():
        _in_a_body(xp_ref, w_ref, cosp_ref, sinp_ref, mkv_ref, *outs[:5])
        for c_ref, w_ref_ in zip(caches, wins):
            _shift_window(c_ref, w_ref_, dec_seq)

    @pl.when(i == n_p)
    def _():
        _in_a_body(xs_ref, w_ref, coss_ref, sins_ref, None, *outs[5:])


def _in_a(xp, xs, w_bf, tab_p, tab_s, memkv_t, seq, caches_t, dec_seq):
    tp, d = xp.shape
    ts = xs.shape[0]
    tm = TOKEN_TILE
    n_p = tp // tm
    nt = seq // tm
    cl = lambda i: jnp.minimum(i, n_p - 1)
    row = lambda i: (cl(i), 0)
    tab = lambda i: (cl(i) % nt, 0)
    widths = (MAIN_WIDTH,) * 4 + (MEM_WIDTH,)
    shift_specs = _shift_specs(caches_t, n_p)
    return pl.pallas_call(
        functools.partial(_in_a_kernel, n_p=n_p, n_shift=len(caches_t), dec_seq=dec_seq),
        grid=(n_p + 1,),
        in_specs=[pl.BlockSpec((tm, d), row), _full_spec(xs), _const_spec((1,) + w_bf.shape[1:], (0, 0, 0)),
                  pl.BlockSpec((tm, LANES), tab), pl.BlockSpec((tm, LANES), tab),
                  _full_spec(tab_s[0]), _full_spec(tab_s[1]),
                  pl.BlockSpec((1, 1) + memkv_t.shape[2:], lambda i: (0, cl(i) // nt, 0, 0))] + shift_specs,
        out_specs=[pl.BlockSpec((tm, w), row) for w in widths]
                  + [pl.BlockSpec((ts, w), lambda i: (0, 0)) for w in widths] + shift_specs,
        out_shape=[jax.ShapeDtypeStruct((tp, w), BF16) for w in widths]
                  + [jax.ShapeDtypeStruct((ts, w), F32) for w in widths]
                  + [jax.ShapeDtypeStruct(c.shape, F32) for c in caches_t],
        compiler_params=_params(("arbitrary",)),
        name="in_proj_a",
    )(xp, xs, w_bf, *tab_p, *tab_s, memkv_t, *caches_t)


def _retention_kernel(q_ref, k_ref, v_ref, g_ref, mix_ref, st_ref, dec_ref, rdec_ref, kdec_ref):
    c = pl.program_id(1)
    cs = RET_CHUNK

    @pl.when(c == 0)
    def _():
        st_ref[...] = jnp.zeros_like(st_ref)
        row = lax.broadcasted_iota(jnp.int32, (cs, cs), 0).astype(F32)
        col = lax.broadcasted_iota(jnp.int32, (cs, cs), 1).astype(F32)
        diff = row - col
        for h in range(RET_HEADS):
            lg = LOG_G[h]
            dec_ref[h] = jnp.where(diff >= 0, jnp.exp(jnp.maximum(diff, 0.0) * lg), 0.0)
            rdec_ref[h] = jnp.exp((row + 1.0) * lg)
            kdec_ref[h] = jnp.exp((cs - 1.0 - row) * lg)

    for j in range(q_ref.shape[0] // cs):
        rs = slice(j * cs, (j + 1) * cs)
        for h in range(RET_HEADS):
            hs = slice(h * RET_HEAD_DIM, (h + 1) * RET_HEAD_DIM)
            qh = q_ref[rs, hs]
            kh = k_ref[rs, hs]
            vh = v_ref[rs, hs]
            st = st_ref[0, h]
            s = _dot_nt(qh, kh) * dec_ref[h]
            inner = _dot(s.astype(BF16), vh)
            cross = _dot(qh, st.astype(BF16)) * rdec_ref[h]
            kd = (kh.astype(F32) * kdec_ref[h]).astype(BF16)
            st_ref[0, h] = math.exp(cs * LOG_G[h]) * st + _dot_tn(kd, vh)
            o = inner + cross
            mu = jnp.mean(o, axis=-1, keepdims=True)
            oc = o - mu
            var = jnp.mean(oc * oc, axis=-1, keepdims=True)
            on = oc * lax.rsqrt(var + LN_EPS)
            mix_ref[rs, hs] = (g_ref[rs, hs].astype(F32) * on).astype(mix_ref.dtype)


def _retention(q, k, v, g, batch, seq):
    t = q.shape[0]
    rows = RET_CHUNK * RET_CHUNKS_PER_STEP
    ns = seq // rows
    tok = lambda b, c: (b * ns + c, 0)
    sq = (RET_HEADS, RET_CHUNK, RET_CHUNK)
    return pl.pallas_call(
        _retention_kernel,
        grid=(batch, ns),
        in_specs=[pl.BlockSpec((rows, MAIN_WIDTH), tok)] * 4,
        out_specs=[pl.BlockSpec((rows, MAIN_WIDTH), tok),
                   pl.BlockSpec((1, RET_HEADS, RET_HEAD_DIM, RET_HEAD_DIM), lambda b, c: (b, 0, 0, 0))],
        out_shape=[jax.ShapeDtypeStruct((t, MAIN_WIDTH), BF16),
                   jax.ShapeDtypeStruct((batch, RET_HEADS, RET_HEAD_DIM, RET_HEAD_DIM), F32)],
        scratch_shapes=[pltpu.VMEM(sq, F32), pltpu.VMEM(sq, F32), pltpu.VMEM(sq, F32)],
        compiler_params=_params(("arbitrary", "arbitrary")),
        name="retention",
    )(q, k, v, g)


def _retention_sample_kernel(q_ref, k_ref, v_ref, g_ref, qm_ref, mkv_ref, st_ref, mix_ref, nst_ref):
    t = q_ref.shape[0]
    pad = jnp.zeros((LANES - t, RET_HEAD_DIM), F32)
    row = lax.broadcasted_iota(jnp.int32, (t, LANES), 0).astype(F32)
    col = lax.broadcasted_iota(jnp.int32, (t, LANES), 1).astype(F32)
    prow = lax.broadcasted_iota(jnp.int32, (LANES, RET_HEAD_DIM), 0).astype(F32)
    diff = row - col
    for h in range(RET_HEADS):
        lg = LOG_G[h]
        hs = slice(h * RET_HEAD_DIM, (h + 1) * RET_HEAD_DIM)
        qh = q_ref[:, hs]
        kp = jnp.concatenate([k_ref[:, hs], pad], axis=0)
        vp = jnp.concatenate([v_ref[:, hs], pad], axis=0)
        st = st_ref[0, 0, h]
        dec = jnp.where(diff >= 0, jnp.exp(jnp.maximum(diff, 0.0) * lg), 0.0)
        inner = _dot(_dot_nt(qh, kp) * dec, vp)
        cross = _dot(qh, st) * jnp.exp((row + 1.0) * lg)
        kd = kp * jnp.exp((t - 1.0 - prow) * lg)
        nst_ref[0, h] = math.exp(t * lg) * st + _dot_tn(kd, vp)
        o = inner + cross
        mu = jnp.mean(o, axis=-1, keepdims=True)
        oc = o - mu
        var = jnp.mean(oc * oc, axis=-1, keepdims=True)
        on = oc * lax.rsqrt(var + LN_EPS)
        mix_ref[:, hs] = g_ref[:, hs] * on
    mix_ref[:, MAIN_WIDTH:] = _mem_attn_tile(qm_ref[...], mkv_ref[0, 0])


def _retention_sample(q, k, v, g, qm, memkv_t, state, batch, t):
    tok = lambda b: (b, 0)
    hd = (RET_HEADS, RET_HEAD_DIM, RET_HEAD_DIM)
    return pl.pallas_call(
        _retention_sample_kernel,
        grid=(batch,),
        in_specs=[pl.BlockSpec((t, MAIN_WIDTH), tok)] * 4
                 + [pl.BlockSpec((t, MEM_WIDTH), tok),
                    pl.BlockSpec((1, 1) + memkv_t.shape[2:], lambda b: (0, b, 0, 0)),
                    pl.BlockSpec((1, 1) + hd, lambda b: (0, b, 0, 0, 0))],
        out_specs=[pl.BlockSpec((t, D_MODEL), tok), pl.BlockSpec((1,) + hd, lambda b: (b, 0, 0, 0))],
        out_shape=[jax.ShapeDtypeStruct((batch * t, D_MODEL), F32),
                   jax.ShapeDtypeStruct((batch,) + hd, F32)],
        compiler_params=_params(("parallel",)),
        name="retention_sample",
    )(q, k, v, g, qm, memkv_t, state)


def _dilated_sample_group(q, cache, new, window, dil):
    t = q.shape[0]
    n_buf = cache.shape[1]
    gw = GROUP_WIDTH
    q4 = _stack_heads(q, DIL_HEAD_DIM)
    rows = q4.shape[0]
    newp = jnp.concatenate([new, jnp.zeros((LANES - t, new.shape[1]), F32)], axis=0).astype(BF16)

    def masked(s, first_index):
        key = lax.broadcasted_iota(jnp.int32, s.shape, 1) + first_index
        tok = lax.broadcasted_iota(jnp.int32, s.shape, 0) & (t - 1)
        delta = n_buf + tok - key
        valid = (delta >= 0) & (delta <= window) & ((delta & (dil - 1)) == 0)
        return jnp.where(valid, s, NEG_BIG)

    s_c = masked(_dot(q4, cache[:gw].astype(BF16)), 0)
    s_n = masked(_dot_nt(q4, newp[:, :gw]), n_buf)
    m = jnp.maximum(jnp.max(s_c, axis=-1, keepdims=True), jnp.max(s_n, axis=-1, keepdims=True))
    e_c = jnp.exp(s_c - m)
    e_n = jnp.exp(s_n - m)
    l = jnp.sum(e_c, axis=-1, keepdims=True) + jnp.sum(e_n, axis=-1, keepdims=True)
    inv = 1.0 / l
    full = (_dot_nt((e_c * inv).astype(BF16), cache[gw:].astype(BF16))
            + _dot((e_n * inv).astype(BF16), newp[:, gw:]))
    o = _pick_heads(full, t, DIL_HEAD_DIM)
    lse = _pick_heads(jnp.broadcast_to(m + jnp.log(l), (rows, gw)), t, DIL_HEAD_DIM)
    return o, lse


def _sample_mixer_kernel(*refs):
    q_refs, cache_refs, new_refs, new_t_refs = refs[0:3], refs[3:6], refs[6:9], refs[9:12]
    qm_ref, mkv_ref = refs[12], refs[13]
    shifted_refs, mix_ref, win_refs = refs[14:17], refs[17], refs[18:21]
    gw = GROUP_WIDTH
    outs, lses = [], []
    for g, (window, dil) in enumerate(DIL_PAIRS):
        t = q_refs[g].shape[0]
        o, lse = _dilated_sample_group(q_refs[g][...], cache_refs[g][0], new_refs[g][...], window, dil)
        outs.append(o)
        lses.append(lse)
        win_refs[g][...] = shifted_refs[g][...]
        win_refs[g][0, :, LANES - t:] = new_t_refs[g][0]
    m = jnp.maximum(jnp.maximum(lses[0], lses[1]), lses[2])
    es = [jnp.exp(v - m) for v in lses]
    inv = 1.0 / (es[0] + es[1] + es[2])
    for g in range(3):
        mix_ref[:, g * gw:(g + 1) * gw] = outs[g] * (es[g] * inv)
    mix_ref[:, MAIN_WIDTH:] = _mem_attn_tile(qm_ref[...], mkv_ref[0, 0])


def _sample_mixer(q_g, caches_t, new_kv, new_kv_t, qm, memkv_t, shifted, batch, t):
    gw = GROUP_WIDTH
    tok = lambda b: (b, 0)
    big = lambda b: (b, 0, 0)
    cache_specs = [pl.BlockSpec((1,) + c.shape[1:], big) for c in caches_t]
    tail_specs = [pl.BlockSpec((1, c.shape[1], LANES), lambda b, j=c.shape[2] // LANES - 1: (b, 0, j))
                  for c in shifted]
    first_shifted = 3 + len(caches_t) + 3 + 3 + 2
    return pl.pallas_call(
        _sample_mixer_kernel,
        grid=(batch,),
        in_specs=[pl.BlockSpec((t, gw), tok)] * 3 + cache_specs + [pl.BlockSpec((t, 2 * gw), tok)] * 3
                 + [pl.BlockSpec((1, 2 * gw, t), big)] * 3
                 + [pl.BlockSpec((t, MEM_WIDTH), tok),
                    pl.BlockSpec((1, 1) + memkv_t.shape[2:], lambda b: (1, b, 0, 0))] + tail_specs,
        out_specs=[pl.BlockSpec((t, D_MODEL), tok)] + tail_specs,
        out_shape=[jax.ShapeDtypeStruct((batch * t, D_MODEL), F32)]
                  + [jax.ShapeDtypeStruct(c.shape, F32) for c in shifted],
        input_output_aliases={first_shifted + g: 1 + g for g in range(len(shifted))},
        compiler_params=_params(("parallel",)),
        name="sample_mixer",
    )(*q_g, *caches_t, *new_kv, *new_kv_t, qm, memkv_t, *shifted)


def _natural_rows(ref, dil, scr):
    if dil == 1:
        return ref[0, 0].astype(F32)
    n = ref.shape[2]
    for r in range(dil):
        v = ref[0, r].astype(F32)
        for c in range(v.shape[1] // LANES):
            scr[c, pl.ds(r, n, stride=dil), :] = v[:, c * LANES:(c + 1) * LANES]
    return jnp.concatenate([scr[c] for c in range(scr.shape[0])], axis=1)


def _post_body(mix, x_ref, wo_ref, g1_ref, b1_ref, wi_ref, w2_ref, g2_ref, b2_ref, out_ref, act_ref):
    rows = x_ref.shape[0]
    x1 = _layer_norm(ALPHA * x_ref[...] + _dot(mix, wo_ref[0]), g1_ref[0], b1_ref[0])
    x1b = x1.astype(BF16)
    for c in range(0, FFN_HIDDEN, FFN_COL_CHUNK):
        gate = _dot(x1b, wi_ref[0, :, c:c + FFN_COL_CHUNK])
        up = _dot(x1b, wi_ref[0, :, FFN_HIDDEN + c:FFN_HIDDEN + c + FFN_COL_CHUNK])
        act_ref[0:rows, c:c + FFN_COL_CHUNK] = (_silu(gate) * up).astype(BF16)
    y = _dot(act_ref[0:rows, :], w2_ref[0])
    out_ref[...] = _layer_norm(ALPHA * x1 + y, g2_ref[0], b2_ref[0])


def _post_kernel(*refs, dils, n_mix_p, n_mix_s, n_p, n_shift, dec_seq):
    mixp_refs, refs = refs[:n_mix_p], refs[n_mix_p:]
    xp_ref, refs = refs[0], refs[1:]
    mixs_refs, refs = refs[:n_mix_s], refs[n_mix_s:]
    xs_ref, refs = refs[0], refs[1:]
    weights, refs = refs[:7], refs[7:]
    caches, refs = refs[:n_shift], refs[n_shift:]
    outp_ref, outs_ref, refs = refs[0], refs[1], refs[2:]
    wins, refs = refs[:n_shift], refs[n_shift:]
    act_ref = refs[0]
    scrs = list(refs[1:])
    i = pl.program_id(0)

    @pl.when(i < n_p)
    def _():
        if dils is not None:
            o_refs, l_refs, mem_ref = mixp_refs[0:3], mixp_refs[3:6], mixp_refs[6]
            pool = list(scrs)
            nat = lambda r, d: _natural_rows(r, d, pool.pop(0) if d > 1 else None)
            os_ = [nat(o_refs[g], dils[g]) for g in range(3)]
            ls = [nat(l_refs[g], dils[g]) for g in range(3)]
            m = jnp.maximum(jnp.maximum(ls[0], ls[1]), ls[2])
            es = [jnp.exp(v - m) for v in ls]
            inv = 1.0 / (es[0] + es[1] + es[2])
            parts = [(os_[g] * (es[g] * inv)).astype(BF16) for g in range(3)]
            parts.append(mem_ref[...].astype(BF16))
        else:
            parts = [r[...].astype(BF16) for r in mixp_refs]
        mix = parts[0] if len(parts) == 1 else jnp.concatenate(parts, axis=1)
        _post_body(mix, xp_ref, *weights, outp_ref, act_ref)
        for c_ref, w_ref in zip(caches, wins):
            _shift_window(c_ref, w_ref, dec_seq)

    @pl.when(i == n_p)
    def _():
        parts = [r[...].astype(BF16) for r in mixs_refs]
        mix = parts[0] if len(parts) == 1 else jnp.concatenate(parts, axis=1)
        _post_body(mix, xs_ref, *weights, outs_ref, act_ref)


def _post(layer, mix_p, xp, mix_s, xs, wo, g1, b1, wi, w2, g2, b2, dils, seq, caches_t=(), dec_seq=0):
    tp, d = xp.shape
    tm = TOKEN_TILE
    n_p = tp // tm
    nt = seq // tm
    cl = lambda i: jnp.minimum(i, n_p - 1)
    row = lambda i: (cl(i), 0)
    if dils is None:
        mixp_specs = [pl.BlockSpec((tm, a.shape[1]), row) for a in mix_p]
        scratch = []
    else:
        res = lambda i: (cl(i) // nt, 0, cl(i) % nt, 0)
        mixp_specs = [pl.BlockSpec((1, dl, tm // dl, GROUP_WIDTH), res) for dl in dils] * 2
        mixp_specs.append(pl.BlockSpec((tm, MEM_WIDTH), row))
        scratch = [pltpu.VMEM((GROUP_WIDTH // LANES, tm, LANES), F32) for dl in dils * 2 if dl > 1]
    lsel = (layer, 0, 0)
    vec = _const_spec((1, 1, d), lsel)
    shift_specs = _shift_specs(caches_t, n_p)
    return pl.pallas_call(
        functools.partial(_post_kernel, dils=dils, n_mix_p=len(mix_p), n_mix_s=len(mix_s), n_p=n_p,
                          n_shift=len(caches_t), dec_seq=dec_seq),
        grid=(n_p + 1,),
        in_specs=mixp_specs + [pl.BlockSpec((tm, d), row)] + [_full_spec(a) for a in mix_s] + [_full_spec(xs)]
                 + [_const_spec((1,) + wo.shape[1:], lsel), vec, vec,
                    _const_spec((1,) + wi.shape[1:], lsel), _const_spec((1,) + w2.shape[1:], lsel), vec, vec]
                 + shift_specs,
        out_specs=[pl.BlockSpec((tm, d), row), pl.BlockSpec(xs.shape, lambda i: (0, 0))] + shift_specs,
        out_shape=[jax.ShapeDtypeStruct((tp, d), F32), jax.ShapeDtypeStruct(xs.shape, F32)]
                  + [jax.ShapeDtypeStruct(c.shape, F32) for c in caches_t],
        scratch_shapes=[pltpu.VMEM((tm, FFN_HIDDEN), BF16)] + scratch,
        compiler_params=_params(("arbitrary",)),
        name="post" if dils is None else "post_combine",
    )(*mix_p, xp, *mix_s, xs, wo, g1, b1, wi, w2, g2, b2, *caches_t)


def _residue_major(ref, y, dil, scr):
    if dil == 1:
        ref[...] = y.astype(ref.dtype).reshape(ref.shape)
        return
    n = ref.shape[2]
    ncol = y.shape[1] // LANES
    for c in range(ncol):
        scr[c] = y[:, c * LANES:(c + 1) * LANES]
    for r in range(dil):
        rows = [scr[c, pl.ds(r, n, stride=dil), :] for c in range(ncol)]
        ref[0, r] = jnp.concatenate(rows, axis=1).astype(ref.dtype)


def _in_b_body(x_ref, wkv_ref, wq_ref, wkvt_ref, cos_ref, sin_ref, cos_t_ref, sin_t_ref, mkv_ref,
               q_refs, kv_refs, m_ref, win_refs, scrs, dils, win_preds):
    xb = x_ref[...].astype(BF16)
    cos = cos_ref[...]
    sin = sin_ref[...]
    gw = GROUP_WIDTH
    mw = MAIN_WIDTH
    half = DIL_HEAD_DIM // 2
    tm = x_ref.shape[0]
    pool = list(scrs)
    for g in range(3):
        ks = slice(g * gw, (g + 1) * gw)
        vs = slice(mw + g * gw, mw + (g + 1) * gw)
        k = _rope_cols(_dot(xb, wkv_ref[:, ks]), cos, sin, half)
        v = _dot(xb, wkv_ref[:, vs])
        _residue_major(kv_refs[g], jnp.concatenate([k, v], axis=1), dils[g], pool.pop(0) if dils[g] > 1 else None)
        q = _rope_cols(_dot(xb, wq_ref[0, :, ks]), cos, sin, half) * ATTN_SCALE
        _residue_major(q_refs[g], q, dils[g], pool.pop(0) if dils[g] > 1 else None)

        def window(g=g, ks=ks, vs=vs):
            k_t = _rope_rows(_dot_nt(wkvt_ref[ks, :], xb), cos_t_ref[...], sin_t_ref[...], DIL_HEAD_DIM)
            v_t = _dot_nt(wkvt_ref[vs, :], xb)
            full = jnp.concatenate([k_t, v_t], axis=0)
            win_refs[g][0] = full[:, tm - win_refs[g].shape[2]:]

        if win_preds[g] is None:
            window()
        else:
            pl.when(win_preds[g])(window)
    qm = _dot(xb, wq_ref[0, :, mw:mw + MEM_WIDTH]) * ATTN_SCALE
    if mkv_ref is not None:
        m_ref[...] = _mem_attn_tile(qm.astype(BF16), mkv_ref[0, 0]).astype(m_ref.dtype)
    else:
        m_ref[...] = qm.astype(m_ref.dtype)


def _in_b_kernel(xp_ref, xs_ref, wkv_ref, wq_ref, wkvt_ref, cosp_ref, sinp_ref, cospt_ref, sinpt_ref,
                 coss_ref, sins_ref, cosst_ref, sinst_ref, mkv_ref, *refs, dils, win_tiles, nt, n_p):
    outp, outs, scrs = refs[0:10], refs[10:20], refs[20:]
    i = pl.program_id(0)

    @pl.when(i < n_p)
    def _():
        j = i % nt
        _in_b_body(xp_ref, wkv_ref, wq_ref, wkvt_ref, cosp_ref, sinp_ref, cospt_ref, sinpt_ref, mkv_ref,
                   outp[0:3], outp[3:6], outp[6], outp[7:10], scrs, dils,
                   [j >= nt - w for w in win_tiles])

    @pl.when(i == n_p)
    def _():
        _in_b_body(xs_ref, wkv_ref, wq_ref, wkvt_ref, coss_ref, sins_ref, cosst_ref, sinst_ref, None,
                   outs[0:3], outs[3:6], outs[6], outs[7:10], (), (1, 1, 1), [None] * 3)


def _in_b(xp, xs, w_kv, w_q, w_kvt, tabs_p, tabs_pt, tabs_s, tabs_st, memkv_t, batch, seq, dils, windows):
    tp, d = xp.shape
    ts = xs.shape[0]
    gw = GROUP_WIDTH
    tm = TOKEN_TILE
    n_p = tp // tm
    nt = seq // tm
    cl = lambda i: jnp.minimum(i, n_p - 1)
    row = lambda i: (cl(i), 0)
    tab = lambda i: (cl(i) % nt, 0)
    tab_t = lambda i: (0, cl(i) % nt)
    res = lambda i: (cl(i) // nt, 0, cl(i) % nt, 0)
    fix2 = lambda i: (0, 0)
    out_specs = ([pl.BlockSpec((1, dl, tm // dl, gw), res) for dl in dils]
                 + [pl.BlockSpec((1, dl, tm // dl, 2 * gw), res) for dl in dils]
                 + [pl.BlockSpec((tm, MEM_WIDTH), row)])
    out_shape = ([jax.ShapeDtypeStruct((batch, dl, seq // dl, gw), BF16) for dl in dils]
                 + [jax.ShapeDtypeStruct((batch, dl, seq // dl, 2 * gw), BF16) for dl in dils]
                 + [jax.ShapeDtypeStruct((tp, MEM_WIDTH), BF16)])
    win_tiles = []
    for w in windows:
        rb = min(w, tm)
        nblk = w // rb
        win_tiles.append(nblk)
        out_specs.append(pl.BlockSpec(
            (1, 2 * gw, rb), lambda i, nblk=nblk: (cl(i) // nt, 0, jnp.maximum(cl(i) % nt - (nt - nblk), 0))))
        out_shape.append(jax.ShapeDtypeStruct((batch, 2 * gw, w), F32))
    out_specs += ([pl.BlockSpec((ts, gw), fix2)] * 3 + [pl.BlockSpec((ts, 2 * gw), fix2)] * 3
                  + [pl.BlockSpec((ts, MEM_WIDTH), fix2)] + [pl.BlockSpec((1, 2 * gw, ts), lambda i: (0, 0, 0))] * 3)
    out_shape += ([jax.ShapeDtypeStruct((ts, gw), F32)] * 3 + [jax.ShapeDtypeStruct((ts, 2 * gw), F32)] * 3
                  + [jax.ShapeDtypeStruct((ts, MEM_WIDTH), F32)] + [jax.ShapeDtypeStruct((1, 2 * gw, ts), F32)] * 3)
    scratch = []
    for dl in dils:
        if dl > 1:
            scratch += [pltpu.VMEM((2 * gw // LANES, tm, LANES), F32), pltpu.VMEM((gw // LANES, tm, LANES), F32)]
    return pl.pallas_call(
        functools.partial(_in_b_kernel, dils=tuple(dils), win_tiles=tuple(win_tiles), nt=nt, n_p=n_p),
        grid=(n_p + 1,),
        in_specs=[pl.BlockSpec((tm, d), row), _full_spec(xs), _const_spec(w_kv.shape, (0, 0)),
                  _const_spec((1,) + w_q.shape[1:], (0, 0, 0)), _const_spec(w_kvt.shape, (0, 0)),
                  pl.BlockSpec((tm, LANES), tab), pl.BlockSpec((tm, LANES), tab),
                  pl.BlockSpec((DIL_HEAD_DIM, tm), tab_t), pl.BlockSpec((DIL_HEAD_DIM, tm), tab_t),
                  _full_spec(tabs_s[0]), _full_spec(tabs_s[1]), _full_spec(tabs_st[0]), _full_spec(tabs_st[1]),
                  pl.BlockSpec((1, 1) + memkv_t.shape[2:], lambda i: (1, cl(i) // nt, 0, 0))],
        out_specs=out_specs,
        out_shape=out_shape,
        scratch_shapes=scratch,
        compiler_params=_params(("arbitrary",)),
        name="in_proj_b",
    )(xp, xs, w_kv, w_q, w_kvt, *tabs_p, *tabs_pt, *tabs_s, *tabs_st, memkv_t)


def _dilated_kernel(q_ref, kv_ref, o_ref, lse_ref, ring_ref):
    i = pl.program_id(2)
    blk = DIL_BLOCK
    gw = GROUP_WIDTH
    nsub = q_ref.shape[2] // blk
    slot = i & 1

    @pl.when(i == 0)
    def _():
        ring_ref[1] = jnp.zeros(ring_ref.shape[1:], ring_ref.dtype)

    cur = kv_ref[0, 0]
    ring_ref[slot] = cur[(nsub - 1) * blk:]
    kext = jnp.concatenate([ring_ref[1 - slot], cur], axis=0)
    rows = GROUP_HEADS * blk
    qi = lax.broadcasted_iota(jnp.int32, (rows, 2 * blk), 0) & (blk - 1)
    kj = lax.broadcasted_iota(jnp.int32, (rows, 2 * blk), 1)
    delta = qi + blk - kj
    band = (delta >= 0) & (delta <= blk)
    for j in range(nsub):
        rs = slice(j * blk, (j + 1) * blk)
        q4 = _stack_heads(q_ref[0, 0, rs, :], DIL_HEAD_DIM)
        kv = kext[j * blk:(j + 2) * blk]
        valid = band if j > 0 else band & ((i > 0) | (kj >= blk))
        s = jnp.where(valid, _dot_nt(q4, kv[:, :gw]), NEG_BIG)
        m = jnp.max(s, axis=-1, keepdims=True)
        e = jnp.exp(s - m)
        l = jnp.sum(e, axis=-1, keepdims=True)
        p = (e * (1.0 / l)).astype(BF16)
        full = _dot(p, kv[:, gw:])
        o_ref[0, 0, rs, :] = _pick_heads(full, blk, DIL_HEAD_DIM).astype(o_ref.dtype)
        lse_ref[0, 0, rs, :] = _pick_heads(jnp.broadcast_to(m + jnp.log(l), (rows, gw)), blk, DIL_HEAD_DIM)


def _dilated(q, kv, dil):
    batch, _, m, gw = q.shape
    rows = DIL_BLOCK * min(DIL_BLOCKS_PER_STEP, m // DIL_BLOCK)
    idx = lambda b, r, i: (b, r, i, 0)
    return pl.pallas_call(
        _dilated_kernel,
        grid=(batch, dil, m // rows),
        in_specs=[pl.BlockSpec((1, 1, rows, gw), idx), pl.BlockSpec((1, 1, rows, 2 * gw), idx)],
        out_specs=[pl.BlockSpec((1, 1, rows, gw), idx)] * 2,
        out_shape=[jax.ShapeDtypeStruct(q.shape, BF16), jax.ShapeDtypeStruct(q.shape, F32)],
        scratch_shapes=[pltpu.VMEM((2, DIL_BLOCK, 2 * gw), BF16)],
        compiler_params=_params(("arbitrary", "arbitrary", "arbitrary")),
        name=f"dilated_d{dil}",
    )(q, kv)


def _rope_tables(base, n):
    split = min(ROPE_SPLIT, n)
    hi = base + split * jnp.arange(n // split, dtype=F32)
    lo = jnp.arange(split, dtype=F32)
    inv = ROPE_THETA ** (-jnp.arange(0, RET_HEAD_DIM, 2, dtype=F32) / RET_HEAD_DIM)
    inv_h = inv[0::2]
    neg = lambda m: jnp.concatenate([-jnp.ones((m,), F32), jnp.ones((m,), F32)])

    def token_major(freq, sign):
        xa, xb = hi[:, None] * freq[None, :], lo[:, None] * freq[None, :]
        ca, sa, cb, sb = jnp.cos(xa), jnp.sin(xa), jnp.cos(xb), jnp.sin(xb)
        cos = ca[:, None, :] * cb[None, :, :] - sa[:, None, :] * sb[None, :, :]
        sin = (sa[:, None, :] * cb[None, :, :] + ca[:, None, :] * sb[None, :, :]) * sign
        return cos.reshape(n, -1), sin.reshape(n, -1)

    def feature_major(freq, sign):
        xa, xb = freq[:, None] * hi[None, :], freq[:, None] * lo[None, :]
        ca, sa, cb, sb = jnp.cos(xa), jnp.sin(xa), jnp.cos(xb), jnp.sin(xb)
        cos = ca[:, :, None] * cb[:, None, :] - sa[:, :, None] * sb[:, None, :]
        sin = (sa[:, :, None] * cb[:, None, :] + ca[:, :, None] * sb[:, None, :]) * sign[:, None, None]
        return cos.reshape(-1, n), sin.reshape(-1, n)

    f_big = jnp.concatenate([inv, inv])
    f_half = jnp.concatenate([inv_h, inv_h])
    return (token_major(f_big, neg(RET_HEAD_DIM // 2)),
            token_major(jnp.tile(f_half, 2), jnp.tile(neg(DIL_HEAD_DIM // 2), 2)),
            feature_major(f_half, neg(DIL_HEAD_DIM // 2)))


def _to_feature_major(x5):
    b, w = x5.shape[0], x5.shape[1]
    return jnp.transpose(x5, (0, 2, 3, 4, 1)).reshape(b, 2 * GROUP_WIDTH, w)


def _from_feature_major(xt):
    b, _, w = xt.shape
    return jnp.transpose(xt.reshape(b, 2, GROUP_HEADS, DIL_HEAD_DIM, w), (0, 4, 1, 2, 3))


def kernel(x_prompt, x_sample, mem_prompt, cache_mem_kv, state_ret, cache_win_kv_g1, cache_win_kv_g2, cache_win_kv_g3, w_in_a, w_in_b, w_out, w_kv_shared, w_mem_kv, ln_mix_g, ln_mix_b, ln_ffn_g, ln_ffn_b, w_ffn_in, w_ffn_out):
    batch, seq, d = x_prompt.shape
    dec_batch, dec_seq, _ = x_sample.shape
    n_mem = mem_prompt.shape[1]
    gw = GROUP_WIDTH
    ts = dec_batch * dec_seq
    win_caches = (cache_win_kv_g1, cache_win_kv_g2, cache_win_kv_g3)
    dils = tuple(dl for _, dl in DIL_PAIRS)

    tab_a_p, tab_b_p, tab_bt_p = _rope_tables(0.0, seq)
    tab_a_s, tab_b_s, tab_bt_s = _rope_tables(float(PAST_LEN), dec_seq)
    tab_a_s = tuple(jnp.tile(a, (dec_batch, 1)) for a in tab_a_s)
    tab_b_s = tuple(jnp.tile(a, (dec_batch, 1)) for a in tab_b_s)
    tab_bt_s = tuple(jnp.tile(a, (1, dec_batch)) for a in tab_bt_s)

    w_a = w_in_a.astype(BF16)
    w_kv = w_kv_shared.astype(BF16)
    w_kvt = w_kv_shared.T.astype(BF16)
    w_q = w_in_b.astype(BF16)
    w_o = w_out.astype(BF16)
    w_fi = w_ffn_in.astype(BF16)
    w_fo = w_ffn_out.astype(BF16)
    w_mem_t = jnp.transpose(w_mem_kv, (0, 2, 1)).astype(BF16)
    lnv = lambda a: a.reshape(DEPTH, 1, d)
    ln = (lnv(ln_mix_g), lnv(ln_mix_b), lnv(ln_ffn_g), lnv(ln_ffn_b))

    memkv_p = _mem_proj(mem_prompt, w_mem_t)
    memkv_s = jnp.transpose(cache_mem_kv, (0, 1, 3, 4, 5, 2)).reshape(DEPTH, dec_batch, 2 * MEM_WIDTH, n_mem)

    xp = x_prompt.reshape(batch * seq, d)
    xs = x_sample.reshape(ts, d)

    def post(l, mix_p, mix_s, dls, shift=()):
        return _post(l, mix_p, xp, mix_s, xs, w_o, ln[0], ln[1], w_fi, w_fo, ln[2], ln[3], dls, seq,
                     shift, dec_seq)

    caches_t = [_to_feature_major(c) for c in win_caches]

    outs = _in_a(xp, xs, w_a, tab_a_p, tab_a_s, memkv_p, seq, caches_t[:2], dec_seq)
    q, k, v, g, mem_o = outs[:5]
    mix_p, state_p = _retention(q, k, v, g, batch, seq)
    q, k, v, g, qm = outs[5:10]
    shifted = list(outs[10:])
    mix_s, state_s = _retention_sample(q, k, v, g, qm, memkv_s, state_ret, dec_batch, dec_seq)
    xp, xs, *shifted_big = post(0, [mix_p, mem_o], [mix_s], None, caches_t[2:])
    shifted += shifted_big

    windows_p = tuple(min(w, seq) for w, _ in DIL_PAIRS)
    outs = _in_b(xp, xs, w_kv, w_q, w_kvt, tab_b_p, tab_bt_p, tab_b_s, tab_bt_s, memkv_p,
                 batch, seq, dils, windows_p)
    q_g, kv_g, mem_o, win_p = outs[0:3], outs[3:6], outs[6], outs[7:10]
    att = [_dilated(q_g[i], kv_g[i], dils[i]) for i in range(3)]

    q_s, kv_new, qm, new_t = outs[10:13], outs[13:16], outs[16], outs[17:20]
    new_t = [jnp.transpose(a.reshape(2 * gw, dec_batch, dec_seq), (1, 0, 2)) for a in new_t]
    mix_s, *win_s = _sample_mixer(q_s, caches_t, kv_new, new_t, qm, memkv_s, shifted, dec_batch, dec_seq)

    xp, xs = post(1, [a[0] for a in att] + [a[1] for a in att] + [mem_o], [mix_s], dils)

    memkv_out = jnp.transpose(memkv_p.reshape(DEPTH, batch, 2, MEM_HEADS, MEM_HEAD_DIM, n_mem), (0, 1, 5, 2, 3, 4))
    return (xp.reshape(batch, seq, d), xs.reshape(dec_batch, dec_seq, d),
            state_p[None], state_s[None], memkv_out,
            _from_feature_major(win_p[0]), _from_feature_major(win_p[1]), _from_feature_major(win_p[2]),
            _from_feature_major(win_s[0]), _from_feature_major(win_s[1]), _from_feature_major(win_s[2]))
```

```python
import functools
import math

import jax
import jax.numpy as jnp
from jax import lax
from jax.experimental import pallas as pl
from jax.experimental.pallas import tpu as pltpu

F32 = jnp.float32
BF16 = jnp.bfloat16

D_MODEL = 1024
MEM_HEADS = 4
MEM_HEAD_DIM = 64
MEM_WIDTH = MEM_HEADS * MEM_HEAD_DIM
MAIN_WIDTH = D_MODEL - MEM_WIDTH
RET_HEADS = 6
RET_HEAD_DIM = MAIN_WIDTH // RET_HEADS
RET_CHUNK = 128
RET_CHUNKS_PER_STEP = 8
DIL_PAIRS = ((128, 1), (512, 4), (2048, 16))
GROUP_HEADS = 4
DIL_HEAD_DIM = 64
GROUP_WIDTH = GROUP_HEADS * DIL_HEAD_DIM
DIL_BLOCK = 128
DIL_BLOCKS_PER_STEP = 8
FFN_HIDDEN = 2816
ROPE_THETA = 10000.0
ROPE_SPLIT = 64
LN_EPS = 1e-5
DEPTH = 2
ALPHA = (2 * DEPTH) ** 0.25
PAST_LEN = 8192
NEG_BIG = -1e30
ATTN_SCALE = DIL_HEAD_DIM ** -0.5

LANES = 128
BF16_SUBLANES = 16
SAMPLE_BATCH_PER_STEP = 2
VMEM_LIMIT = 56 * 1024 * 1024
FFN_COL_CHUNK = 256
TOKEN_TILE = 512

LOG_G = tuple(math.log1p(-(2.0 ** (-5.0 - h))) for h in range(RET_HEADS))


def _dot(a, b):
    return jnp.dot(a, b, preferred_element_type=F32)


def _dot_nt(a, b):
    return lax.dot_general(a, b, (((1,), (1,)), ((), ())), preferred_element_type=F32)


def _dot_tn(a, b):
    return lax.dot_general(a, b, (((0,), (0,)), ((), ())), preferred_element_type=F32)


def _silu(x):
    return x / (1.0 + jnp.exp(-x))


def _layer_norm(z, g, b):
    mu = jnp.mean(z, axis=-1, keepdims=True)
    zc = z - mu
    var = jnp.mean(zc * zc, axis=-1, keepdims=True)
    return zc * lax.rsqrt(var + LN_EPS) * g + b


def _rope_lanes(y, cos, sin_signed, half):
    if 2 * half == LANES:
        partner = pltpu.roll(y, half, 1)
    else:
        lane = lax.broadcasted_iota(jnp.int32, y.shape, 1)
        first = (lane & (2 * half - 1)) < half
        partner = jnp.where(first, pltpu.roll(y, LANES - half, 1), pltpu.roll(y, half, 1))
    return y * cos + partner * sin_signed


def _rope_cols(y, cos, sin_signed, half):
    parts = [_rope_lanes(y[:, j:j + LANES], cos, sin_signed, half) for j in range(0, y.shape[1], LANES)]
    return parts[0] if len(parts) == 1 else jnp.concatenate(parts, axis=1)


def _rope_rows(yt, cos_t, sin_t, head_dim):
    half = head_dim // 2
    parts = []
    for r0 in range(0, yt.shape[0], head_dim):
        blk = yt[r0:r0 + head_dim]
        swapped = jnp.concatenate([blk[half:], blk[:half]], axis=0)
        parts.append(blk * cos_t + swapped * sin_t)
    return jnp.concatenate(parts, axis=0)


def _head_mask(shape, h, width):
    lane = lax.broadcasted_iota(jnp.int32, shape, len(shape) - 1)
    return (lane >= h * width) & (lane < (h + 1) * width)


def _stack_heads(q, width):
    zero = jnp.zeros((), q.dtype)
    parts = [jnp.where(_head_mask(q.shape, h, width), q, zero) for h in range(q.shape[1] // width)]
    return jnp.concatenate(parts, axis=0).astype(BF16)


def _pick_heads(full, t, width):
    per = LANES // width
    cols = []
    for c in range(full.shape[1] // LANES):
        out = None
        for i in range(per):
            h = c * per + i
            blk = full[h * t:(h + 1) * t, c * LANES:(c + 1) * LANES]
            out = blk if out is None else jnp.where(_head_mask(blk.shape, i, width), blk, out)
        cols.append(out)
    return cols[0] if len(cols) == 1 else jnp.concatenate(cols, axis=1)


def _const_spec(block, index):
    return pl.BlockSpec(block, lambda *_: index, pipeline_mode=pl.Buffered(1))


def _full_spec(a):
    return pl.BlockSpec(a.shape, lambda *_: (0,) * a.ndim)


def _params(sem):
    return pltpu.CompilerParams(dimension_semantics=sem, vmem_limit_bytes=VMEM_LIMIT)


def _mem_proj_kernel(m_ref, w_ref, o_ref):
    o_ref[0, 0] = _dot_nt(w_ref[0], m_ref[0].astype(BF16))


def _mem_proj(mem, w_t):
    batch, n_mem, d = mem.shape
    depth, n, _ = w_t.shape
    return pl.pallas_call(
        _mem_proj_kernel,
        grid=(depth, batch),
        in_specs=[pl.BlockSpec((1, n_mem, d), lambda l, b: (b, 0, 0)),
                  pl.BlockSpec((1, n, d), lambda l, b: (l, 0, 0))],
        out_specs=pl.BlockSpec((1, 1, n, n_mem), lambda l, b: (l, b, 0, 0)),
        out_shape=jax.ShapeDtypeStruct((depth, batch, n, n_mem), F32),
        compiler_params=_params(("arbitrary", "arbitrary")),
        name="mem_proj",
    )(mem, w_t)


def _mem_attn_tile(qm, kv_t):
    t = qm.shape[0]
    q4 = _stack_heads(qm, MEM_HEAD_DIM)
    k_t = kv_t[:MEM_WIDTH].astype(BF16)
    v_t = kv_t[MEM_WIDTH:].astype(BF16)
    s = _dot(q4, k_t)
    m = jnp.max(s, axis=-1, keepdims=True)
    e = jnp.exp(s - m)
    p = (e / jnp.sum(e, axis=-1, keepdims=True)).astype(BF16)
    return _pick_heads(_dot_nt(p, v_t), t, MEM_HEAD_DIM)


def _in_a_body(x_ref, w_ref, cos_ref, sin_ref, mkv_ref, q_ref, k_ref, v_ref, g_ref, m_ref):
    xb = x_ref[...].astype(BF16)
    cos = cos_ref[...]
    sin = sin_ref[...]
    mw = MAIN_WIDTH
    half = RET_HEAD_DIM // 2
    q = _rope_cols(_dot(xb, w_ref[0, :, 0:mw]), cos, sin, half)
    q_ref[...] = q.astype(q_ref.dtype)
    k = _rope_cols(_dot(xb, w_ref[0, :, mw:2 * mw]), cos, sin, half) * (RET_HEAD_DIM ** -0.5)
    k_ref[...] = k.astype(k_ref.dtype)
    v_ref[...] = _dot(xb, w_ref[0, :, 2 * mw:3 * mw]).astype(v_ref.dtype)
    g_ref[...] = _silu(_dot(xb, w_ref[0, :, 3 * mw:4 * mw])).astype(g_ref.dtype)
    qm = _dot(xb, w_ref[0, :, 4 * mw:4 * mw + MEM_WIDTH]) * ATTN_SCALE
    if mkv_ref is not None:
        m_ref[...] = _mem_attn_tile(qm.astype(BF16), mkv_ref[0, 0]).astype(m_ref.dtype)
    else:
        m_ref[...] = qm.astype(m_ref.dtype)


def _shift_window(cache_ref, win_ref, t):
    n = cache_ref.shape[2]
    win_ref[0, :, 0:n - t] = cache_ref[0, :, t:n]
    win_ref[0, :, n - t:n] = jnp.zeros((cache_ref.shape[1], t), F32)


def _shift_specs(caches_t, n_steps):
    last = caches_t[0].shape[0] - 1 if caches_t else 0
    assert not caches_t or n_steps > last
    return [pl.BlockSpec((1,) + c.shape[1:], lambda i: (jnp.minimum(i, last), 0, 0)) for c in caches_t]


def _in_a_kernel(xp_ref, xs_ref, w_ref, cosp_ref, sinp_ref, coss_ref, sins_ref, mkv_ref, *refs,
                 n_p, n_shift, dec_seq):
    caches, outs, wins = refs[:n_shift], refs[n_shift:n_shift + 10], refs[n_shift + 10:]
    i = pl.program_id(0)

    @pl.when(i < n_p)
    def _():
        _in_a_body(xp_ref, w_ref, cosp_ref, sinp_ref, mkv_ref, *outs[:5])
        for c_ref, w_ref_ in zip(caches, wins):
            _shift_window(c_ref, w_ref_, dec_seq)

    @pl.when(i == n_p)
    def _():
        _in_a_body(xs_ref, w_ref, coss_ref, sins_ref, None, *outs[5:])


def _in_a(xp, xs, w_bf, tab_p, tab_s, memkv_t, seq, caches_t, dec_seq):
    tp, d = xp.shape
    ts = xs.shape[0]
    tm = TOKEN_TILE
    n_p = tp // tm
    nt = seq // tm
    cl = lambda i: jnp.minimum(i, n_p - 1)
    row = lambda i: (cl(i), 0)
    tab = lambda i: (cl(i) % nt, 0)
    widths = (MAIN_WIDTH,) * 4 + (MEM_WIDTH,)
    shift_specs = _shift_specs(caches_t, n_p)
    return pl.pallas_call(
        functools.partial(_in_a_kernel, n_p=n_p, n_shift=len(caches_t), dec_seq=dec_seq),
        grid=(n_p + 1,),
        in_specs=[pl.BlockSpec((tm, d), row), _full_spec(xs), _const_spec((1,) + w_bf.shape[1:], (0, 0, 0)),
                  pl.BlockSpec((tm, LANES), tab), pl.BlockSpec((tm, LANES), tab),
                  _full_spec(tab_s[0]), _full_spec(tab_s[1]),
                  pl.BlockSpec((1, 1) + memkv_t.shape[2:], lambda i: (0, cl(i) // nt, 0, 0))] + shift_specs,
        out_specs=[pl.BlockSpec((tm, w), row) for w in widths]
                  + [pl.BlockSpec((ts, w), lambda i: (0, 0)) for w in widths] + shift_specs,
        out_shape=[jax.ShapeDtypeStruct((tp, w), BF16) for w in widths]
                  + [jax.ShapeDtypeStruct((ts, w), F32) for w in widths]
                  + [jax.ShapeDtypeStruct(c.shape, F32) for c in caches_t],
        compiler_params=_params(("arbitrary",)),
        name="in_proj_a",
    )(xp, xs, w_bf, *tab_p, *tab_s, memkv_t, *caches_t)


def _retention_kernel(q_ref, k_ref, v_ref, g_ref, mix_ref, st_ref, dec_ref, rdec_ref, kdec_ref,
                      inner_ref, kv_ref):
    c = pl.program_id(1)
    cs = RET_CHUNK

    @pl.when(c == 0)
    def _():
        st_ref[...] = jnp.zeros_like(st_ref)
        row = lax.broadcasted_iota(jnp.int32, (cs, cs), 0).astype(F32)
        col = lax.broadcasted_iota(jnp.int32, (cs, cs), 1).astype(F32)
        diff = row - col
        for h in range(RET_HEADS):
            lg = LOG_G[h]
            dec_ref[h] = jnp.where(diff >= 0, jnp.exp(jnp.maximum(diff, 0.0) * lg), 0.0)
            rdec_ref[h] = jnp.exp((row + 1.0) * lg)
            kdec_ref[h] = jnp.exp((cs - 1.0 - row) * lg)

    nj = q_ref.shape[0] // cs
    for j in range(nj):
        rs = slice(j * cs, (j + 1) * cs)
        for h in range(RET_HEADS):
            hs = slice(h * RET_HEAD_DIM, (h + 1) * RET_HEAD_DIM)
            kh = k_ref[rs, hs]
            vh = v_ref[rs, hs]
            s = _dot_nt(q_ref[rs, hs], kh) * dec_ref[h]
            inner_ref[j, h] = _dot(s.astype(BF16), vh)
            kd = (kh.astype(F32) * kdec_ref[h]).astype(BF16)
            kv_ref[j, h] = _dot_tn(kd, vh)
    for j in range(nj):
        rs = slice(j * cs, (j + 1) * cs)
        for h in range(RET_HEADS):
            hs = slice(h * RET_HEAD_DIM, (h + 1) * RET_HEAD_DIM)
            st = st_ref[0, h]
            cross = _dot(q_ref[rs, hs], st.astype(BF16)) * rdec_ref[h]
            st_ref[0, h] = math.exp(cs * LOG_G[h]) * st + kv_ref[j, h]
            o = inner_ref[j, h] + cross
            mu = jnp.mean(o, axis=-1, keepdims=True)
            oc = o - mu
            var = jnp.mean(oc * oc, axis=-1, keepdims=True)
            on = oc * lax.rsqrt(var + LN_EPS)
            mix_ref[rs, hs] = (g_ref[rs, hs].astype(F32) * on).astype(mix_ref.dtype)


def _retention(q, k, v, g, batch, seq):
    t = q.shape[0]
    rows = RET_CHUNK * RET_CHUNKS_PER_STEP
    ns = seq // rows
    tok = lambda b, c: (b * ns + c, 0)
    sq = (RET_HEADS, RET_CHUNK, RET_CHUNK)
    return pl.pallas_call(
        _retention_kernel,
        grid=(batch, ns),
        in_specs=[pl.BlockSpec((rows, MAIN_WIDTH), tok)] * 4,
        out_specs=[pl.BlockSpec((rows, MAIN_WIDTH), tok),
                   pl.BlockSpec((1, RET_HEADS, RET_HEAD_DIM, RET_HEAD_DIM), lambda b, c: (b, 0, 0, 0))],
        out_shape=[jax.ShapeDtypeStruct((t, MAIN_WIDTH), BF16),
                   jax.ShapeDtypeStruct((batch, RET_HEADS, RET_HEAD_DIM, RET_HEAD_DIM), F32)],
        scratch_shapes=[pltpu.VMEM(sq, F32)] * 3 + [pltpu.VMEM((RET_CHUNKS_PER_STEP,) + sq, F32)] * 2,
        compiler_params=_params(("arbitrary", "arbitrary")),
        name="retention",
    )(q, k, v, g)


def _retention_sample_kernel(q_ref, k_ref, v_ref, g_ref, qm_ref, mkv_ref, st_ref, mix_ref, nst_ref, *, t):
    nb = q_ref.shape[0] // t
    pad_k = jnp.zeros((LANES - t, RET_HEAD_DIM), F32)
    pad_q = jnp.zeros((BF16_SUBLANES - t, RET_HEAD_DIM), F32)
    lhs = lambda x: jnp.concatenate([x, pad_q], axis=0).astype(BF16)
    row = lax.broadcasted_iota(jnp.int32, (t, LANES), 0).astype(F32)
    col = lax.broadcasted_iota(jnp.int32, (t, LANES), 1).astype(F32)
    prow = lax.broadcasted_iota(jnp.int32, (LANES, RET_HEAD_DIM), 0).astype(F32)
    diff = row - col
    for h in range(RET_HEADS):
        lg = LOG_G[h]
        hs = slice(h * RET_HEAD_DIM, (h + 1) * RET_HEAD_DIM)
        dec = jnp.where(diff >= 0, jnp.exp(jnp.maximum(diff, 0.0) * lg), 0.0)
        rdec = jnp.exp((row + 1.0) * lg)
        kdec = jnp.exp((t - 1.0 - prow) * lg)
        for bb in range(nb):
            rs = slice(bb * t, (bb + 1) * t)
            qh = lhs(q_ref[rs, hs])
            kp = jnp.concatenate([k_ref[rs, hs], pad_k], axis=0)
            vp = jnp.concatenate([v_ref[rs, hs], pad_k], axis=0).astype(BF16)
            st = st_ref[0, bb, h]
            inner = _dot(lhs(_dot_nt(qh, kp.astype(BF16))[:t] * dec), vp)[:t]
            cross = _dot(qh, st.astype(BF16))[:t] * rdec
            nst_ref[bb, h] = math.exp(t * lg) * st + _dot_tn((kp * kdec).astype(BF16), vp)
            o = inner + cross
            mu = jnp.mean(o, axis=-1, keepdims=True)
            oc = o - mu
            var = jnp.mean(oc * oc, axis=-1, keepdims=True)
            on = oc * lax.rsqrt(var + LN_EPS)
            mix_ref[rs, hs] = g_ref[rs, hs] * on
    for bb in range(nb):
        rs = slice(bb * t, (bb + 1) * t)
        mix_ref[rs, MAIN_WIDTH:] = _mem_attn_tile(qm_ref[rs, :], mkv_ref[0, bb])


def _retention_sample(q, k, v, g, qm, memkv_t, state, batch, t):
    nb = SAMPLE_BATCH_PER_STEP
    tok = lambda b: (b, 0)
    hd = (RET_HEADS, RET_HEAD_DIM, RET_HEAD_DIM)
    return pl.pallas_call(
        functools.partial(_retention_sample_kernel, t=t),
        grid=(batch // nb,),
        in_specs=[pl.BlockSpec((nb * t, MAIN_WIDTH), tok)] * 4
                 + [pl.BlockSpec((nb * t, MEM_WIDTH), tok),
                    pl.BlockSpec((1, nb) + memkv_t.shape[2:], lambda b: (0, b, 0, 0)),
                    pl.BlockSpec((1, nb) + hd, lambda b: (0, b, 0, 0, 0))],
        out_specs=[pl.BlockSpec((nb * t, D_MODEL), tok), pl.BlockSpec((nb,) + hd, lambda b: (b, 0, 0, 0))],
        out_shape=[jax.ShapeDtypeStruct((batch * t, D_MODEL), F32),
                   jax.ShapeDtypeStruct((batch,) + hd, F32)],
        compiler_params=_params(("parallel",)),
        name="retention_sample",
    )(q, k, v, g, qm, memkv_t, state)


def _dilated_sample_group(q, cache, new, window, dil):
    t = q.shape[0]
    n_buf = cache.shape[1]
    gw = GROUP_WIDTH
    q4 = _stack_heads(q, DIL_HEAD_DIM)
    rows = q4.shape[0]
    newp = jnp.concatenate([new, jnp.zeros((LANES - t, new.shape[1]), F32)], axis=0).astype(BF16)

    def masked(s, first_index):
        key = lax.broadcasted_iota(jnp.int32, s.shape, 1) + first_index
        tok = lax.broadcasted_iota(jnp.int32, s.shape, 0) & (t - 1)
        delta = n_buf + tok - key
        valid = (delta >= 0) & (delta <= window) & ((delta & (dil - 1)) == 0)
        return jnp.where(valid, s, NEG_BIG)

    s_c = masked(_dot(q4, cache[:gw].astype(BF16)), 0)
    s_n = masked(_dot_nt(q4, newp[:, :gw]), n_buf)
    m = jnp.maximum(jnp.max(s_c, axis=-1, keepdims=True), jnp.max(s_n, axis=-1, keepdims=True))
    e_c = jnp.exp(s_c - m)
    e_n = jnp.exp(s_n - m)
    l = jnp.sum(e_c, axis=-1, keepdims=True) + jnp.sum(e_n, axis=-1, keepdims=True)
    inv = 1.0 / l
    full = (_dot_nt((e_c * inv).astype(BF16), cache[gw:].astype(BF16))
            + _dot((e_n * inv).astype(BF16), newp[:, gw:]))
    o = _pick_heads(full, t, DIL_HEAD_DIM)
    lse = _pick_heads(jnp.broadcast_to(m + jnp.log(l), (rows, gw)), t, DIL_HEAD_DIM)
    return o, lse


def _sample_mixer_kernel(*refs, t):
    q_refs, cache_refs, new_refs, new_t_refs = refs[0:3], refs[3:6], refs[6:9], refs[9:12]
    qm_ref, mkv_ref = refs[12], refs[13]
    shifted_refs, mix_ref, win_refs = refs[14:17], refs[17], refs[18:21]
    gw = GROUP_WIDTH
    for bb in range(mix_ref.shape[0] // t):
        rs = slice(bb * t, (bb + 1) * t)
        outs, lses = [], []
        for g, (window, dil) in enumerate(DIL_PAIRS):
            o, lse = _dilated_sample_group(q_refs[g][rs, :], cache_refs[g][bb], new_refs[g][rs, :], window, dil)
            outs.append(o)
            lses.append(lse)
            win_refs[g][bb] = shifted_refs[g][bb]
            win_refs[g][bb, :, LANES - t:] = new_t_refs[g][bb]
        m = jnp.maximum(jnp.maximum(lses[0], lses[1]), lses[2])
        es = [jnp.exp(v - m) for v in lses]
        inv = 1.0 / (es[0] + es[1] + es[2])
        for g in range(3):
            mix_ref[rs, g * gw:(g + 1) * gw] = outs[g] * (es[g] * inv)
        mix_ref[rs, MAIN_WIDTH:] = _mem_attn_tile(qm_ref[rs, :], mkv_ref[0, bb])


def _sample_mixer(q_g, caches_t, new_kv, new_kv_t, qm, memkv_t, shifted, batch, t):
    gw = GROUP_WIDTH
    nb = SAMPLE_BATCH_PER_STEP
    tok = lambda b: (b, 0)
    big = lambda b: (b, 0, 0)
    cache_specs = [pl.BlockSpec((nb,) + c.shape[1:], big) for c in caches_t]
    tail_specs = [pl.BlockSpec((nb, c.shape[1], LANES), lambda b, j=c.shape[2] // LANES - 1: (b, 0, j))
                  for c in shifted]
    first_shifted = 3 + len(caches_t) + 3 + 3 + 2
    return pl.pallas_call(
        functools.partial(_sample_mixer_kernel, t=t),
        grid=(batch // nb,),
        in_specs=[pl.BlockSpec((nb * t, gw), tok)] * 3 + cache_specs + [pl.BlockSpec((nb * t, 2 * gw), tok)] * 3
                 + [pl.BlockSpec((nb, 2 * gw, t), big)] * 3
                 + [pl.BlockSpec((nb * t, MEM_WIDTH), tok),
                    pl.BlockSpec((1, nb) + memkv_t.shape[2:], lambda b: (1, b, 0, 0))] + tail_specs,
        out_specs=[pl.BlockSpec((nb * t, D_MODEL), tok)] + tail_specs,
        out_shape=[jax.ShapeDtypeStruct((batch * t, D_MODEL), F32)]
                  + [jax.ShapeDtypeStruct(c.shape, F32) for c in shifted],
        input_output_aliases={first_shifted + g: 1 + g for g in range(len(shifted))},
        compiler_params=_params(("parallel",)),
        name="sample_mixer",
    )(*q_g, *caches_t, *new_kv, *new_kv_t, qm, memkv_t, *shifted)


def _natural_rows(ref, dil, scr):
    if dil == 1:
        return ref[0, 0].astype(F32)
    n = ref.shape[2]
    for r in range(dil):
        v = ref[0, r].astype(F32)
        for c in range(v.shape[1] // LANES):
            scr[c, pl.ds(r, n, stride=dil), :] = v[:, c * LANES:(c + 1) * LANES]
    return jnp.concatenate([scr[c] for c in range(scr.shape[0])], axis=1)


def _post_body(mix, x_ref, wo_ref, g1_ref, b1_ref, wi_ref, w2_ref, g2_ref, b2_ref, out_ref, act_ref):
    rows = x_ref.shape[0]
    x1 = _layer_norm(ALPHA * x_ref[...] + _dot(mix, wo_ref[0]), g1_ref[0], b1_ref[0])
    x1b = x1.astype(BF16)
    for c in range(0, FFN_HIDDEN, FFN_COL_CHUNK):
        gate = _dot(x1b, wi_ref[0, :, c:c + FFN_COL_CHUNK])
        up = _dot(x1b, wi_ref[0, :, FFN_HIDDEN + c:FFN_HIDDEN + c + FFN_COL_CHUNK])
        act_ref[0:rows, c:c + FFN_COL_CHUNK] = (_silu(gate) * up).astype(BF16)
    y = _dot(act_ref[0:rows, :], w2_ref[0])
    out_ref[...] = _layer_norm(ALPHA * x1 + y, g2_ref[0], b2_ref[0])


def _post_kernel(*refs, dils, n_mix_p, n_mix_s, n_p, n_shift, dec_seq):
    mixp_refs, refs = refs[:n_mix_p], refs[n_mix_p:]
    xp_ref, refs = refs[0], refs[1:]
    mixs_refs, refs = refs[:n_mix_s], refs[n_mix_s:]
    xs_ref, refs = refs[0], refs[1:]
    weights, refs = refs[:7], refs[7:]
    caches, refs = refs[:n_shift], refs[n_shift:]
    outp_ref, outs_ref, refs = refs[0], refs[1], refs[2:]
    wins, refs = refs[:n_shift], refs[n_shift:]
    act_ref = refs[0]
    scrs = list(refs[1:])
    i = pl.program_id(0)

    @pl.when(i < n_p)
    def _():
        if dils is not None:
            o_refs, l_refs, mem_ref = mixp_refs[0:3], mixp_refs[3:6], mixp_refs[6]
            pool = list(scrs)
            nat = lambda r, d: _natural_rows(r, d, pool.pop(0) if d > 1 else None)
            os_ = [nat(o_refs[g], dils[g]) for g in range(3)]
            ls = [nat(l_refs[g], dils[g]) for g in range(3)]
            m = jnp.maximum(jnp.maximum(ls[0], ls[1]), ls[2])
            es = [jnp.exp(v - m) for v in ls]
            inv = 1.0 / (es[0] + es[1] + es[2])
            parts = [(os_[g] * (es[g] * inv)).astype(BF16) for g in range(3)]
            parts.append(mem_ref[...].astype(BF16))
        else:
            parts = [r[...].astype(BF16) for r in mixp_refs]
        mix = parts[0] if len(parts) == 1 else jnp.concatenate(parts, axis=1)
        _post_body(mix, xp_ref, *weights, outp_ref, act_ref)
        for c_ref, w_ref in zip(caches, wins):
            _shift_window(c_ref, w_ref, dec_seq)

    @pl.when(i == n_p)
    def _():
        parts = [r[...].astype(BF16) for r in mixs_refs]
        mix = parts[0] if len(parts) == 1 else jnp.concatenate(parts, axis=1)
        _post_body(mix, xs_ref, *weights, outs_ref, act_ref)


def _post(layer, mix_p, xp, mix_s, xs, wo, g1, b1, wi, w2, g2, b2, dils, seq, caches_t=(), dec_seq=0):
    tp, d = xp.shape
    tm = TOKEN_TILE
    n_p = tp // tm
    nt = seq // tm
    cl = lambda i: jnp.minimum(i, n_p - 1)
    row = lambda i: (cl(i), 0)
    if dils is None:
        mixp_specs = [pl.BlockSpec((tm, a.shape[1]), row) for a in mix_p]
        scratch = []
    else:
        res = lambda i: (cl(i) // nt, 0, cl(i) % nt, 0)
        mixp_specs = [pl.BlockSpec((1, dl, tm // dl, GROUP_WIDTH), res) for dl in dils] * 2
        mixp_specs.append(pl.BlockSpec((tm, MEM_WIDTH), row))
        scratch = [pltpu.VMEM((GROUP_WIDTH // LANES, tm, LANES), F32) for dl in dils * 2 if dl > 1]
    lsel = (layer, 0, 0)
    vec = _const_spec((1, 1, d), lsel)
    shift_specs = _shift_specs(caches_t, n_p)
    return pl.pallas_call(
        functools.partial(_post_kernel, dils=dils, n_mix_p=len(mix_p), n_mix_s=len(mix_s), n_p=n_p,
                          n_shift=len(caches_t), dec_seq=dec_seq),
        grid=(n_p + 1,),
        in_specs=mixp_specs + [pl.BlockSpec((tm, d), row)] + [_full_spec(a) for a in mix_s] + [_full_spec(xs)]
                 + [_const_spec((1,) + wo.shape[1:], lsel), vec, vec,
                    _const_spec((1,) + wi.shape[1:], lsel), _const_spec((1,) + w2.shape[1:], lsel), vec, vec]
                 + shift_specs,
        out_specs=[pl.BlockSpec((tm, d), row), pl.BlockSpec(xs.shape, lambda i: (0, 0))] + shift_specs,
        out_shape=[jax.ShapeDtypeStruct((tp, d), F32), jax.ShapeDtypeStruct(xs.shape, F32)]
                  + [jax.ShapeDtypeStruct(c.shape, F32) for c in caches_t],
        scratch_shapes=[pltpu.VMEM((tm, FFN_HIDDEN), BF16)] + scratch,
        compiler_params=_params(("arbitrary",)),
        name="post" if dils is None else "post_combine",
    )(*mix_p, xp, *mix_s, xs, wo, g1, b1, wi, w2, g2, b2, *caches_t)


def _residue_major(ref, y, dil, scr):
    if dil == 1:
        ref[...] = y.astype(ref.dtype).reshape(ref.shape)
        return
    n = ref.shape[2]
    ncol = y.shape[1] // LANES
    for c in range(ncol):
        scr[c] = y[:, c * LANES:(c + 1) * LANES]
    for r in range(dil):
        rows = [scr[c, pl.ds(r, n, stride=dil), :] for c in range(ncol)]
        ref[0, r] = jnp.concatenate(rows, axis=1).astype(ref.dtype)


def _in_b_body(x_ref, wkv_ref, wq_ref, wkvt_ref, cos_ref, sin_ref, cos_t_ref, sin_t_ref, mkv_ref,
               q_refs, kv_refs, m_ref, win_refs, scrs, dils, win_preds):
    xb = x_ref[...].astype(BF16)
    cos = cos_ref[...]
    sin = sin_ref[...]
    gw = GROUP_WIDTH
    mw = MAIN_WIDTH
    half = DIL_HEAD_DIM // 2
    tm = x_ref.shape[0]
    pool = list(scrs)
    for g in range(3):
        ks = slice(g * gw, (g + 1) * gw)
        vs = slice(mw + g * gw, mw + (g + 1) * gw)
        k = _rope_cols(_dot(xb, wkv_ref[:, ks]), cos, sin, half)
        v = _dot(xb, wkv_ref[:, vs])
        _residue_major(kv_refs[g], jnp.concatenate([k, v], axis=1), dils[g], pool.pop(0) if dils[g] > 1 else None)
        q = _rope_cols(_dot(xb, wq_ref[0, :, ks]), cos, sin, half) * ATTN_SCALE
        _residue_major(q_refs[g], q, dils[g], pool.pop(0) if dils[g] > 1 else None)

        def window(g=g, ks=ks, vs=vs):
            k_t = _rope_rows(_dot_nt(wkvt_ref[ks, :], xb), cos_t_ref[...], sin_t_ref[...], DIL_HEAD_DIM)
            v_t = _dot_nt(wkvt_ref[vs, :], xb)
            full = jnp.concatenate([k_t, v_t], axis=0)
            win_refs[g][0] = full[:, tm - win_refs[g].shape[2]:]

        if win_preds[g] is None:
            window()
        else:
            pl.when(win_preds[g])(window)
    qm = _dot(xb, wq_ref[0, :, mw:mw + MEM_WIDTH]) * ATTN_SCALE
    if mkv_ref is not None:
        m_ref[...] = _mem_attn_tile(qm.astype(BF16), mkv_ref[0, 0]).astype(m_ref.dtype)
    else:
        m_ref[...] = qm.astype(m_ref.dtype)


def _in_b_kernel(xp_ref, xs_ref, wkv_ref, wq_ref, wkvt_ref, cosp_ref, sinp_ref, cospt_ref, sinpt_ref,
                 coss_ref, sins_ref, cosst_ref, sinst_ref, mkv_ref, *refs, dils, win_tiles, nt, n_p):
    outp, outs, scrs = refs[0:10], refs[10:20], refs[20:]
    i = pl.program_id(0)

    @pl.when(i < n_p)
    def _():
        j = i % nt
        _in_b_body(xp_ref, wkv_ref, wq_ref, wkvt_ref, cosp_ref, sinp_ref, cospt_ref, sinpt_ref, mkv_ref,
                   outp[0:3], outp[3:6], outp[6], outp[7:10], scrs, dils,
                   [j >= nt - w for w in win_tiles])

    @pl.when(i == n_p)
    def _():
        _in_b_body(xs_ref, wkv_ref, wq_ref, wkvt_ref, coss_ref, sins_ref, cosst_ref, sinst_ref, None,
                   outs[0:3], outs[3:6], outs[6], outs[7:10], (), (1, 1, 1), [None] * 3)


def _in_b(xp, xs, w_kv, w_q, w_kvt, tabs_p, tabs_pt, tabs_s, tabs_st, memkv_t, batch, seq, dils, windows):
    tp, d = xp.shape
    ts = xs.shape[0]
    gw = GROUP_WIDTH
    tm = TOKEN_TILE
    n_p = tp // tm
    nt = seq // tm
    cl = lambda i: jnp.minimum(i, n_p - 1)
    row = lambda i: (cl(i), 0)
    tab = lambda i: (cl(i) % nt, 0)
    tab_t = lambda i: (0, cl(i) % nt)
    res = lambda i: (cl(i) // nt, 0, cl(i) % nt, 0)
    fix2 = lambda i: (0, 0)
    out_specs = ([pl.BlockSpec((1, dl, tm // dl, gw), res) for dl in dils]
                 + [pl.BlockSpec((1, dl, tm // dl, 2 * gw), res) for dl in dils]
                 + [pl.BlockSpec((tm, MEM_WIDTH), row)])
    out_shape = ([jax.ShapeDtypeStruct((batch, dl, seq // dl, gw), BF16) for dl in dils]
                 + [jax.ShapeDtypeStruct((batch, dl, seq // dl, 2 * gw), BF16) for dl in dils]
                 + [jax.ShapeDtypeStruct((tp, MEM_WIDTH), BF16)])
    win_tiles = []
    for w in windows:
        rb = min(w, tm)
        nblk = w // rb
        win_tiles.append(nblk)
        out_specs.append(pl.BlockSpec(
            (1, 2 * gw, rb), lambda i, nblk=nblk: (cl(i) // nt, 0, jnp.maximum(cl(i) % nt - (nt - nblk), 0))))
        out_shape.append(jax.ShapeDtypeStruct((batch, 2 * gw, w), F32))
    out_specs += ([pl.BlockSpec((ts, gw), fix2)] * 3 + [pl.BlockSpec((ts, 2 * gw), fix2)] * 3
                  + [pl.BlockSpec((ts, MEM_WIDTH), fix2)] + [pl.BlockSpec((1, 2 * gw, ts), lambda i: (0, 0, 0))] * 3)
    out_shape += ([jax.ShapeDtypeStruct((ts, gw), F32)] * 3 + [jax.ShapeDtypeStruct((ts, 2 * gw), F32)] * 3
                  + [jax.ShapeDtypeStruct((ts, MEM_WIDTH), F32)] + [jax.ShapeDtypeStruct((1, 2 * gw, ts), F32)] * 3)
    scratch = []
    for dl in dils:
        if dl > 1:
            scratch += [pltpu.VMEM((2 * gw // LANES, tm, LANES), F32), pltpu.VMEM((gw // LANES, tm, LANES), F32)]
    return pl.pallas_call(
        functools.partial(_in_b_kernel, dils=tuple(dils), win_tiles=tuple(win_tiles), nt=nt, n_p=n_p),
        grid=(n_p + 1,),
        in_specs=[pl.BlockSpec((tm, d), row), _full_spec(xs), _const_spec(w_kv.shape, (0, 0)),
                  _const_spec((1,) + w_q.shape[1:], (0, 0, 0)), _const_spec(w_kvt.shape, (0, 0)),
                  pl.BlockSpec((tm, LANES), tab), pl.BlockSpec((tm, LANES), tab),
                  pl.BlockSpec((DIL_HEAD_DIM, tm), tab_t), pl.BlockSpec((DIL_HEAD_DIM, tm), tab_t),
                  _full_spec(tabs_s[0]), _full_spec(tabs_s[1]), _full_spec(tabs_st[0]), _full_spec(tabs_st[1]),
                  pl.BlockSpec((1, 1) + memkv_t.shape[2:], lambda i: (1, cl(i) // nt, 0, 0))],
        out_specs=out_specs,
        out_shape=out_shape,
        scratch_shapes=scratch,
        compiler_params=_params(("arbitrary",)),
        name="in_proj_b",
    )(xp, xs, w_kv, w_q, w_kvt, *tabs_p, *tabs_pt, *tabs_s, *tabs_st, memkv_t)


def _dilated_kernel(q_ref, kv_ref, o_ref, lse_ref, ring_ref):
    i = pl.program_id(2)
    blk = DIL_BLOCK
    gw = GROUP_WIDTH
    nsub = q_ref.shape[2] // blk
    slot = i & 1

    @pl.when(i == 0)
    def _():
        ring_ref[1] = jnp.zeros(ring_ref.shape[1:], ring_ref.dtype)

    cur = kv_ref[0, 0]
    ring_ref[slot] = cur[(nsub - 1) * blk:]
    kext = jnp.concatenate([ring_ref[1 - slot], cur], axis=0)
    rows = GROUP_HEADS * blk
    qi = lax.broadcasted_iota(jnp.int32, (rows, 2 * blk), 0) & (blk - 1)
    kj = lax.broadcasted_iota(jnp.int32, (rows, 2 * blk), 1)
    delta = qi + blk - kj
    band = (delta >= 0) & (delta <= blk)
    for j in range(nsub):
        rs = slice(j * blk, (j + 1) * blk)
        q4 = _stack_heads(q_ref[0, 0, rs, :], DIL_HEAD_DIM)
        kv = kext[j * blk:(j + 2) * blk]
        valid = band if j > 0 else band & ((i > 0) | (kj >= blk))
        s = jnp.where(valid, _dot_nt(q4, kv[:, :gw]), NEG_BIG)
        m = jnp.max(s, axis=-1, keepdims=True)
        e = jnp.exp(s - m)
        l = jnp.sum(e, axis=-1, keepdims=True)
        p = (e * (1.0 / l)).astype(BF16)
        full = _dot(p, kv[:, gw:])
        o_ref[0, 0, rs, :] = _pick_heads(full, blk, DIL_HEAD_DIM).astype(o_ref.dtype)
        lse_ref[0, 0, rs, :] = _pick_heads(jnp.broadcast_to(m + jnp.log(l), (rows, gw)), blk, DIL_HEAD_DIM)


def _dilated(q, kv, dil):
    batch, _, m, gw = q.shape
    rows = DIL_BLOCK * min(DIL_BLOCKS_PER_STEP, m // DIL_BLOCK)
    idx = lambda b, r, i: (b, r, i, 0)
    return pl.pallas_call(
        _dilated_kernel,
        grid=(batch, dil, m // rows),
        in_specs=[pl.BlockSpec((1, 1, rows, gw), idx), pl.BlockSpec((1, 1, rows, 2 * gw), idx)],
        out_specs=[pl.BlockSpec((1, 1, rows, gw), idx)] * 2,
        out_shape=[jax.ShapeDtypeStruct(q.shape, BF16), jax.ShapeDtypeStruct(q.shape, F32)],
        scratch_shapes=[pltpu.VMEM((2, DIL_BLOCK, 2 * gw), BF16)],
        compiler_params=_params(("arbitrary", "arbitrary", "arbitrary")),
        name=f"dilated_d{dil}",
    )(q, kv)


def _rope_tables(base, n):
    split = min(ROPE_SPLIT, n)
    hi = base + split * jnp.arange(n // split, dtype=F32)
    lo = jnp.arange(split, dtype=F32)
    inv = ROPE_THETA ** (-jnp.arange(0, RET_HEAD_DIM, 2, dtype=F32) / RET_HEAD_DIM)
    inv_h = inv[0::2]
    neg = lambda m: jnp.concatenate([-jnp.ones((m,), F32), jnp.ones((m,), F32)])

    def token_major(freq, sign):
        xa, xb = hi[:, None] * freq[None, :], lo[:, None] * freq[None, :]
        ca, sa, cb, sb = jnp.cos(xa), jnp.sin(xa), jnp.cos(xb), jnp.sin(xb)
        cos = ca[:, None, :] * cb[None, :, :] - sa[:, None, :] * sb[None, :, :]
        sin = (sa[:, None, :] * cb[None, :, :] + ca[:, None, :] * sb[None, :, :]) * sign
        return cos.reshape(n, -1), sin.reshape(n, -1)

    def feature_major(freq, sign):
        xa, xb = freq[:, None] * hi[None, :], freq[:, None] * lo[None, :]
        ca, sa, cb, sb = jnp.cos(xa), jnp.sin(xa), jnp.cos(xb), jnp.sin(xb)
        cos = ca[:, :, None] * cb[:, None, :] - sa[:, :, None] * sb[:, None, :]
        sin = (sa[:, :, None] * cb[:, None, :] + ca[:, :, None] * sb[:, None, :]) * sign[:, None, None]
        return cos.reshape(-1, n), sin.reshape(-1, n)

    f_big = jnp.concatenate([inv, inv])
    f_half = jnp.concatenate([inv_h, inv_h])
    return (token_major(f_big, neg(RET_HEAD_DIM // 2)),
            token_major(jnp.tile(f_half, 2), jnp.tile(neg(DIL_HEAD_DIM // 2), 2)),
            feature_major(f_half, neg(DIL_HEAD_DIM // 2)))


def _to_feature_major(x5):
    b, w = x5.shape[0], x5.shape[1]
    return jnp.transpose(x5, (0, 2, 3, 4, 1)).reshape(b, 2 * GROUP_WIDTH, w)


def _from_feature_major(xt):
    b, _, w = xt.shape
    return jnp.transpose(xt.reshape(b, 2, GROUP_HEADS, DIL_HEAD_DIM, w), (0, 4, 1, 2, 3))


def kernel(x_prompt, x_sample, mem_prompt, cache_mem_kv, state_ret, cache_win_kv_g1, cache_win_kv_g2, cache_win_kv_g3, w_in_a, w_in_b, w_out, w_kv_shared, w_mem_kv, ln_mix_g, ln_mix_b, ln_ffn_g, ln_ffn_b, w_ffn_in, w_ffn_out):
    batch, seq, d = x_prompt.shape
    dec_batch, dec_seq, _ = x_sample.shape
    n_mem = mem_prompt.shape[1]
    gw = GROUP_WIDTH
    ts = dec_batch * dec_seq
    win_caches = (cache_win_kv_g1, cache_win_kv_g2, cache_win_kv_g3)
    dils = tuple(dl for _, dl in DIL_PAIRS)

    tab_a_p, tab_b_p, tab_bt_p = _rope_tables(0.0, seq)
    tab_a_s, tab_b_s, tab_bt_s = _rope_tables(float(PAST_LEN), dec_seq)
    tab_a_s = tuple(jnp.tile(a, (dec_batch, 1)) for a in tab_a_s)
    tab_b_s = tuple(jnp.tile(a, (dec_batch, 1)) for a in tab_b_s)
    tab_bt_s = tuple(jnp.tile(a, (1, dec_batch)) for a in tab_bt_s)

    w_a = w_in_a.astype(BF16)
    w_kv = w_kv_shared.astype(BF16)
    w_kvt = w_kv_shared.T.astype(BF16)
    w_q = w_in_b.astype(BF16)
    w_o = w_out.astype(BF16)
    w_fi = w_ffn_in.astype(BF16)
    w_fo = w_ffn_out.astype(BF16)
    w_mem_t = jnp.transpose(w_mem_kv, (0, 2, 1)).astype(BF16)
    lnv = lambda a: a.reshape(DEPTH, 1, d)
    ln = (lnv(ln_mix_g), lnv(ln_mix_b), lnv(ln_ffn_g), lnv(ln_ffn_b))

    memkv_p = _mem_proj(mem_prompt, w_mem_t)
    memkv_s = jnp.transpose(cache_mem_kv, (0, 1, 3, 4, 5, 2)).reshape(DEPTH, dec_batch, 2 * MEM_WIDTH, n_mem)

    xp = x_prompt.reshape(batch * seq, d)
    xs = x_sample.reshape(ts, d)

    def post(l, mix_p, mix_s, dls, shift=()):
        return _post(l, mix_p, xp, mix_s, xs, w_o, ln[0], ln[1], w_fi, w_fo, ln[2], ln[3], dls, seq,
                     shift, dec_seq)

    caches_t = [_to_feature_major(c) for c in win_caches]

    outs = _in_a(xp, xs, w_a, tab_a_p, tab_a_s, memkv_p, seq, caches_t[:2], dec_seq)
    q, k, v, g, mem_o = outs[:5]
    mix_p, state_p = _retention(q, k, v, g, batch, seq)
    q, k, v, g, qm = outs[5:10]
    shifted = list(outs[10:])
    mix_s, state_s = _retention_sample(q, k, v, g, qm, memkv_s, state_ret, dec_batch, dec_seq)
    xp, xs, *shifted_big = post(0, [mix_p, mem_o], [mix_s], None, caches_t[2:])
    shifted += shifted_big

    windows_p = tuple(min(w, seq) for w, _ in DIL_PAIRS)
    outs = _in_b(xp, xs, w_kv, w_q, w_kvt, tab_b_p, tab_bt_p, tab_b_s, tab_bt_s, memkv_p,
                 batch, seq, dils, windows_p)
    q_g, kv_g, mem_o, win_p = outs[0:3], outs[3:6], outs[6], outs[7:10]
    att = [_dilated(q_g[i], kv_g[i], dils[i]) for i in range(3)]

    q_s, kv_new, qm, new_t = outs[10:13], outs[13:16], outs[16], outs[17:20]
    new_t = [jnp.transpose(a.reshape(2 * gw, dec_batch, dec_seq), (1, 0, 2)) for a in new_t]
    mix_s, *win_s = _sample_mixer(q_s, caches_t, kv_new, new_t, qm, memkv_s, shifted, dec_batch, dec_seq)

    xp, xs = post(1, [a[0] for a in att] + [a[1] for a in att] + [mem_o], [mix_s], dils)

    memkv_out = jnp.transpose(memkv_p.reshape(DEPTH, batch, 2, MEM_HEADS, MEM_HEAD_DIM, n_mem), (0, 1, 5, 2, 3, 4))
    return (xp.reshape(batch, seq, d), xs.reshape(dec_batch, dec_seq, d),
            state_p[None], state_s[None], memkv_out,
            _from_feature_major(win_p[0]), _from_feature_major(win_p[1]), _from_feature_major(win_p[2]),
            _from_feature_major(win_s[0]), _from_feature_major(win_s[1]), _from_feature_major(win_s[2]))
```

```python
import functools
import math

import jax
import jax.numpy as jnp
from jax import lax
from jax.experimental import pallas as pl
from jax.experimental.pallas import tpu as pltpu

F32 = jnp.float32
BF16 = jnp.bfloat16

D_MODEL = 1024
MEM_HEADS = 4
MEM_HEAD_DIM = 64
MEM_WIDTH = MEM_HEADS * MEM_HEAD_DIM
MAIN_WIDTH = D_MODEL - MEM_WIDTH
RET_HEADS = 6
RET_HEAD_DIM = MAIN_WIDTH // RET_HEADS
RET_CHUNK = 128
RET_CHUNKS_PER_STEP = 8
DIL_PAIRS = ((128, 1), (512, 4), (2048, 16))
GROUP_HEADS = 4
DIL_HEAD_DIM = 64
GROUP_WIDTH = GROUP_HEADS * DIL_HEAD_DIM
DIL_BLOCK = 128
DIL_BLOCKS_PER_STEP = 8
FFN_HIDDEN = 2816
ROPE_THETA = 10000.0
ROPE_SPLIT = 64
LN_EPS = 1e-5
DEPTH = 2
ALPHA = (2 * DEPTH) ** 0.25
PAST_LEN = 8192
NEG_BIG = -1e30
ATTN_SCALE = DIL_HEAD_DIM ** -0.5

LANES = 128
BF16_SUBLANES = 16
SAMPLE_BATCH_PER_STEP = 2
VMEM_LIMIT = 56 * 1024 * 1024
FFN_COL_CHUNK = 256
TOKEN_TILE = 512

LOG_G = tuple(math.log1p(-(2.0 ** (-5.0 - h))) for h in range(RET_HEADS))


def _dot(a, b):
    return jnp.dot(a, b, preferred_element_type=F32)


def _dot_nt(a, b):
    return lax.dot_general(a, b, (((1,), (1,)), ((), ())), preferred_element_type=F32)


def _dot_tn(a, b):
    return lax.dot_general(a, b, (((0,), (0,)), ((), ())), preferred_element_type=F32)


def _silu(x):
    return x / (1.0 + jnp.exp(-x))


def _layer_norm(z, g, b):
    mu = jnp.mean(z, axis=-1, keepdims=True)
    zc = z - mu
    var = jnp.mean(zc * zc, axis=-1, keepdims=True)
    return zc * lax.rsqrt(var + LN_EPS) * g + b


def _rope_lanes(y, cos, sin_signed, half):
    if 2 * half == LANES:
        partner = pltpu.roll(y, half, 1)
    else:
        lane = lax.broadcasted_iota(jnp.int32, y.shape, 1)
        first = (lane & (2 * half - 1)) < half
        partner = jnp.where(first, pltpu.roll(y, LANES - half, 1), pltpu.roll(y, half, 1))
    return y * cos + partner * sin_signed


def _rope_cols(y, cos, sin_signed, half):
    parts = [_rope_lanes(y[:, j:j + LANES], cos, sin_signed, half) for j in range(0, y.shape[1], LANES)]
    return parts[0] if len(parts) == 1 else jnp.concatenate(parts, axis=1)


def _head_mask(shape, h, width):
    lane = lax.broadcasted_iota(jnp.int32, shape, len(shape) - 1)
    return (lane >= h * width) & (lane < (h + 1) * width)


def _stack_heads(q, width):
    zero = jnp.zeros((), q.dtype)
    parts = [jnp.where(_head_mask(q.shape, h, width), q, zero) for h in range(q.shape[1] // width)]
    return jnp.concatenate(parts, axis=0).astype(BF16)


def _pick_heads(full, t, width):
    per = LANES // width
    cols = []
    for c in range(full.shape[1] // LANES):
        out = None
        for i in range(per):
            h = c * per + i
            blk = full[h * t:(h + 1) * t, c * LANES:(c + 1) * LANES]
            out = blk if out is None else jnp.where(_head_mask(blk.shape, i, width), blk, out)
        cols.append(out)
    return cols[0] if len(cols) == 1 else jnp.concatenate(cols, axis=1)


def _const_spec(block, index):
    return pl.BlockSpec(block, lambda *_: index, pipeline_mode=pl.Buffered(1))


def _full_spec(a):
    return pl.BlockSpec(a.shape, lambda *_: (0,) * a.ndim)


def _params(sem):
    return pltpu.CompilerParams(dimension_semantics=sem, vmem_limit_bytes=VMEM_LIMIT)


def _mem_proj_kernel(m_ref, w_ref, o_ref):
    o_ref[0, 0] = _dot_nt(w_ref[0], m_ref[0].astype(BF16))


def _mem_proj(mem, w_t):
    batch, n_mem, d = mem.shape
    depth, n, _ = w_t.shape
    return pl.pallas_call(
        _mem_proj_kernel,
        grid=(depth, batch),
        in_specs=[pl.BlockSpec((1, n_mem, d), lambda l, b: (b, 0, 0)),
                  pl.BlockSpec((1, n, d), lambda l, b: (l, 0, 0))],
        out_specs=pl.BlockSpec((1, 1, n, n_mem), lambda l, b: (l, b, 0, 0)),
        out_shape=jax.ShapeDtypeStruct((depth, batch, n, n_mem), F32),
        compiler_params=_params(("arbitrary", "arbitrary")),
        name="mem_proj",
    )(mem, w_t)


def _mem_attn_tile(qm, kv_t):
    t = qm.shape[0]
    q4 = _stack_heads(qm, MEM_HEAD_DIM)
    k_t = kv_t[:MEM_WIDTH].astype(BF16)
    v_t = kv_t[MEM_WIDTH:].astype(BF16)
    s = _dot(q4, k_t)
    m = jnp.max(s, axis=-1, keepdims=True)
    e = jnp.exp(s - m)
    p = (e / jnp.sum(e, axis=-1, keepdims=True)).astype(BF16)
    return _pick_heads(_dot_nt(p, v_t), t, MEM_HEAD_DIM)


def _in_a_body(x_ref, w_ref, cos_ref, sin_ref, mkv_ref, q_ref, k_ref, v_ref, g_ref, m_ref):
    xb = x_ref[...].astype(BF16)
    cos = cos_ref[...]
    sin = sin_ref[...]
    mw = MAIN_WIDTH
    half = RET_HEAD_DIM // 2
    q = _rope_cols(_dot(xb, w_ref[0, :, 0:mw]), cos, sin, half)
    q_ref[...] = q.astype(q_ref.dtype)
    k = _rope_cols(_dot(xb, w_ref[0, :, mw:2 * mw]), cos, sin, half) * (RET_HEAD_DIM ** -0.5)
    k_ref[...] = k.astype(k_ref.dtype)
    v_ref[...] = _dot(xb, w_ref[0, :, 2 * mw:3 * mw]).astype(v_ref.dtype)
    g_ref[...] = _silu(_dot(xb, w_ref[0, :, 3 * mw:4 * mw])).astype(g_ref.dtype)
    qm = _dot(xb, w_ref[0, :, 4 * mw:4 * mw + MEM_WIDTH]) * ATTN_SCALE
    if mkv_ref is not None:
        m_ref[...] = _mem_attn_tile(qm.astype(BF16), mkv_ref[0, 0]).astype(m_ref.dtype)
    else:
        m_ref[...] = qm.astype(m_ref.dtype)


def _shift_window(cache_ref, win_ref, t):
    n = cache_ref.shape[2]
    win_ref[0, :, 0:n - t] = cache_ref[0, :, t:n]
    win_ref[0, :, n - t:n] = jnp.zeros((cache_ref.shape[1], t), F32)


def _shift_specs(caches_t, n_steps):
    last = caches_t[0].shape[0] - 1 if caches_t else 0
    assert not caches_t or n_steps > last
    return [pl.BlockSpec((1,) + c.shape[1:], lambda i: (jnp.minimum(i, last), 0, 0)) for c in caches_t]


def _in_a_kernel(xp_ref, xs_ref, w_ref, cosp_ref, sinp_ref, coss_ref, sins_ref, mkv_ref, *refs,
                 n_p, n_shift, dec_seq):
    caches, outs, wins = refs[:n_shift], refs[n_shift:n_shift + 10], refs[n_shift + 10:]
    i = pl.program_id(0)

    @pl.when(i < n_p)
    def _():
        _in_a_body(xp_ref, w_ref, cosp_ref, sinp_ref, mkv_ref, *outs[:5])
        for c_ref, w_ref_ in zip(caches, wins):
            _shift_window(c_ref, w_ref_, dec_seq)

    @pl.when(i == n_p)
    def _():
        _in_a_body(xs_ref, w_ref, coss_ref, sins_ref, None, *outs[5:])


def _in_a(xp, xs, w_bf, tab_p, tab_s, memkv_t, seq, caches_t, dec_seq):
    tp, d = xp.shape
    ts = xs.shape[0]
    tm = TOKEN_TILE
    n_p = tp // tm
    nt = seq // tm
    cl = lambda i: jnp.minimum(i, n_p - 1)
    row = lambda i: (cl(i), 0)
    tab = lambda i: (cl(i) % nt, 0)
    widths = (MAIN_WIDTH,) * 4 + (MEM_WIDTH,)
    shift_specs = _shift_specs(caches_t, n_p)
    return pl.pallas_call(
        functools.partial(_in_a_kernel, n_p=n_p, n_shift=len(caches_t), dec_seq=dec_seq),
        grid=(n_p + 1,),
        in_specs=[pl.BlockSpec((tm, d), row), _full_spec(xs), _const_spec((1,) + w_bf.shape[1:], (0, 0, 0)),
                  pl.BlockSpec((tm, LANES), tab), pl.BlockSpec((tm, LANES), tab),
                  _full_spec(tab_s[0]), _full_spec(tab_s[1]),
                  pl.BlockSpec((1, 1) + memkv_t.shape[2:], lambda i: (0, cl(i) // nt, 0, 0))] + shift_specs,
        out_specs=[pl.BlockSpec((tm, w), row) for w in widths]
                  + [pl.BlockSpec((ts, w), lambda i: (0, 0)) for w in widths] + shift_specs,
        out_shape=[jax.ShapeDtypeStruct((tp, w), BF16) for w in widths]
                  + [jax.ShapeDtypeStruct((ts, w), F32) for w in widths]
                  + [jax.ShapeDtypeStruct(c.shape, F32) for c in caches_t],
        compiler_params=_params(("arbitrary",)),
        name="in_proj_a",
    )(xp, xs, w_bf, *tab_p, *tab_s, memkv_t, *caches_t)


def _retention_kernel(q_ref, k_ref, v_ref, g_ref, mix_ref, st_ref, dec_ref, rdec_ref, kdec_ref,
                      inner_ref, kv_ref):
    c = pl.program_id(1)
    cs = RET_CHUNK

    @pl.when(c == 0)
    def _():
        st_ref[...] = jnp.zeros_like(st_ref)
        row = lax.broadcasted_iota(jnp.int32, (cs, cs), 0).astype(F32)
        col = lax.broadcasted_iota(jnp.int32, (cs, cs), 1).astype(F32)
        diff = row - col
        for h in range(RET_HEADS):
            lg = LOG_G[h]
            dec_ref[h] = jnp.where(diff >= 0, jnp.exp(jnp.maximum(diff, 0.0) * lg), 0.0)
            rdec_ref[h] = jnp.exp((row + 1.0) * lg)
            kdec_ref[h] = jnp.exp((cs - 1.0 - row) * lg)

    nj = q_ref.shape[0] // cs
    for j in range(nj):
        rs = slice(j * cs, (j + 1) * cs)
        for h in range(RET_HEADS):
            hs = slice(h * RET_HEAD_DIM, (h + 1) * RET_HEAD_DIM)
            kh = k_ref[rs, hs]
            vh = v_ref[rs, hs]
            s = _dot_nt(q_ref[rs, hs], kh) * dec_ref[h]
            inner_ref[j, h] = _dot(s.astype(BF16), vh)
            kd = (kh.astype(F32) * kdec_ref[h]).astype(BF16)
            kv_ref[j, h] = _dot_tn(kd, vh)
    for j in range(nj):
        rs = slice(j * cs, (j + 1) * cs)
        for h in range(RET_HEADS):
            hs = slice(h * RET_HEAD_DIM, (h + 1) * RET_HEAD_DIM)
            st = st_ref[0, h]
            cross = _dot(q_ref[rs, hs], st.astype(BF16)) * rdec_ref[h]
            st_ref[0, h] = math.exp(cs * LOG_G[h]) * st + kv_ref[j, h]
            o = inner_ref[j, h] + cross
            mu = jnp.mean(o, axis=-1, keepdims=True)
            oc = o - mu
            var = jnp.mean(oc * oc, axis=-1, keepdims=True)
            on = oc * lax.rsqrt(var + LN_EPS)
            mix_ref[rs, hs] = (g_ref[rs, hs].astype(F32) * on).astype(mix_ref.dtype)


def _retention(q, k, v, g, batch, seq):
    t = q.shape[0]
    rows = RET_CHUNK * RET_CHUNKS_PER_STEP
    ns = seq // rows
    tok = lambda b, c: (b * ns + c, 0)
    sq = (RET_HEADS, RET_CHUNK, RET_CHUNK)
    return pl.pallas_call(
        _retention_kernel,
        grid=(batch, ns),
        in_specs=[pl.BlockSpec((rows, MAIN_WIDTH), tok)] * 4,
        out_specs=[pl.BlockSpec((rows, MAIN_WIDTH), tok),
                   pl.BlockSpec((1, RET_HEADS, RET_HEAD_DIM, RET_HEAD_DIM), lambda b, c: (b, 0, 0, 0))],
        out_shape=[jax.ShapeDtypeStruct((t, MAIN_WIDTH), BF16),
                   jax.ShapeDtypeStruct((batch, RET_HEADS, RET_HEAD_DIM, RET_HEAD_DIM), F32)],
        scratch_shapes=[pltpu.VMEM(sq, F32)] * 3 + [pltpu.VMEM((RET_CHUNKS_PER_STEP,) + sq, F32)] * 2,
        compiler_params=_params(("arbitrary", "arbitrary")),
        name="retention",
    )(q, k, v, g)


def _retention_sample_kernel(q_ref, k_ref, v_ref, g_ref, qm_ref, mkv_ref, st_ref, mix_ref, nst_ref, *, t):
    nb = q_ref.shape[0] // t
    pad_k = jnp.zeros((LANES - t, RET_HEAD_DIM), F32)
    pad_q = jnp.zeros((BF16_SUBLANES - t, RET_HEAD_DIM), F32)
    lhs = lambda x: jnp.concatenate([x, pad_q], axis=0).astype(BF16)
    row = lax.broadcasted_iota(jnp.int32, (t, LANES), 0).astype(F32)
    col = lax.broadcasted_iota(jnp.int32, (t, LANES), 1).astype(F32)
    prow = lax.broadcasted_iota(jnp.int32, (LANES, RET_HEAD_DIM), 0).astype(F32)
    diff = row - col
    for h in range(RET_HEADS):
        lg = LOG_G[h]
        hs = slice(h * RET_HEAD_DIM, (h + 1) * RET_HEAD_DIM)
        dec = jnp.where(diff >= 0, jnp.exp(jnp.maximum(diff, 0.0) * lg), 0.0)
        rdec = jnp.exp((row + 1.0) * lg)
        kdec = jnp.exp((t - 1.0 - prow) * lg)
        for bb in range(nb):
            rs = slice(bb * t, (bb + 1) * t)
            qh = lhs(q_ref[rs, hs])
            kp = jnp.concatenate([k_ref[rs, hs], pad_k], axis=0)
            vp = jnp.concatenate([v_ref[rs, hs], pad_k], axis=0).astype(BF16)
            st = st_ref[0, bb, h]
            inner = _dot(lhs(_dot_nt(qh, kp.astype(BF16))[:t] * dec), vp)[:t]
            cross = _dot(qh, st.astype(BF16))[:t] * rdec
            nst_ref[bb, h] = math.exp(t * lg) * st + _dot_tn((kp * kdec).astype(BF16), vp)
            o = inner + cross
            mu = jnp.mean(o, axis=-1, keepdims=True)
            oc = o - mu
            var = jnp.mean(oc * oc, axis=-1, keepdims=True)
            on = oc * lax.rsqrt(var + LN_EPS)
            mix_ref[rs, hs] = g_ref[rs, hs] * on
    for bb in range(nb):
        rs = slice(bb * t, (bb + 1) * t)
        mix_ref[rs, MAIN_WIDTH:] = _mem_attn_tile(qm_ref[rs, :], mkv_ref[0, bb])


def _retention_sample(q, k, v, g, qm, memkv_t, state, batch, t):
    nb = SAMPLE_BATCH_PER_STEP
    tok = lambda b: (b, 0)
    hd = (RET_HEADS, RET_HEAD_DIM, RET_HEAD_DIM)
    return pl.pallas_call(
        functools.partial(_retention_sample_kernel, t=t),
        grid=(batch // nb,),
        in_specs=[pl.BlockSpec((nb * t, MAIN_WIDTH), tok)] * 4
                 + [pl.BlockSpec((nb * t, MEM_WIDTH), tok),
                    pl.BlockSpec((1, nb) + memkv_t.shape[2:], lambda b: (0, b, 0, 0)),
                    pl.BlockSpec((1, nb) + hd, lambda b: (0, b, 0, 0, 0))],
        out_specs=[pl.BlockSpec((nb * t, D_MODEL), tok), pl.BlockSpec((nb,) + hd, lambda b: (b, 0, 0, 0))],
        out_shape=[jax.ShapeDtypeStruct((batch * t, D_MODEL), F32),
                   jax.ShapeDtypeStruct((batch,) + hd, F32)],
        compiler_params=_params(("parallel",)),
        name="retention_sample",
    )(q, k, v, g, qm, memkv_t, state)


def _dilated_sample_group(q, cache, new, window, dil):
    t = q.shape[0]
    n_buf = cache.shape[1]
    gw = GROUP_WIDTH
    q4 = _stack_heads(q, DIL_HEAD_DIM)
    rows = q4.shape[0]
    newp = jnp.concatenate([new, jnp.zeros((LANES - t, new.shape[1]), F32)], axis=0).astype(BF16)

    def masked(s, first_index):
        key = lax.broadcasted_iota(jnp.int32, s.shape, 1) + first_index
        tok = lax.broadcasted_iota(jnp.int32, s.shape, 0) & (t - 1)
        delta = n_buf + tok - key
        valid = (delta >= 0) & (delta <= window) & ((delta & (dil - 1)) == 0)
        return jnp.where(valid, s, NEG_BIG)

    s_c = masked(_dot(q4, cache[:gw].astype(BF16)), 0)
    s_n = masked(_dot_nt(q4, newp[:, :gw]), n_buf)
    m = jnp.maximum(jnp.max(s_c, axis=-1, keepdims=True), jnp.max(s_n, axis=-1, keepdims=True))
    e_c = jnp.exp(s_c - m)
    e_n = jnp.exp(s_n - m)
    l = jnp.sum(e_c, axis=-1, keepdims=True) + jnp.sum(e_n, axis=-1, keepdims=True)
    inv = 1.0 / l
    full = (_dot_nt((e_c * inv).astype(BF16), cache[gw:].astype(BF16))
            + _dot((e_n * inv).astype(BF16), newp[:, gw:]))
    o = _pick_heads(full, t, DIL_HEAD_DIM)
    lse = _pick_heads(jnp.broadcast_to(m + jnp.log(l), (rows, gw)), t, DIL_HEAD_DIM)
    return o, lse


def _sample_mixer_kernel(*refs, t):
    q_refs, cache_refs, new_refs, new_t_refs = refs[0:3], refs[3:6], refs[6:9], refs[9:12]
    qm_ref, mkv_ref = refs[12], refs[13]
    shifted_refs, mix_ref, win_refs = refs[14:17], refs[17], refs[18:21]
    gw = GROUP_WIDTH
    nb = mix_ref.shape[0] // t
    for bb in range(nb):
        rs = slice(bb * t, (bb + 1) * t)
        outs, lses = [], []
        col0 = (pl.program_id(0) * nb + bb) * t
        block = pl.multiple_of((col0 // LANES) * LANES, LANES)
        shift = LANES - t - col0 % LANES
        for g, (window, dil) in enumerate(DIL_PAIRS):
            o, lse = _dilated_sample_group(q_refs[g][rs, :], cache_refs[g][bb], new_refs[g][rs, :], window, dil)
            outs.append(o)
            lses.append(lse)
            win_refs[g][bb] = shifted_refs[g][bb]
            new_cols = pltpu.roll(new_t_refs[g][:, pl.ds(block, LANES)], shift, 1)
            win_refs[g][bb, :, LANES - t:] = new_cols[:, LANES - t:]
        m = jnp.maximum(jnp.maximum(lses[0], lses[1]), lses[2])
        es = [jnp.exp(v - m) for v in lses]
        inv = 1.0 / (es[0] + es[1] + es[2])
        for g in range(3):
            mix_ref[rs, g * gw:(g + 1) * gw] = outs[g] * (es[g] * inv)
        mix_ref[rs, MAIN_WIDTH:] = _mem_attn_tile(qm_ref[rs, :], mkv_ref[0, bb])


def _sample_mixer(q_g, caches_t, new_kv, new_kv_t, qm, memkv_t, shifted, batch, t):
    assert LANES % t == 0
    gw = GROUP_WIDTH
    nb = SAMPLE_BATCH_PER_STEP
    tok = lambda b: (b, 0)
    big = lambda b: (b, 0, 0)
    cache_specs = [pl.BlockSpec((nb,) + c.shape[1:], big) for c in caches_t]
    tail_specs = [pl.BlockSpec((nb, c.shape[1], LANES), lambda b, j=c.shape[2] // LANES - 1: (b, 0, j))
                  for c in shifted]
    first_shifted = 3 + len(caches_t) + 3 + 3 + 2
    return pl.pallas_call(
        functools.partial(_sample_mixer_kernel, t=t),
        grid=(batch // nb,),
        in_specs=[pl.BlockSpec((nb * t, gw), tok)] * 3 + cache_specs + [pl.BlockSpec((nb * t, 2 * gw), tok)] * 3
                 + [_full_spec(a) for a in new_kv_t]
                 + [pl.BlockSpec((nb * t, MEM_WIDTH), tok),
                    pl.BlockSpec((1, nb) + memkv_t.shape[2:], lambda b: (1, b, 0, 0))] + tail_specs,
        out_specs=[pl.BlockSpec((nb * t, D_MODEL), tok)] + tail_specs,
        out_shape=[jax.ShapeDtypeStruct((batch * t, D_MODEL), F32)]
                  + [jax.ShapeDtypeStruct(c.shape, F32) for c in shifted],
        input_output_aliases={first_shifted + g: 1 + g for g in range(len(shifted))},
        compiler_params=_params(("parallel",)),
        name="sample_mixer",
    )(*q_g, *caches_t, *new_kv, *new_kv_t, qm, memkv_t, *shifted)


def _natural_rows(ref, dil, scr):
    if dil == 1:
        return ref[0, 0].astype(F32)
    n = ref.shape[2]
    for r in range(dil):
        v = ref[0, r].astype(F32)
        for c in range(v.shape[1] // LANES):
            scr[c, pl.ds(r, n, stride=dil), :] = v[:, c * LANES:(c + 1) * LANES]
    return jnp.concatenate([scr[c] for c in range(scr.shape[0])], axis=1)


def _post_body(mix, x_ref, wo_ref, g1_ref, b1_ref, wi_ref, w2_ref, g2_ref, b2_ref, out_ref, act_ref):
    rows = x_ref.shape[0]
    x1 = _layer_norm(ALPHA * x_ref[...] + _dot(mix, wo_ref[0]), g1_ref[0], b1_ref[0])
    x1b = x1.astype(BF16)
    for c in range(0, FFN_HIDDEN, FFN_COL_CHUNK):
        gate = _dot(x1b, wi_ref[0, :, c:c + FFN_COL_CHUNK])
        up = _dot(x1b, wi_ref[0, :, FFN_HIDDEN + c:FFN_HIDDEN + c + FFN_COL_CHUNK])
        act_ref[0:rows, c:c + FFN_COL_CHUNK] = (_silu(gate) * up).astype(BF16)
    y = _dot(act_ref[0:rows, :], w2_ref[0])
    out_ref[...] = _layer_norm(ALPHA * x1 + y, g2_ref[0], b2_ref[0])


def _post_kernel(*refs, dils, n_mix_p, n_mix_s, n_p, n_shift, dec_seq):
    mixp_refs, refs = refs[:n_mix_p], refs[n_mix_p:]
    xp_ref, refs = refs[0], refs[1:]
    mixs_refs, refs = refs[:n_mix_s], refs[n_mix_s:]
    xs_ref, refs = refs[0], refs[1:]
    weights, refs = refs[:7], refs[7:]
    caches, refs = refs[:n_shift], refs[n_shift:]
    outp_ref, outs_ref, refs = refs[0], refs[1], refs[2:]
    wins, refs = refs[:n_shift], refs[n_shift:]
    act_ref = refs[0]
    scrs = list(refs[1:])
    i = pl.program_id(0)

    @pl.when(i < n_p)
    def _():
        if dils is not None:
            o_refs, l_refs, mem_ref = mixp_refs[0:3], mixp_refs[3:6], mixp_refs[6]
            pool = list(scrs)
            nat = lambda r, d: _natural_rows(r, d, pool.pop(0) if d > 1 else None)
            os_ = [nat(o_refs[g], dils[g]) for g in range(3)]
            ls = [nat(l_refs[g], dils[g]) for g in range(3)]
            m = jnp.maximum(jnp.maximum(ls[0], ls[1]), ls[2])
            es = [jnp.exp(v - m) for v in ls]
            inv = 1.0 / (es[0] + es[1] + es[2])
            parts = [(os_[g] * (es[g] * inv)).astype(BF16) for g in range(3)]
            parts.append(mem_ref[...].astype(BF16))
        else:
            parts = [r[...].astype(BF16) for r in mixp_refs]
        mix = parts[0] if len(parts) == 1 else jnp.concatenate(parts, axis=1)
        _post_body(mix, xp_ref, *weights, outp_ref, act_ref)
        for c_ref, w_ref in zip(caches, wins):
            _shift_window(c_ref, w_ref, dec_seq)

    @pl.when(i == n_p)
    def _():
        parts = [r[...].astype(BF16) for r in mixs_refs]
        mix = parts[0] if len(parts) == 1 else jnp.concatenate(parts, axis=1)
        _post_body(mix, xs_ref, *weights, outs_ref, act_ref)


def _post(layer, mix_p, xp, mix_s, xs, wo, g1, b1, wi, w2, g2, b2, dils, seq, caches_t=(), dec_seq=0):
    tp, d = xp.shape
    tm = TOKEN_TILE
    n_p = tp // tm
    nt = seq // tm
    cl = lambda i: jnp.minimum(i, n_p - 1)
    row = lambda i: (cl(i), 0)
    if dils is None:
        mixp_specs = [pl.BlockSpec((tm, a.shape[1]), row) for a in mix_p]
        scratch = []
    else:
        res = lambda i: (cl(i) // nt, 0, cl(i) % nt, 0)
        mixp_specs = [pl.BlockSpec((1, dl, tm // dl, GROUP_WIDTH), res) for dl in dils] * 2
        mixp_specs.append(pl.BlockSpec((tm, MEM_WIDTH), row))
        scratch = [pltpu.VMEM((GROUP_WIDTH // LANES, tm, LANES), F32) for dl in dils * 2 if dl > 1]
    lsel = (layer, 0, 0)
    vec = _const_spec((1, 1, d), lsel)
    shift_specs = _shift_specs(caches_t, n_p)
    return pl.pallas_call(
        functools.partial(_post_kernel, dils=dils, n_mix_p=len(mix_p), n_mix_s=len(mix_s), n_p=n_p,
                          n_shift=len(caches_t), dec_seq=dec_seq),
        grid=(n_p + 1,),
        in_specs=mixp_specs + [pl.BlockSpec((tm, d), row)] + [_full_spec(a) for a in mix_s] + [_full_spec(xs)]
                 + [_const_spec((1,) + wo.shape[1:], lsel), vec, vec,
                    _const_spec((1,) + wi.shape[1:], lsel), _const_spec((1,) + w2.shape[1:], lsel), vec, vec]
                 + shift_specs,
        out_specs=[pl.BlockSpec((tm, d), row), pl.BlockSpec(xs.shape, lambda i: (0, 0))] + shift_specs,
        out_shape=[jax.ShapeDtypeStruct((tp, d), F32), jax.ShapeDtypeStruct(xs.shape, F32)]
                  + [jax.ShapeDtypeStruct(c.shape, F32) for c in caches_t],
        scratch_shapes=[pltpu.VMEM((tm, FFN_HIDDEN), BF16)] + scratch,
        compiler_params=_params(("arbitrary",)),
        name="post" if dils is None else "post_combine",
    )(*mix_p, xp, *mix_s, xs, wo, g1, b1, wi, w2, g2, b2, *caches_t)


def _residue_major(ref, y, dil, scr):
    if dil == 1:
        ref[...] = y.astype(ref.dtype).reshape(ref.shape)
        return
    n = ref.shape[2]
    ncol = y.shape[1] // LANES
    for c in range(ncol):
        scr[c] = y[:, c * LANES:(c + 1) * LANES]
    for r in range(dil):
        rows = [scr[c, pl.ds(r, n, stride=dil), :] for c in range(ncol)]
        ref[0, r] = jnp.concatenate(rows, axis=1).astype(ref.dtype)


def _in_b_body(x_ref, wkv_ref, wq_ref, cos_ref, sin_ref, mkv_ref,
               q_refs, kv_refs, m_ref, win_refs, scrs, dils, win_preds):
    xb = x_ref[...].astype(BF16)
    cos = cos_ref[...]
    sin = sin_ref[...]
    gw = GROUP_WIDTH
    mw = MAIN_WIDTH
    half = DIL_HEAD_DIM // 2
    tm = x_ref.shape[0]
    pool = list(scrs)
    for g in range(3):
        ks = slice(g * gw, (g + 1) * gw)
        vs = slice(mw + g * gw, mw + (g + 1) * gw)
        k = _rope_cols(_dot(xb, wkv_ref[:, ks]), cos, sin, half)
        v = _dot(xb, wkv_ref[:, vs])
        kv = jnp.concatenate([k, v], axis=1)
        _residue_major(kv_refs[g], kv, dils[g], pool.pop(0) if dils[g] > 1 else None)
        q = _rope_cols(_dot(xb, wq_ref[0, :, ks]), cos, sin, half) * ATTN_SCALE
        _residue_major(q_refs[g], q, dils[g], pool.pop(0) if dils[g] > 1 else None)

        def window(g=g, kv=kv):
            win_refs[g][0] = jnp.transpose(kv[tm - win_refs[g].shape[2]:, :])

        if win_preds[g] is None:
            window()
        else:
            pl.when(win_preds[g])(window)
    qm = _dot(xb, wq_ref[0, :, mw:mw + MEM_WIDTH]) * ATTN_SCALE
    if mkv_ref is not None:
        m_ref[...] = _mem_attn_tile(qm.astype(BF16), mkv_ref[0, 0]).astype(m_ref.dtype)
    else:
        m_ref[...] = qm.astype(m_ref.dtype)


def _in_b_kernel(xp_ref, xs_ref, wkv_ref, wq_ref, cosp_ref, sinp_ref, coss_ref, sins_ref, mkv_ref, *refs,
                 dils, win_tiles, nt, n_p):
    outp, outs, scrs = refs[0:10], refs[10:20], refs[20:]
    i = pl.program_id(0)

    @pl.when(i < n_p)
    def _():
        j = i % nt
        _in_b_body(xp_ref, wkv_ref, wq_ref, cosp_ref, sinp_ref, mkv_ref,
                   outp[0:3], outp[3:6], outp[6], outp[7:10], scrs, dils,
                   [j >= nt - w for w in win_tiles])

    @pl.when(i == n_p)
    def _():
        _in_b_body(xs_ref, wkv_ref, wq_ref, coss_ref, sins_ref, None,
                   outs[0:3], outs[3:6], outs[6], outs[7:10], (), (1, 1, 1), [None] * 3)


def _in_b(xp, xs, w_kv, w_q, tabs_p, tabs_s, memkv_t, batch, seq, dils, windows):
    tp, d = xp.shape
    ts = xs.shape[0]
    gw = GROUP_WIDTH
    tm = TOKEN_TILE
    n_p = tp // tm
    nt = seq // tm
    cl = lambda i: jnp.minimum(i, n_p - 1)
    row = lambda i: (cl(i), 0)
    tab = lambda i: (cl(i) % nt, 0)
    res = lambda i: (cl(i) // nt, 0, cl(i) % nt, 0)
    fix2 = lambda i: (0, 0)
    out_specs = ([pl.BlockSpec((1, dl, tm // dl, gw), res) for dl in dils]
                 + [pl.BlockSpec((1, dl, tm // dl, 2 * gw), res) for dl in dils]
                 + [pl.BlockSpec((tm, MEM_WIDTH), row)])
    out_shape = ([jax.ShapeDtypeStruct((batch, dl, seq // dl, gw), BF16) for dl in dils]
                 + [jax.ShapeDtypeStruct((batch, dl, seq // dl, 2 * gw), BF16) for dl in dils]
                 + [jax.ShapeDtypeStruct((tp, MEM_WIDTH), BF16)])
    win_tiles = []
    for w in windows:
        rb = min(w, tm)
        nblk = w // rb
        win_tiles.append(nblk)
        out_specs.append(pl.BlockSpec(
            (1, 2 * gw, rb), lambda i, nblk=nblk: (cl(i) // nt, 0, jnp.maximum(cl(i) % nt - (nt - nblk), 0))))
        out_shape.append(jax.ShapeDtypeStruct((batch, 2 * gw, w), F32))
    out_specs += ([pl.BlockSpec((ts, gw), fix2)] * 3 + [pl.BlockSpec((ts, 2 * gw), fix2)] * 3
                  + [pl.BlockSpec((ts, MEM_WIDTH), fix2)] + [pl.BlockSpec((1, 2 * gw, ts), lambda i: (0, 0, 0))] * 3)
    out_shape += ([jax.ShapeDtypeStruct((ts, gw), F32)] * 3 + [jax.ShapeDtypeStruct((ts, 2 * gw), F32)] * 3
                  + [jax.ShapeDtypeStruct((ts, MEM_WIDTH), F32)] + [jax.ShapeDtypeStruct((1, 2 * gw, ts), F32)] * 3)
    scratch = []
    for dl in dils:
        if dl > 1:
            scratch += [pltpu.VMEM((2 * gw // LANES, tm, LANES), F32), pltpu.VMEM((gw // LANES, tm, LANES), F32)]
    return pl.pallas_call(
        functools.partial(_in_b_kernel, dils=tuple(dils), win_tiles=tuple(win_tiles), nt=nt, n_p=n_p),
        grid=(n_p + 1,),
        in_specs=[pl.BlockSpec((tm, d), row), _full_spec(xs), _const_spec(w_kv.shape, (0, 0)),
                  _const_spec((1,) + w_q.shape[1:], (0, 0, 0)),
                  pl.BlockSpec((tm, LANES), tab), pl.BlockSpec((tm, LANES), tab),
                  _full_spec(tabs_s[0]), _full_spec(tabs_s[1]),
                  pl.BlockSpec((1, 1) + memkv_t.shape[2:], lambda i: (1, cl(i) // nt, 0, 0))],
        out_specs=out_specs,
        out_shape=out_shape,
        scratch_shapes=scratch,
        compiler_params=_params(("arbitrary",)),
        name="in_proj_b",
    )(xp, xs, w_kv, w_q, *tabs_p, *tabs_s, memkv_t)


def _dilated_kernel(q_ref, kv_ref, o_ref, lse_ref, ring_ref):
    i = pl.program_id(2)
    blk = DIL_BLOCK
    gw = GROUP_WIDTH
    nres = q_ref.shape[1]
    nsub = q_ref.shape[2] // blk
    slot = i & 1

    @pl.when(i == 0)
    def _():
        for r in range(nres):
            ring_ref[r, 1] = jnp.zeros(ring_ref.shape[2:], ring_ref.dtype)

    rows = GROUP_HEADS * blk
    qi = lax.broadcasted_iota(jnp.int32, (rows, 2 * blk), 0) & (blk - 1)
    kj = lax.broadcasted_iota(jnp.int32, (rows, 2 * blk), 1)
    delta = qi + blk - kj
    band = (delta >= 0) & (delta <= blk)
    for r in range(nres):
        cur = kv_ref[0, r]
        ring_ref[r, slot] = cur[(nsub - 1) * blk:]
        kext = jnp.concatenate([ring_ref[r, 1 - slot], cur], axis=0)
        for j in range(nsub):
            rs = slice(j * blk, (j + 1) * blk)
            q4 = _stack_heads(q_ref[0, r, rs, :], DIL_HEAD_DIM)
            kv = kext[j * blk:(j + 2) * blk]
            valid = band if j > 0 else band & ((i > 0) | (kj >= blk))
            s = jnp.where(valid, _dot_nt(q4, kv[:, :gw]), NEG_BIG)
            m = jnp.max(s, axis=-1, keepdims=True)
            e = jnp.exp(s - m)
            l = jnp.sum(e, axis=-1, keepdims=True)
            p = (e * (1.0 / l)).astype(BF16)
            full = _dot(p, kv[:, gw:])
            o_ref[0, r, rs, :] = _pick_heads(full, blk, DIL_HEAD_DIM).astype(o_ref.dtype)
            lse_ref[0, r, rs, :] = _pick_heads(jnp.broadcast_to(m + jnp.log(l), (rows, gw)), blk, DIL_HEAD_DIM)


def _dilated(q, kv, dil):
    batch, _, m, gw = q.shape
    nsub = min(DIL_BLOCKS_PER_STEP, m // DIL_BLOCK)
    nres = min(dil, DIL_BLOCKS_PER_STEP // nsub)
    rows = DIL_BLOCK * nsub
    idx = lambda b, r, i: (b, r, i, 0)
    return pl.pallas_call(
        _dilated_kernel,
        grid=(batch, dil // nres, m // rows),
        in_specs=[pl.BlockSpec((1, nres, rows, gw), idx), pl.BlockSpec((1, nres, rows, 2 * gw), idx)],
        out_specs=[pl.BlockSpec((1, nres, rows, gw), idx)] * 2,
        out_shape=[jax.ShapeDtypeStruct(q.shape, BF16), jax.ShapeDtypeStruct(q.shape, F32)],
        scratch_shapes=[pltpu.VMEM((nres, 2, DIL_BLOCK, 2 * gw), BF16)],
        compiler_params=_params(("arbitrary", "arbitrary", "arbitrary")),
        name=f"dilated_d{dil}",
    )(q, kv)


def _rope_tables(base, n):
    split = min(ROPE_SPLIT, n)
    hi = base + split * jnp.arange(n // split, dtype=F32)
    lo = jnp.arange(split, dtype=F32)
    inv = ROPE_THETA ** (-jnp.arange(0, RET_HEAD_DIM, 2, dtype=F32) / RET_HEAD_DIM)
    inv_h = inv[0::2]
    neg = lambda m: jnp.concatenate([-jnp.ones((m,), F32), jnp.ones((m,), F32)])

    def token_major(freq, sign):
        xa, xb = hi[:, None] * freq[None, :], lo[:, None] * freq[None, :]
        ca, sa, cb, sb = jnp.cos(xa), jnp.sin(xa), jnp.cos(xb), jnp.sin(xb)
        cos = ca[:, None, :] * cb[None, :, :] - sa[:, None, :] * sb[None, :, :]
        sin = (sa[:, None, :] * cb[None, :, :] + ca[:, None, :] * sb[None, :, :]) * sign
        return cos.reshape(n, -1), sin.reshape(n, -1)

    f_big = jnp.concatenate([inv, inv])
    f_half = jnp.concatenate([inv_h, inv_h])
    return (token_major(f_big, neg(RET_HEAD_DIM // 2)),
            token_major(jnp.tile(f_half, 2), jnp.tile(neg(DIL_HEAD_DIM // 2), 2)))


def _to_feature_major(x5):
    b, w = x5.shape[0], x5.shape[1]
    return jnp.transpose(x5, (0, 2, 3, 4, 1)).reshape(b, 2 * GROUP_WIDTH, w)


def _from_feature_major(xt):
    b, _, w = xt.shape
    return jnp.transpose(xt.reshape(b, 2, GROUP_HEADS, DIL_HEAD_DIM, w), (0, 4, 1, 2, 3))


def kernel(x_prompt, x_sample, mem_prompt, cache_mem_kv, state_ret, cache_win_kv_g1, cache_win_kv_g2, cache_win_kv_g3, w_in_a, w_in_b, w_out, w_kv_shared, w_mem_kv, ln_mix_g, ln_mix_b, ln_ffn_g, ln_ffn_b, w_ffn_in, w_ffn_out):
    batch, seq, d = x_prompt.shape
    dec_batch, dec_seq, _ = x_sample.shape
    n_mem = mem_prompt.shape[1]
    gw = GROUP_WIDTH
    ts = dec_batch * dec_seq
    win_caches = (cache_win_kv_g1, cache_win_kv_g2, cache_win_kv_g3)
    dils = tuple(dl for _, dl in DIL_PAIRS)

    tab_a_p, tab_b_p = _rope_tables(0.0, seq)
    tab_a_s, tab_b_s = _rope_tables(float(PAST_LEN), dec_seq)
    tab_a_s = tuple(jnp.tile(a, (dec_batch, 1)) for a in tab_a_s)
    tab_b_s = tuple(jnp.tile(a, (dec_batch, 1)) for a in tab_b_s)

    w_a = w_in_a.astype(BF16)
    w_kv = w_kv_shared.astype(BF16)
    w_q = w_in_b.astype(BF16)
    w_o = w_out.astype(BF16)
    w_fi = w_ffn_in.astype(BF16)
    w_fo = w_ffn_out.astype(BF16)
    w_mem_t = jnp.transpose(w_mem_kv, (0, 2, 1)).astype(BF16)
    lnv = lambda a: a.reshape(DEPTH, 1, d)
    ln = (lnv(ln_mix_g), lnv(ln_mix_b), lnv(ln_ffn_g), lnv(ln_ffn_b))

    memkv_p = _mem_proj(mem_prompt, w_mem_t)
    memkv_s = jnp.transpose(cache_mem_kv, (0, 1, 3, 4, 5, 2)).reshape(DEPTH, dec_batch, 2 * MEM_WIDTH, n_mem)

    xp = x_prompt.reshape(batch * seq, d)
    xs = x_sample.reshape(ts, d)

    def post(l, mix_p, mix_s, dls, shift=()):
        return _post(l, mix_p, xp, mix_s, xs, w_o, ln[0], ln[1], w_fi, w_fo, ln[2], ln[3], dls, seq,
                     shift, dec_seq)

    caches_t = [_to_feature_major(c) for c in win_caches]

    outs = _in_a(xp, xs, w_a, tab_a_p, tab_a_s, memkv_p, seq, caches_t[:2], dec_seq)
    q, k, v, g, mem_o = outs[:5]
    mix_p, state_p = _retention(q, k, v, g, batch, seq)
    q, k, v, g, qm = outs[5:10]
    shifted = list(outs[10:])
    mix_s, state_s = _retention_sample(q, k, v, g, qm, memkv_s, state_ret, dec_batch, dec_seq)
    xp, xs, *shifted_big = post(0, [mix_p, mem_o], [mix_s], None, caches_t[2:])
    shifted += shifted_big

    windows_p = tuple(min(w, seq) for w, _ in DIL_PAIRS)
    outs = _in_b(xp, xs, w_kv, w_q, tab_b_p, tab_b_s, memkv_p, batch, seq, dils, windows_p)
    q_g, kv_g, mem_o, win_p = outs[0:3], outs[3:6], outs[6], outs[7:10]
    att = [_dilated(q_g[i], kv_g[i], dils[i]) for i in range(3)]

    q_s, kv_new, qm, new_t = outs[10:13], outs[13:16], outs[16], outs[17:20]
    new_t = [a.reshape(2 * gw, ts) for a in new_t]
    mix_s, *win_s = _sample_mixer(q_s, caches_t, kv_new, new_t, qm, memkv_s, shifted, dec_batch, dec_seq)

    xp, xs = post(1, [a[0] for a in att] + [a[1] for a in att] + [mem_o], [mix_s], dils)

    memkv_out = jnp.transpose(memkv_p.reshape(DEPTH, batch, 2, MEM_HEADS, MEM_HEAD_DIM, n_mem), (0, 1, 5, 2, 3, 4))
    return (xp.reshape(batch, seq, d), xs.reshape(dec_batch, dec_seq, d),
            state_p[None], state_s[None], memkv_out,
            _from_feature_major(win_p[0]), _from_feature_major(win_p[1]), _from_feature_major(win_p[2]),
            _from_feature_major(win_s[0]), _from_feature_major(win_s[1]), _from_feature_major(win_s[2]))
```

```python
import functools
import math

import jax
import jax.numpy as jnp
from jax import lax
from jax.experimental import pallas as pl
from jax.experimental.pallas import tpu as pltpu

F32 = jnp.float32
BF16 = jnp.bfloat16

D_MODEL = 1024
MEM_HEADS = 4
MEM_HEAD_DIM = 64
MEM_WIDTH = MEM_HEADS * MEM_HEAD_DIM
MAIN_WIDTH = D_MODEL - MEM_WIDTH
RET_HEADS = 6
RET_HEAD_DIM = MAIN_WIDTH // RET_HEADS
RET_CHUNK = 128
RET_CHUNKS_PER_STEP = 8
DIL_PAIRS = ((128, 1), (512, 4), (2048, 16))
GROUP_HEADS = 4
DIL_HEAD_DIM = 64
GROUP_WIDTH = GROUP_HEADS * DIL_HEAD_DIM
DIL_BLOCK = 128
DIL_BLOCKS_PER_STEP = 8
FFN_HIDDEN = 2816
ROPE_THETA = 10000.0
ROPE_SPLIT = 64
LN_EPS = 1e-5
DEPTH = 2
ALPHA = (2 * DEPTH) ** 0.25
PAST_LEN = 8192
NEG_BIG = -1e30
ATTN_SCALE = DIL_HEAD_DIM ** -0.5

LANES = 128
BF16_SUBLANES = 16
SAMPLE_BATCH_PER_STEP = 2
VMEM_LIMIT = 56 * 1024 * 1024
FFN_COL_CHUNK = 256
TOKEN_TILE = 512

LOG_G = tuple(math.log1p(-(2.0 ** (-5.0 - h))) for h in range(RET_HEADS))


def _dot(a, b):
    return jnp.dot(a, b, preferred_element_type=F32)


def _dot_nt(a, b):
    return lax.dot_general(a, b, (((1,), (1,)), ((), ())), preferred_element_type=F32)


def _dot_tn(a, b):
    return lax.dot_general(a, b, (((0,), (0,)), ((), ())), preferred_element_type=F32)


def _silu(x):
    return x / (1.0 + jnp.exp(-x))


def _layer_norm(z, g, b):
    mu = jnp.mean(z, axis=-1, keepdims=True)
    zc = z - mu
    var = jnp.mean(zc * zc, axis=-1, keepdims=True)
    return zc * lax.rsqrt(var + LN_EPS) * g + b


def _rope_lanes(y, cos, sin_signed, half):
    if 2 * half == LANES:
        partner = pltpu.roll(y, half, 1)
    else:
        lane = lax.broadcasted_iota(jnp.int32, y.shape, 1)
        first = (lane & (2 * half - 1)) < half
        partner = jnp.where(first, pltpu.roll(y, LANES - half, 1), pltpu.roll(y, half, 1))
    return y * cos + partner * sin_signed


def _rope_cols(y, cos, sin_signed, half):
    parts = [_rope_lanes(y[:, j:j + LANES], cos, sin_signed, half) for j in range(0, y.shape[1], LANES)]
    return parts[0] if len(parts) == 1 else jnp.concatenate(parts, axis=1)


def _head_mask(shape, h, width):
    lane = lax.broadcasted_iota(jnp.int32, shape, len(shape) - 1)
    return (lane >= h * width) & (lane < (h + 1) * width)


def _stack_heads(q, width):
    zero = jnp.zeros((), q.dtype)
    parts = [jnp.where(_head_mask(q.shape, h, width), q, zero) for h in range(q.shape[1] // width)]
    return jnp.concatenate(parts, axis=0).astype(BF16)


def _pick_heads(full, t, width):
    per = LANES // width
    cols = []
    for c in range(full.shape[1] // LANES):
        out = None
        for i in range(per):
            h = c * per + i
            blk = full[h * t:(h + 1) * t, c * LANES:(c + 1) * LANES]
            out = blk if out is None else jnp.where(_head_mask(blk.shape, i, width), blk, out)
        cols.append(out)
    return cols[0] if len(cols) == 1 else jnp.concatenate(cols, axis=1)


def _const_spec(block, index):
    return pl.BlockSpec(block, lambda *_: index, pipeline_mode=pl.Buffered(1))


def _full_spec(a):
    return pl.BlockSpec(a.shape, lambda *_: (0,) * a.ndim)


def _params(sem):
    return pltpu.CompilerParams(dimension_semantics=sem, vmem_limit_bytes=VMEM_LIMIT)


def _mem_proj_kernel(m_ref, w_ref, o_ref):
    o_ref[0, 0] = _dot_nt(w_ref[0], m_ref[0].astype(BF16))


def _mem_proj(mem, w_t):
    batch, n_mem, d = mem.shape
    depth, n, _ = w_t.shape
    return pl.pallas_call(
        _mem_proj_kernel,
        grid=(depth, batch),
        in_specs=[pl.BlockSpec((1, n_mem, d), lambda l, b: (b, 0, 0)),
                  pl.BlockSpec((1, n, d), lambda l, b: (l, 0, 0))],
        out_specs=pl.BlockSpec((1, 1, n, n_mem), lambda l, b: (l, b, 0, 0)),
        out_shape=jax.ShapeDtypeStruct((depth, batch, n, n_mem), F32),
        compiler_params=_params(("arbitrary", "arbitrary")),
        name="mem_proj",
    )(mem, w_t)


def _mem_attn_tile(qm, kv_t):
    t = qm.shape[0]
    q4 = _stack_heads(qm, MEM_HEAD_DIM)
    k_t = kv_t[:MEM_WIDTH].astype(BF16)
    v_t = kv_t[MEM_WIDTH:].astype(BF16)
    s = _dot(q4, k_t)
    m = jnp.max(s, axis=-1, keepdims=True)
    e = jnp.exp(s - m)
    p = (e / jnp.sum(e, axis=-1, keepdims=True)).astype(BF16)
    return _pick_heads(_dot_nt(p, v_t), t, MEM_HEAD_DIM)


def _in_a_body(x_ref, w_ref, cos_ref, sin_ref, mkv_ref, q_ref, k_ref, v_ref, g_ref, m_ref):
    xb = x_ref[...].astype(BF16)
    cos = cos_ref[...]
    sin = sin_ref[...]
    mw = MAIN_WIDTH
    half = RET_HEAD_DIM // 2
    q = _rope_cols(_dot(xb, w_ref[0, :, 0:mw]), cos, sin, half)
    q_ref[...] = q.astype(q_ref.dtype)
    k = _rope_cols(_dot(xb, w_ref[0, :, mw:2 * mw]), cos, sin, half) * (RET_HEAD_DIM ** -0.5)
    k_ref[...] = k.astype(k_ref.dtype)
    v_ref[...] = _dot(xb, w_ref[0, :, 2 * mw:3 * mw]).astype(v_ref.dtype)
    g_ref[...] = _silu(_dot(xb, w_ref[0, :, 3 * mw:4 * mw])).astype(g_ref.dtype)
    qm = _dot(xb, w_ref[0, :, 4 * mw:4 * mw + MEM_WIDTH]) * ATTN_SCALE
    if mkv_ref is not None:
        m_ref[...] = _mem_attn_tile(qm.astype(BF16), mkv_ref[0, 0]).astype(m_ref.dtype)
    else:
        m_ref[...] = qm.astype(m_ref.dtype)


def _shift_window(cache_ref, win_ref, t):
    n = cache_ref.shape[2]
    win_ref[0, :, 0:n - t] = cache_ref[0, :, t:n]
    win_ref[0, :, n - t:n] = jnp.zeros((cache_ref.shape[1], t), F32)


def _shift_specs(caches_t, n_steps):
    last = caches_t[0].shape[0] - 1 if caches_t else 0
    assert not caches_t or n_steps > last
    return [pl.BlockSpec((1,) + c.shape[1:], lambda i: (jnp.minimum(i, last), 0, 0)) for c in caches_t]


def _in_a_kernel(xp_ref, xs_ref, w_ref, cosp_ref, sinp_ref, coss_ref, sins_ref, mkv_ref, *refs,
                 n_p, n_shift, dec_seq):
    caches, outs, wins = refs[:n_shift], refs[n_shift:n_shift + 10], refs[n_shift + 10:]
    i = pl.program_id(0)

    @pl.when(i < n_p)
    def _():
        _in_a_body(xp_ref, w_ref, cosp_ref, sinp_ref, mkv_ref, *outs[:5])
        for c_ref, w_ref_ in zip(caches, wins):
            _shift_window(c_ref, w_ref_, dec_seq)

    @pl.when(i == n_p)
    def _():
        _in_a_body(xs_ref, w_ref, coss_ref, sins_ref, None, *outs[5:])


def _in_a(xp, xs, w_bf, tab_p, tab_s, memkv_t, seq, caches_t, dec_seq):
    tp, d = xp.shape
    ts = xs.shape[0]
    tm = TOKEN_TILE
    n_p = tp // tm
    nt = seq // tm
    cl = lambda i: jnp.minimum(i, n_p - 1)
    row = lambda i: (cl(i), 0)
    tab = lambda i: (cl(i) % nt, 0)
    widths = (MAIN_WIDTH,) * 4 + (MEM_WIDTH,)
    shift_specs = _shift_specs(caches_t, n_p)
    return pl.pallas_call(
        functools.partial(_in_a_kernel, n_p=n_p, n_shift=len(caches_t), dec_seq=dec_seq),
        grid=(n_p + 1,),
        in_specs=[pl.BlockSpec((tm, d), row), _full_spec(xs), _const_spec((1,) + w_bf.shape[1:], (0, 0, 0)),
                  pl.BlockSpec((tm, LANES), tab), pl.BlockSpec((tm, LANES), tab),
                  _full_spec(tab_s[0]), _full_spec(tab_s[1]),
                  pl.BlockSpec((1, 1) + memkv_t.shape[2:], lambda i: (0, cl(i) // nt, 0, 0))] + shift_specs,
        out_specs=[pl.BlockSpec((tm, w), row) for w in widths]
                  + [pl.BlockSpec((ts, w), lambda i: (0, 0)) for w in widths] + shift_specs,
        out_shape=[jax.ShapeDtypeStruct((tp, w), BF16) for w in widths]
                  + [jax.ShapeDtypeStruct((ts, w), F32) for w in widths]
                  + [jax.ShapeDtypeStruct(c.shape, F32) for c in caches_t],
        compiler_params=_params(("arbitrary",)),
        name="in_proj_a",
    )(xp, xs, w_bf, *tab_p, *tab_s, memkv_t, *caches_t)


def _retention_kernel(q_ref, k_ref, v_ref, g_ref, mix_ref, st_ref, dec_ref, rdec_ref, kdec_ref,
                      inner_ref, kv_ref):
    c = pl.program_id(1)
    cs = RET_CHUNK

    @pl.when(c == 0)
    def _():
        st_ref[...] = jnp.zeros_like(st_ref)
        row = lax.broadcasted_iota(jnp.int32, (cs, cs), 0).astype(F32)
        col = lax.broadcasted_iota(jnp.int32, (cs, cs), 1).astype(F32)
        diff = row - col
        for h in range(RET_HEADS):
            lg = LOG_G[h]
            dec_ref[h] = jnp.where(diff >= 0, jnp.exp(jnp.maximum(diff, 0.0) * lg), 0.0)
            rdec_ref[h] = jnp.exp((row + 1.0) * lg)
            kdec_ref[h] = jnp.exp((cs - 1.0 - row) * lg)

    nj = q_ref.shape[0] // cs
    for j in range(nj):
        rs = slice(j * cs, (j + 1) * cs)
        for h in range(RET_HEADS):
            hs = slice(h * RET_HEAD_DIM, (h + 1) * RET_HEAD_DIM)
            kh = k_ref[rs, hs]
            vh = v_ref[rs, hs]
            s = _dot_nt(q_ref[rs, hs], kh) * dec_ref[h]
            inner_ref[j, h] = _dot(s.astype(BF16), vh)
            kd = (kh.astype(F32) * kdec_ref[h]).astype(BF16)
            kv_ref[j, h] = _dot_tn(kd, vh)
    for j in range(nj):
        rs = slice(j * cs, (j + 1) * cs)
        for h in range(RET_HEADS):
            hs = slice(h * RET_HEAD_DIM, (h + 1) * RET_HEAD_DIM)
            st = st_ref[0, h]
            cross = _dot(q_ref[rs, hs], st.astype(BF16)) * rdec_ref[h]
            st_ref[0, h] = math.exp(cs * LOG_G[h]) * st + kv_ref[j, h]
            o = inner_ref[j, h] + cross
            mu = jnp.mean(o, axis=-1, keepdims=True)
            oc = o - mu
            var = jnp.mean(oc * oc, axis=-1, keepdims=True)
            on = oc * lax.rsqrt(var + LN_EPS)
            mix_ref[rs, hs] = (g_ref[rs, hs].astype(F32) * on).astype(mix_ref.dtype)


def _retention(q, k, v, g, batch, seq):
    t = q.shape[0]
    rows = RET_CHUNK * RET_CHUNKS_PER_STEP
    ns = seq // rows
    tok = lambda b, c: (b * ns + c, 0)
    sq = (RET_HEADS, RET_CHUNK, RET_CHUNK)
    return pl.pallas_call(
        _retention_kernel,
        grid=(batch, ns),
        in_specs=[pl.BlockSpec((rows, MAIN_WIDTH), tok)] * 4,
        out_specs=[pl.BlockSpec((rows, MAIN_WIDTH), tok),
                   pl.BlockSpec((1, RET_HEADS, RET_HEAD_DIM, RET_HEAD_DIM), lambda b, c: (b, 0, 0, 0))],
        out_shape=[jax.ShapeDtypeStruct((t, MAIN_WIDTH), BF16),
                   jax.ShapeDtypeStruct((batch, RET_HEADS, RET_HEAD_DIM, RET_HEAD_DIM), F32)],
        scratch_shapes=[pltpu.VMEM(sq, F32)] * 3 + [pltpu.VMEM((RET_CHUNKS_PER_STEP,) + sq, F32)] * 2,
        compiler_params=_params(("arbitrary", "arbitrary")),
        name="retention",
    )(q, k, v, g)


def _retention_sample_kernel(q_ref, k_ref, v_ref, g_ref, qm_ref, mkv_ref, st_ref, mix_ref, nst_ref, *, t):
    nb = q_ref.shape[0] // t
    pad_k = jnp.zeros((LANES - t, RET_HEAD_DIM), F32)
    pad_q = jnp.zeros((BF16_SUBLANES - t, RET_HEAD_DIM), F32)
    lhs = lambda x: jnp.concatenate([x, pad_q], axis=0).astype(BF16)
    row = lax.broadcasted_iota(jnp.int32, (t, LANES), 0).astype(F32)
    col = lax.broadcasted_iota(jnp.int32, (t, LANES), 1).astype(F32)
    prow = lax.broadcasted_iota(jnp.int32, (LANES, RET_HEAD_DIM), 0).astype(F32)
    diff = row - col
    for h in range(RET_HEADS):
        lg = LOG_G[h]
        hs = slice(h * RET_HEAD_DIM, (h + 1) * RET_HEAD_DIM)
        dec = jnp.where(diff >= 0, jnp.exp(jnp.maximum(diff, 0.0) * lg), 0.0)
        rdec = jnp.exp((row + 1.0) * lg)
        kdec = jnp.exp((t - 1.0 - prow) * lg)
        for bb in range(nb):
            rs = slice(bb * t, (bb + 1) * t)
            qh = lhs(q_ref[rs, hs])
            kp = jnp.concatenate([k_ref[rs, hs], pad_k], axis=0)
            vp = jnp.concatenate([v_ref[rs, hs], pad_k], axis=0).astype(BF16)
            st = st_ref[0, bb, h]
            inner = _dot(lhs(_dot_nt(qh, kp.astype(BF16))[:t] * dec), vp)[:t]
            cross = _dot(qh, st.astype(BF16))[:t] * rdec
            nst_ref[bb, h] = math.exp(t * lg) * st + _dot_tn((kp * kdec).astype(BF16), vp)
            o = inner + cross
            mu = jnp.mean(o, axis=-1, keepdims=True)
            oc = o - mu
            var = jnp.mean(oc * oc, axis=-1, keepdims=True)
            on = oc * lax.rsqrt(var + LN_EPS)
            mix_ref[rs, hs] = g_ref[rs, hs] * on
    for bb in range(nb):
        rs = slice(bb * t, (bb + 1) * t)
        mix_ref[rs, MAIN_WIDTH:] = _mem_attn_tile(qm_ref[rs, :], mkv_ref[0, bb])


def _retention_sample(q, k, v, g, qm, memkv_t, state, batch, t):
    nb = SAMPLE_BATCH_PER_STEP
    tok = lambda b: (b, 0)
    hd = (RET_HEADS, RET_HEAD_DIM, RET_HEAD_DIM)
    return pl.pallas_call(
        functools.partial(_retention_sample_kernel, t=t),
        grid=(batch // nb,),
        in_specs=[pl.BlockSpec((nb * t, MAIN_WIDTH), tok)] * 4
                 + [pl.BlockSpec((nb * t, MEM_WIDTH), tok),
                    pl.BlockSpec((1, nb) + memkv_t.shape[2:], lambda b: (0, b, 0, 0)),
                    pl.BlockSpec((1, nb) + hd, lambda b: (0, b, 0, 0, 0))],
        out_specs=[pl.BlockSpec((nb * t, D_MODEL), tok), pl.BlockSpec((nb,) + hd, lambda b: (b, 0, 0, 0))],
        out_shape=[jax.ShapeDtypeStruct((batch * t, D_MODEL), F32),
                   jax.ShapeDtypeStruct((batch,) + hd, F32)],
        compiler_params=_params(("parallel",)),
        name="retention_sample",
    )(q, k, v, g, qm, memkv_t, state)


def _dilated_sample_group(q, cache, new, window, dil):
    t = q.shape[0]
    n_buf = cache.shape[1]
    gw = GROUP_WIDTH
    q4 = _stack_heads(q, DIL_HEAD_DIM)
    rows = q4.shape[0]
    newp = jnp.concatenate([new, jnp.zeros((LANES - t, new.shape[1]), F32)], axis=0).astype(BF16)

    def masked(s, first_index):
        key = lax.broadcasted_iota(jnp.int32, s.shape, 1) + first_index
        tok = lax.broadcasted_iota(jnp.int32, s.shape, 0) & (t - 1)
        delta = n_buf + tok - key
        valid = (delta >= 0) & (delta <= window) & ((delta & (dil - 1)) == 0)
        return jnp.where(valid, s, NEG_BIG)

    s_c = masked(_dot(q4, cache[:gw].astype(BF16)), 0)
    s_n = masked(_dot_nt(q4, newp[:, :gw]), n_buf)
    m = jnp.maximum(jnp.max(s_c, axis=-1, keepdims=True), jnp.max(s_n, axis=-1, keepdims=True))
    e_c = jnp.exp(s_c - m)
    e_n = jnp.exp(s_n - m)
    l = jnp.sum(e_c, axis=-1, keepdims=True) + jnp.sum(e_n, axis=-1, keepdims=True)
    inv = 1.0 / l
    full = (_dot_nt((e_c * inv).astype(BF16), cache[gw:].astype(BF16))
            + _dot((e_n * inv).astype(BF16), newp[:, gw:]))
    o = _pick_heads(full, t, DIL_HEAD_DIM)
    lse = _pick_heads(jnp.broadcast_to(m + jnp.log(l), (rows, gw)), t, DIL_HEAD_DIM)
    return o, lse


def _sample_mixer_kernel(*refs, t):
    q_refs, cache_refs, new_refs, new_t_refs = refs[0:3], refs[3:6], refs[6:9], refs[9:12]
    qm_ref, mkv_ref = refs[12], refs[13]
    shifted_refs, mix_ref, win_refs = refs[14:17], refs[17], refs[18:21]
    gw = GROUP_WIDTH
    nb = mix_ref.shape[0] // t
    for bb in range(nb):
        rs = slice(bb * t, (bb + 1) * t)
        outs, lses = [], []
        col0 = (pl.program_id(0) * nb + bb) * t
        block = pl.multiple_of((col0 // LANES) * LANES, LANES)
        shift = LANES - t - col0 % LANES
        for g, (window, dil) in enumerate(DIL_PAIRS):
            o, lse = _dilated_sample_group(q_refs[g][rs, :], cache_refs[g][bb], new_refs[g][rs, :], window, dil)
            outs.append(o)
            lses.append(lse)
            win_refs[g][bb] = shifted_refs[g][bb]
            new_cols = pltpu.roll(new_t_refs[g][:, pl.ds(block, LANES)], shift, 1)
            win_refs[g][bb, :, LANES - t:] = new_cols[:, LANES - t:]
        m = jnp.maximum(jnp.maximum(lses[0], lses[1]), lses[2])
        es = [jnp.exp(v - m) for v in lses]
        inv = 1.0 / (es[0] + es[1] + es[2])
        for g in range(3):
            mix_ref[rs, g * gw:(g + 1) * gw] = outs[g] * (es[g] * inv)
        mix_ref[rs, MAIN_WIDTH:] = _mem_attn_tile(qm_ref[rs, :], mkv_ref[0, bb])


def _sample_mixer(q_g, caches_t, new_kv, new_kv_t, qm, memkv_t, shifted, batch, t):
    assert LANES % t == 0
    gw = GROUP_WIDTH
    nb = SAMPLE_BATCH_PER_STEP
    tok = lambda b: (b, 0)
    big = lambda b: (b, 0, 0)
    cache_specs = [pl.BlockSpec((nb,) + c.shape[1:], big) for c in caches_t]
    tail_specs = [pl.BlockSpec((nb, c.shape[1], LANES), lambda b, j=c.shape[2] // LANES - 1: (b, 0, j))
                  for c in shifted]
    first_shifted = 3 + len(caches_t) + 3 + 3 + 2
    return pl.pallas_call(
        functools.partial(_sample_mixer_kernel, t=t),
        grid=(batch // nb,),
        in_specs=[pl.BlockSpec((nb * t, gw), tok)] * 3 + cache_specs + [pl.BlockSpec((nb * t, 2 * gw), tok)] * 3
                 + [_full_spec(a) for a in new_kv_t]
                 + [pl.BlockSpec((nb * t, MEM_WIDTH), tok),
                    pl.BlockSpec((1, nb) + memkv_t.shape[2:], lambda b: (1, b, 0, 0))] + tail_specs,
        out_specs=[pl.BlockSpec((nb * t, D_MODEL), tok)] + tail_specs,
        out_shape=[jax.ShapeDtypeStruct((batch * t, D_MODEL), F32)]
                  + [jax.ShapeDtypeStruct(c.shape, F32) for c in shifted],
        input_output_aliases={first_shifted + g: 1 + g for g in range(len(shifted))},
        compiler_params=_params(("parallel",)),
        name="sample_mixer",
    )(*q_g, *caches_t, *new_kv, *new_kv_t, qm, memkv_t, *shifted)


def _natural_rows(ref, dil, scr):
    if dil == 1:
        return ref[0, 0].astype(F32)
    n = ref.shape[2]
    for r in range(dil):
        v = ref[0, r].astype(F32)
        for c in range(v.shape[1] // LANES):
            scr[c, pl.ds(r, n, stride=dil), :] = v[:, c * LANES:(c + 1) * LANES]
    return jnp.concatenate([scr[c] for c in range(scr.shape[0])], axis=1)


def _post_body(mix, x_ref, wo_ref, g1_ref, b1_ref, wi_ref, w2_ref, g2_ref, b2_ref, out_ref, act_ref):
    rows = x_ref.shape[0]
    x1 = _layer_norm(ALPHA * x_ref[...] + _dot(mix, wo_ref[0]), g1_ref[0], b1_ref[0])
    x1b = x1.astype(BF16)
    for c in range(0, FFN_HIDDEN, FFN_COL_CHUNK):
        gate = _dot(x1b, wi_ref[0, :, c:c + FFN_COL_CHUNK])
        up = _dot(x1b, wi_ref[0, :, FFN_HIDDEN + c:FFN_HIDDEN + c + FFN_COL_CHUNK])
        act_ref[0:rows, c:c + FFN_COL_CHUNK] = (_silu(gate) * up).astype(BF16)
    y = _dot(act_ref[0:rows, :], w2_ref[0])
    out_ref[...] = _layer_norm(ALPHA * x1 + y, g2_ref[0], b2_ref[0])


def _post_kernel(*refs, dils, n_mix_p, n_mix_s, n_p, n_shift, dec_seq, flip_blocks):
    n_flip = len(flip_blocks)
    mixp_refs, refs = refs[:n_mix_p], refs[n_mix_p:]
    xp_ref, refs = refs[0], refs[1:]
    mixs_refs, refs = refs[:n_mix_s], refs[n_mix_s:]
    xs_ref, refs = refs[0], refs[1:]
    weights, refs = refs[:7], refs[7:]
    caches, refs = refs[:n_shift], refs[n_shift:]
    flip_in, refs = refs[:n_flip], refs[n_flip:]
    outp_ref, outs_ref, refs = refs[0], refs[1], refs[2:]
    wins, refs = refs[:n_shift], refs[n_shift:]
    flip_out, refs = refs[:n_flip], refs[n_flip:]
    act_ref = refs[0]
    scrs = list(refs[1:])
    i = pl.program_id(0)

    @pl.when(i < n_p)
    def _():
        if dils is not None:
            o_refs, l_refs, mem_ref = mixp_refs[0:3], mixp_refs[3:6], mixp_refs[6]
            pool = list(scrs)
            nat = lambda r, d: _natural_rows(r, d, pool.pop(0) if d > 1 else None)
            os_ = [nat(o_refs[g], dils[g]) for g in range(3)]
            ls = [nat(l_refs[g], dils[g]) for g in range(3)]
            m = jnp.maximum(jnp.maximum(ls[0], ls[1]), ls[2])
            es = [jnp.exp(v - m) for v in ls]
            inv = 1.0 / (es[0] + es[1] + es[2])
            parts = [(os_[g] * (es[g] * inv)).astype(BF16) for g in range(3)]
            parts.append(mem_ref[...].astype(BF16))
        else:
            parts = [r[...].astype(BF16) for r in mixp_refs]
        mix = parts[0] if len(parts) == 1 else jnp.concatenate(parts, axis=1)
        _post_body(mix, xp_ref, *weights, outp_ref, act_ref)
        for c_ref, w_ref in zip(caches, wins):
            _shift_window(c_ref, w_ref, dec_seq)
        for src, dst, nblk in zip(flip_in, flip_out, flip_blocks):
            @pl.when(i < nblk)
            def _(src=src, dst=dst):
                dst[0] = jnp.transpose(src[...])

    @pl.when(i == n_p)
    def _():
        parts = [r[...].astype(BF16) for r in mixs_refs]
        mix = parts[0] if len(parts) == 1 else jnp.concatenate(parts, axis=1)
        _post_body(mix, xs_ref, *weights, outs_ref, act_ref)


def _post(layer, mix_p, xp, mix_s, xs, wo, g1, b1, wi, w2, g2, b2, dils, seq, caches_t=(), dec_seq=0,
          flips=()):
    tp, d = xp.shape
    tm = TOKEN_TILE
    n_p = tp // tm
    nt = seq // tm
    cl = lambda i: jnp.minimum(i, n_p - 1)
    row = lambda i: (cl(i), 0)
    if dils is None:
        mixp_specs = [pl.BlockSpec((tm, a.shape[1]), row) for a in mix_p]
        scratch = []
    else:
        res = lambda i: (cl(i) // nt, 0, cl(i) % nt, 0)
        mixp_specs = [pl.BlockSpec((1, dl, tm // dl, GROUP_WIDTH), res) for dl in dils] * 2
        mixp_specs.append(pl.BlockSpec((tm, MEM_WIDTH), row))
        scratch = [pltpu.VMEM((GROUP_WIDTH // LANES, tm, LANES), F32) for dl in dils * 2 if dl > 1]
    lsel = (layer, 0, 0)
    vec = _const_spec((1, 1, d), lsel)
    shift_specs = _shift_specs(caches_t, n_p)
    flip_in_specs, flip_out_specs, flip_shapes, flip_blocks = [], [], [], []
    for a, nbatch in flips:
        w = a.shape[0] // nbatch
        rb = min(w, tm)
        per = w // rb
        nblk = nbatch * per
        assert nblk <= n_p
        blk = lambda i, nblk=nblk: jnp.minimum(i, nblk - 1)
        flip_in_specs.append(pl.BlockSpec((rb, a.shape[1]), lambda i, blk=blk: (blk(i), 0)))
        flip_out_specs.append(pl.BlockSpec((1, a.shape[1], rb),
                                           lambda i, blk=blk, per=per: (blk(i) // per, 0, blk(i) % per)))
        flip_shapes.append(jax.ShapeDtypeStruct((nbatch, a.shape[1], w), F32))
        flip_blocks.append(nblk)
    return pl.pallas_call(
        functools.partial(_post_kernel, dils=dils, n_mix_p=len(mix_p), n_mix_s=len(mix_s), n_p=n_p,
                          n_shift=len(caches_t), dec_seq=dec_seq, flip_blocks=tuple(flip_blocks)),
        grid=(n_p + 1,),
        in_specs=mixp_specs + [pl.BlockSpec((tm, d), row)] + [_full_spec(a) for a in mix_s] + [_full_spec(xs)]
                 + [_const_spec((1,) + wo.shape[1:], lsel), vec, vec,
                    _const_spec((1,) + wi.shape[1:], lsel), _const_spec((1,) + w2.shape[1:], lsel), vec, vec]
                 + shift_specs + flip_in_specs,
        out_specs=[pl.BlockSpec((tm, d), row), pl.BlockSpec(xs.shape, lambda i: (0, 0))] + shift_specs
                  + flip_out_specs,
        out_shape=[jax.ShapeDtypeStruct((tp, d), F32), jax.ShapeDtypeStruct(xs.shape, F32)]
                  + [jax.ShapeDtypeStruct(c.shape, F32) for c in caches_t] + flip_shapes,
        scratch_shapes=[pltpu.VMEM((tm, FFN_HIDDEN), BF16)] + scratch,
        compiler_params=_params(("arbitrary",)),
        name="post" if dils is None else "post_combine",
    )(*mix_p, xp, *mix_s, xs, wo, g1, b1, wi, w2, g2, b2, *caches_t, *[a for a, _ in flips])


def _residue_major(ref, y, dil, scr):
    if dil == 1:
        ref[...] = y.astype(ref.dtype).reshape(ref.shape)
        return
    n = ref.shape[2]
    ncol = y.shape[1] // LANES
    for c in range(ncol):
        scr[c] = y[:, c * LANES:(c + 1) * LANES]
    for r in range(dil):
        rows = [scr[c, pl.ds(r, n, stride=dil), :] for c in range(ncol)]
        ref[0, r] = jnp.concatenate(rows, axis=1).astype(ref.dtype)


def _in_b_body(x_ref, wkv_ref, wq_ref, cos_ref, sin_ref, mkv_ref,
               q_refs, kv_refs, m_ref, win_refs, scrs, dils, win_preds):
    xb = x_ref[...].astype(BF16)
    cos = cos_ref[...]
    sin = sin_ref[...]
    gw = GROUP_WIDTH
    mw = MAIN_WIDTH
    half = DIL_HEAD_DIM // 2
    tm = x_ref.shape[0]
    pool = list(scrs)
    for g in range(3):
        ks = slice(g * gw, (g + 1) * gw)
        vs = slice(mw + g * gw, mw + (g + 1) * gw)
        k = _rope_cols(_dot(xb, wkv_ref[:, ks]), cos, sin, half)
        v = _dot(xb, wkv_ref[:, vs])
        kv = jnp.concatenate([k, v], axis=1)
        _residue_major(kv_refs[g], kv, dils[g], pool.pop(0) if dils[g] > 1 else None)
        q = _rope_cols(_dot(xb, wq_ref[0, :, ks]), cos, sin, half) * ATTN_SCALE
        _residue_major(q_refs[g], q, dils[g], pool.pop(0) if dils[g] > 1 else None)

        def window(g=g, kv=kv):
            if len(win_refs[g].shape) == 2:
                win_refs[g][...] = kv[tm - win_refs[g].shape[0]:, :]
            else:
                win_refs[g][0] = jnp.transpose(kv)

        if win_preds[g] is None:
            window()
        else:
            pl.when(win_preds[g])(window)
    qm = _dot(xb, wq_ref[0, :, mw:mw + MEM_WIDTH]) * ATTN_SCALE
    if mkv_ref is not None:
        m_ref[...] = _mem_attn_tile(qm.astype(BF16), mkv_ref[0, 0]).astype(m_ref.dtype)
    else:
        m_ref[...] = qm.astype(m_ref.dtype)


def _in_b_kernel(xp_ref, xs_ref, wkv_ref, wq_ref, cosp_ref, sinp_ref, coss_ref, sins_ref, mkv_ref, *refs,
                 dils, win_tiles, nt, n_p):
    outp, outs, scrs = refs[0:10], refs[10:20], refs[20:]
    i = pl.program_id(0)

    @pl.when(i < n_p)
    def _():
        j = i % nt
        _in_b_body(xp_ref, wkv_ref, wq_ref, cosp_ref, sinp_ref, mkv_ref,
                   outp[0:3], outp[3:6], outp[6], outp[7:10], scrs, dils,
                   [j >= nt - w for w in win_tiles])

    @pl.when(i == n_p)
    def _():
        _in_b_body(xs_ref, wkv_ref, wq_ref, coss_ref, sins_ref, None,
                   outs[0:3], outs[3:6], outs[6], outs[7:10], (), (1, 1, 1), [None] * 3)


def _in_b(xp, xs, w_kv, w_q, tabs_p, tabs_s, memkv_t, batch, seq, dils, windows):
    tp, d = xp.shape
    ts = xs.shape[0]
    gw = GROUP_WIDTH
    tm = TOKEN_TILE
    n_p = tp // tm
    nt = seq // tm
    cl = lambda i: jnp.minimum(i, n_p - 1)
    row = lambda i: (cl(i), 0)
    tab = lambda i: (cl(i) % nt, 0)
    res = lambda i: (cl(i) // nt, 0, cl(i) % nt, 0)
    fix2 = lambda i: (0, 0)
    out_specs = ([pl.BlockSpec((1, dl, tm // dl, gw), res) for dl in dils]
                 + [pl.BlockSpec((1, dl, tm // dl, 2 * gw), res) for dl in dils]
                 + [pl.BlockSpec((tm, MEM_WIDTH), row)])
    out_shape = ([jax.ShapeDtypeStruct((batch, dl, seq // dl, gw), BF16) for dl in dils]
                 + [jax.ShapeDtypeStruct((batch, dl, seq // dl, 2 * gw), BF16) for dl in dils]
                 + [jax.ShapeDtypeStruct((tp, MEM_WIDTH), BF16)])
    win_tiles = []
    for w in windows:
        rb = min(w, tm)
        nblk = w // rb
        win_tiles.append(nblk)
        out_specs.append(pl.BlockSpec(
            (rb, 2 * gw), lambda i, nblk=nblk: ((cl(i) // nt) * nblk + jnp.maximum(cl(i) % nt - (nt - nblk), 0), 0)))
        out_shape.append(jax.ShapeDtypeStruct((batch * w, 2 * gw), F32))
    out_specs += ([pl.BlockSpec((ts, gw), fix2)] * 3 + [pl.BlockSpec((ts, 2 * gw), fix2)] * 3
                  + [pl.BlockSpec((ts, MEM_WIDTH), fix2)] + [pl.BlockSpec((1, 2 * gw, ts), lambda i: (0, 0, 0))] * 3)
    out_shape += ([jax.ShapeDtypeStruct((ts, gw), F32)] * 3 + [jax.ShapeDtypeStruct((ts, 2 * gw), F32)] * 3
                  + [jax.ShapeDtypeStruct((ts, MEM_WIDTH), F32)] + [jax.ShapeDtypeStruct((1, 2 * gw, ts), F32)] * 3)
    scratch = []
    for dl in dils:
        if dl > 1:
            scratch += [pltpu.VMEM((2 * gw // LANES, tm, LANES), F32), pltpu.VMEM((gw // LANES, tm, LANES), F32)]
    return pl.pallas_call(
        functools.partial(_in_b_kernel, dils=tuple(dils), win_tiles=tuple(win_tiles), nt=nt, n_p=n_p),
        grid=(n_p + 1,),
        in_specs=[pl.BlockSpec((tm, d), row), _full_spec(xs), _const_spec(w_kv.shape, (0, 0)),
                  _const_spec((1,) + w_q.shape[1:], (0, 0, 0)),
                  pl.BlockSpec((tm, LANES), tab), pl.BlockSpec((tm, LANES), tab),
                  _full_spec(tabs_s[0]), _full_spec(tabs_s[1]),
                  pl.BlockSpec((1, 1) + memkv_t.shape[2:], lambda i: (1, cl(i) // nt, 0, 0))],
        out_specs=out_specs,
        out_shape=out_shape,
        scratch_shapes=scratch,
        compiler_params=_params(("arbitrary",)),
        name="in_proj_b",
    )(xp, xs, w_kv, w_q, *tabs_p, *tabs_s, memkv_t)


def _dilated_kernel(q_ref, kv_ref, o_ref, lse_ref, ring_ref):
    i = pl.program_id(2)
    blk = DIL_BLOCK
    gw = GROUP_WIDTH
    nres = q_ref.shape[1]
    nsub = q_ref.shape[2] // blk
    slot = i & 1

    @pl.when(i == 0)
    def _():
        for r in range(nres):
            ring_ref[r, 1] = jnp.zeros(ring_ref.shape[2:], ring_ref.dtype)

    rows = GROUP_HEADS * blk
    qi = lax.broadcasted_iota(jnp.int32, (rows, 2 * blk), 0) & (blk - 1)
    kj = lax.broadcasted_iota(jnp.int32, (rows, 2 * blk), 1)
    delta = qi + blk - kj
    band = (delta >= 0) & (delta <= blk)
    for r in range(nres):
        cur = kv_ref[0, r]
        ring_ref[r, slot] = cur[(nsub - 1) * blk:]
        kext = jnp.concatenate([ring_ref[r, 1 - slot], cur], axis=0)
        for j in range(nsub):
            rs = slice(j * blk, (j + 1) * blk)
            q4 = _stack_heads(q_ref[0, r, rs, :], DIL_HEAD_DIM)
            kv = kext[j * blk:(j + 2) * blk]
            valid = band if j > 0 else band & ((i > 0) | (kj >= blk))
            s = jnp.where(valid, _dot_nt(q4, kv[:, :gw]), NEG_BIG)
            m = jnp.max(s, axis=-1, keepdims=True)
            e = jnp.exp(s - m)
            l = jnp.sum(e, axis=-1, keepdims=True)
            p = (e * (1.0 / l)).astype(BF16)
            full = _dot(p, kv[:, gw:])
            o_ref[0, r, rs, :] = _pick_heads(full, blk, DIL_HEAD_DIM).astype(o_ref.dtype)
            lse_ref[0, r, rs, :] = _pick_heads(jnp.broadcast_to(m + jnp.log(l), (rows, gw)), blk, DIL_HEAD_DIM)


def _dilated(q, kv, dil):
    batch, _, m, gw = q.shape
    nsub = min(DIL_BLOCKS_PER_STEP, m // DIL_BLOCK)
    nres = min(dil, DIL_BLOCKS_PER_STEP // nsub)
    rows = DIL_BLOCK * nsub
    idx = lambda b, r, i: (b, r, i, 0)
    return pl.pallas_call(
        _dilated_kernel,
        grid=(batch, dil // nres, m // rows),
        in_specs=[pl.BlockSpec((1, nres, rows, gw), idx), pl.BlockSpec((1, nres, rows, 2 * gw), idx)],
        out_specs=[pl.BlockSpec((1, nres, rows, gw), idx)] * 2,
        out_shape=[jax.ShapeDtypeStruct(q.shape, BF16), jax.ShapeDtypeStruct(q.shape, F32)],
        scratch_shapes=[pltpu.VMEM((nres, 2, DIL_BLOCK, 2 * gw), BF16)],
        compiler_params=_params(("arbitrary", "arbitrary", "arbitrary")),
        name=f"dilated_d{dil}",
    )(q, kv)


def _rope_tables(base, n):
    split = min(ROPE_SPLIT, n)
    hi = base + split * jnp.arange(n // split, dtype=F32)
    lo = jnp.arange(split, dtype=F32)
    inv = ROPE_THETA ** (-jnp.arange(0, RET_HEAD_DIM, 2, dtype=F32) / RET_HEAD_DIM)
    inv_h = inv[0::2]
    neg = lambda m: jnp.concatenate([-jnp.ones((m,), F32), jnp.ones((m,), F32)])

    def token_major(freq, sign):
        xa, xb = hi[:, None] * freq[None, :], lo[:, None] * freq[None, :]
        ca, sa, cb, sb = jnp.cos(xa), jnp.sin(xa), jnp.cos(xb), jnp.sin(xb)
        cos = ca[:, None, :] * cb[None, :, :] - sa[:, None, :] * sb[None, :, :]
        sin = (sa[:, None, :] * cb[None, :, :] + ca[:, None, :] * sb[None, :, :]) * sign
        return cos.reshape(n, -1), sin.reshape(n, -1)

    f_big = jnp.concatenate([inv, inv])
    f_half = jnp.concatenate([inv_h, inv_h])
    return (token_major(f_big, neg(RET_HEAD_DIM // 2)),
            token_major(jnp.tile(f_half, 2), jnp.tile(neg(DIL_HEAD_DIM // 2), 2)))


def _to_feature_major(x5):
    b, w = x5.shape[0], x5.shape[1]
    return jnp.transpose(x5, (0, 2, 3, 4, 1)).reshape(b, 2 * GROUP_WIDTH, w)


def _from_feature_major(xt):
    b, _, w = xt.shape
    return jnp.transpose(xt.reshape(b, 2, GROUP_HEADS, DIL_HEAD_DIM, w), (0, 4, 1, 2, 3))


def kernel(x_prompt, x_sample, mem_prompt, cache_mem_kv, state_ret, cache_win_kv_g1, cache_win_kv_g2, cache_win_kv_g3, w_in_a, w_in_b, w_out, w_kv_shared, w_mem_kv, ln_mix_g, ln_mix_b, ln_ffn_g, ln_ffn_b, w_ffn_in, w_ffn_out):
    batch, seq, d = x_prompt.shape
    dec_batch, dec_seq, _ = x_sample.shape
    n_mem = mem_prompt.shape[1]
    gw = GROUP_WIDTH
    ts = dec_batch * dec_seq
    win_caches = (cache_win_kv_g1, cache_win_kv_g2, cache_win_kv_g3)
    dils = tuple(dl for _, dl in DIL_PAIRS)

    tab_a_p, tab_b_p = _rope_tables(0.0, seq)
    tab_a_s, tab_b_s = _rope_tables(float(PAST_LEN), dec_seq)
    tab_a_s = tuple(jnp.tile(a, (dec_batch, 1)) for a in tab_a_s)
    tab_b_s = tuple(jnp.tile(a, (dec_batch, 1)) for a in tab_b_s)

    w_a = w_in_a.astype(BF16)
    w_kv = w_kv_shared.astype(BF16)
    w_q = w_in_b.astype(BF16)
    w_o = w_out.astype(BF16)
    w_fi = w_ffn_in.astype(BF16)
    w_fo = w_ffn_out.astype(BF16)
    w_mem_t = jnp.transpose(w_mem_kv, (0, 2, 1)).astype(BF16)
    lnv = lambda a: a.reshape(DEPTH, 1, d)
    ln = (lnv(ln_mix_g), lnv(ln_mix_b), lnv(ln_ffn_g), lnv(ln_ffn_b))

    memkv_p = _mem_proj(mem_prompt, w_mem_t)
    memkv_s = jnp.transpose(cache_mem_kv, (0, 1, 3, 4, 5, 2)).reshape(DEPTH, dec_batch, 2 * MEM_WIDTH, n_mem)

    xp = x_prompt.reshape(batch * seq, d)
    xs = x_sample.reshape(ts, d)

    def post(l, mix_p, mix_s, dls, shift=(), flips=()):
        return _post(l, mix_p, xp, mix_s, xs, w_o, ln[0], ln[1], w_fi, w_fo, ln[2], ln[3], dls, seq,
                     shift, dec_seq, flips)

    caches_t = [_to_feature_major(c) for c in win_caches]

    outs = _in_a(xp, xs, w_a, tab_a_p, tab_a_s, memkv_p, seq, caches_t[:2], dec_seq)
    q, k, v, g, mem_o = outs[:5]
    mix_p, state_p = _retention(q, k, v, g, batch, seq)
    q, k, v, g, qm = outs[5:10]
    shifted = list(outs[10:])
    mix_s, state_s = _retention_sample(q, k, v, g, qm, memkv_s, state_ret, dec_batch, dec_seq)
    xp, xs, *shifted_big = post(0, [mix_p, mem_o], [mix_s], None, caches_t[2:])
    shifted += shifted_big

    windows_p = tuple(min(w, seq) for w, _ in DIL_PAIRS)
    outs = _in_b(xp, xs, w_kv, w_q, tab_b_p, tab_b_s, memkv_p, batch, seq, dils, windows_p)
    q_g, kv_g, mem_o, win_p = outs[0:3], outs[3:6], outs[6], outs[7:10]
    att = [_dilated(q_g[i], kv_g[i], dils[i]) for i in range(3)]

    q_s, kv_new, qm, new_t = outs[10:13], outs[13:16], outs[16], outs[17:20]
    new_t = [a.reshape(2 * gw, ts) for a in new_t]
    mix_s, *win_s = _sample_mixer(q_s, caches_t, kv_new, new_t, qm, memkv_s, shifted, dec_batch, dec_seq)

    xp, xs, *win_p = post(1, [a[0] for a in att] + [a[1] for a in att] + [mem_o], [mix_s], dils,
                          flips=[(w, batch) for w in win_p])

    memkv_out = jnp.transpose(memkv_p.reshape(DEPTH, batch, 2, MEM_HEADS, MEM_HEAD_DIM, n_mem), (0, 1, 5, 2, 3, 4))
    return (xp.reshape(batch, seq, d), xs.reshape(dec_batch, dec_seq, d),
            state_p[None], state_s[None], memkv_out,
            _from_feature_major(win_p[0]), _from_feature_major(win_p[1]), _from_feature_major(win_p[2]),
            _from_feature_major(win_s[0]), _from_feature_major(win_s[1]), _from_feature_major(win_s[2]))
```

```python
import functools
import math

import jax
import jax.numpy as jnp
from jax import lax
from jax.experimental import pallas as pl
from jax.experimental.pallas import tpu as pltpu

F32 = jnp.float32
BF16 = jnp.bfloat16

D_MODEL = 1024
MEM_HEADS = 4
MEM_HEAD_DIM = 64
MEM_WIDTH = MEM_HEADS * MEM_HEAD_DIM
MAIN_WIDTH = D_MODEL - MEM_WIDTH
RET_HEADS = 6
RET_HEAD_DIM = MAIN_WIDTH // RET_HEADS
RET_CHUNK = 128
RET_CHUNKS_PER_STEP = 8
DIL_PAIRS = ((128, 1), (512, 4), (2048, 16))
GROUP_HEADS = 4
DIL_HEAD_DIM = 64
GROUP_WIDTH = GROUP_HEADS * DIL_HEAD_DIM
DIL_BLOCK = 128
DIL_BLOCKS_PER_STEP = 8
FFN_HIDDEN = 2816
ROPE_THETA = 10000.0
ROPE_SPLIT = 64
LN_EPS = 1e-5
DEPTH = 2
ALPHA = (2 * DEPTH) ** 0.25
PAST_LEN = 8192
NEG_BIG = -1e30
ATTN_SCALE = DIL_HEAD_DIM ** -0.5

LANES = 128
BF16_SUBLANES = 16
SAMPLE_BATCH_PER_STEP = 2
VMEM_LIMIT = 56 * 1024 * 1024
FFN_COL_CHUNK = 256
TOKEN_TILE = 512

LOG_G = tuple(math.log1p(-(2.0 ** (-5.0 - h))) for h in range(RET_HEADS))


def _dot(a, b):
    return jnp.dot(a, b, preferred_element_type=F32)


def _dot_nt(a, b):
    return lax.dot_general(a, b, (((1,), (1,)), ((), ())), preferred_element_type=F32)


def _dot_tn(a, b):
    return lax.dot_general(a, b, (((0,), (0,)), ((), ())), preferred_element_type=F32)


def _silu(x):
    return x / (1.0 + jnp.exp(-x))


def _layer_norm(z, g, b):
    mu = jnp.mean(z, axis=-1, keepdims=True)
    zc = z - mu
    var = jnp.mean(zc * zc, axis=-1, keepdims=True)
    return zc * lax.rsqrt(var + LN_EPS) * g + b


def _rope_lanes(y, cos, sin_signed, half):
    if 2 * half == LANES:
        partner = pltpu.roll(y, half, 1)
    else:
        lane = lax.broadcasted_iota(jnp.int32, y.shape, 1)
        first = (lane & (2 * half - 1)) < half
        partner = jnp.where(first, pltpu.roll(y, LANES - half, 1), pltpu.roll(y, half, 1))
    return y * cos + partner * sin_signed


def _rope_cols(y, cos, sin_signed, half):
    parts = [_rope_lanes(y[:, j:j + LANES], cos, sin_signed, half) for j in range(0, y.shape[1], LANES)]
    return parts[0] if len(parts) == 1 else jnp.concatenate(parts, axis=1)


def _head_mask(shape, h, width):
    lane = lax.broadcasted_iota(jnp.int32, shape, len(shape) - 1)
    return (lane >= h * width) & (lane < (h + 1) * width)


def _stack_heads(q, width):
    zero = jnp.zeros((), q.dtype)
    parts = [jnp.where(_head_mask(q.shape, h, width), q, zero) for h in range(q.shape[1] // width)]
    return jnp.concatenate(parts, axis=0).astype(BF16)


def _pick_heads(full, t, width):
    per = LANES // width
    cols = []
    for c in range(full.shape[1] // LANES):
        out = None
        for i in range(per):
            h = c * per + i
            blk = full[h * t:(h + 1) * t, c * LANES:(c + 1) * LANES]
            out = blk if out is None else jnp.where(_head_mask(blk.shape, i, width), blk, out)
        cols.append(out)
    return cols[0] if len(cols) == 1 else jnp.concatenate(cols, axis=1)


def _const_spec(block, index):
    return pl.BlockSpec(block, lambda *_: index, pipeline_mode=pl.Buffered(1))


def _full_spec(a):
    return pl.BlockSpec(a.shape, lambda *_: (0,) * a.ndim)


def _params(sem):
    return pltpu.CompilerParams(dimension_semantics=sem, vmem_limit_bytes=VMEM_LIMIT)


def _mem_proj_kernel(m_ref, w_ref, o_ref):
    o_ref[0, 0] = _dot_nt(w_ref[0], m_ref[0].astype(BF16))


def _mem_proj(mem, w_t):
    batch, n_mem, d = mem.shape
    depth, n, _ = w_t.shape
    return pl.pallas_call(
        _mem_proj_kernel,
        grid=(depth, batch),
        in_specs=[pl.BlockSpec((1, n_mem, d), lambda l, b: (b, 0, 0)),
                  pl.BlockSpec((1, n, d), lambda l, b: (l, 0, 0))],
        out_specs=pl.BlockSpec((1, 1, n, n_mem), lambda l, b: (l, b, 0, 0)),
        out_shape=jax.ShapeDtypeStruct((depth, batch, n, n_mem), F32),
        compiler_params=_params(("arbitrary", "arbitrary")),
        name="mem_proj",
    )(mem, w_t)


def _mem_attn_tile(qm, kv_t):
    t = qm.shape[0]
    q4 = _stack_heads(qm, MEM_HEAD_DIM)
    k_t = kv_t[:MEM_WIDTH].astype(BF16)
    v_t = kv_t[MEM_WIDTH:].astype(BF16)
    s = _dot(q4, k_t)
    m = jnp.max(s, axis=-1, keepdims=True)
    e = jnp.exp(s - m)
    p = (e / jnp.sum(e, axis=-1, keepdims=True)).astype(BF16)
    return _pick_heads(_dot_nt(p, v_t), t, MEM_HEAD_DIM)


def _in_a_body(x_ref, w_ref, cos_ref, sin_ref, mkv_ref, q_ref, k_ref, v_ref, g_ref, m_ref):
    xb = x_ref[...].astype(BF16)
    cos = cos_ref[...]
    sin = sin_ref[...]
    mw = MAIN_WIDTH
    half = RET_HEAD_DIM // 2
    q = _rope_cols(_dot(xb, w_ref[0, :, 0:mw]), cos, sin, half)
    q_ref[...] = q.astype(q_ref.dtype)
    k = _rope_cols(_dot(xb, w_ref[0, :, mw:2 * mw]), cos, sin, half) * (RET_HEAD_DIM ** -0.5)
    k_ref[...] = k.astype(k_ref.dtype)
    v_ref[...] = _dot(xb, w_ref[0, :, 2 * mw:3 * mw]).astype(v_ref.dtype)
    g_ref[...] = _silu(_dot(xb, w_ref[0, :, 3 * mw:4 * mw])).astype(g_ref.dtype)
    qm = _dot(xb, w_ref[0, :, 4 * mw:4 * mw + MEM_WIDTH]) * ATTN_SCALE
    if mkv_ref is not None:
        m_ref[...] = _mem_attn_tile(qm.astype(BF16), mkv_ref[0, 0]).astype(m_ref.dtype)
    else:
        m_ref[...] = qm.astype(m_ref.dtype)


def _shift_window(cache_ref, win_ref, t):
    n = cache_ref.shape[2]
    win_ref[0, :, 0:n - t] = cache_ref[0, :, t:n]
    win_ref[0, :, n - t:n] = jnp.zeros((cache_ref.shape[1], t), F32)


def _shift_specs(caches_t, n_steps):
    last = caches_t[0].shape[0] - 1 if caches_t else 0
    assert not caches_t or n_steps > last
    return [pl.BlockSpec((1,) + c.shape[1:], lambda i: (jnp.minimum(i, last), 0, 0)) for c in caches_t]


def _in_a_kernel(xp_ref, xs_ref, w_ref, cosp_ref, sinp_ref, coss_ref, sins_ref, mkv_ref, *refs,
                 n_p, n_shift, dec_seq):
    caches, outs, wins = refs[:n_shift], refs[n_shift:n_shift + 10], refs[n_shift + 10:]
    i = pl.program_id(0)

    @pl.when(i < n_p)
    def _():
        _in_a_body(xp_ref, w_ref, cosp_ref, sinp_ref, mkv_ref, *outs[:5])
        for c_ref, w_ref_ in zip(caches, wins):
            _shift_window(c_ref, w_ref_, dec_seq)

    @pl.when(i == n_p)
    def _():
        _in_a_body(xs_ref, w_ref, coss_ref, sins_ref, None, *outs[5:])


def _in_a(xp, xs, w_bf, tab_p, tab_s, memkv_t, seq, caches_t, dec_seq):
    tp, d = xp.shape
    ts = xs.shape[0]
    tm = TOKEN_TILE
    n_p = tp // tm
    nt = seq // tm
    cl = lambda i: jnp.minimum(i, n_p - 1)
    row = lambda i: (cl(i), 0)
    tab = lambda i: (cl(i) % nt, 0)
    widths = (MAIN_WIDTH,) * 4 + (MEM_WIDTH,)
    shift_specs = _shift_specs(caches_t, n_p)
    return pl.pallas_call(
        functools.partial(_in_a_kernel, n_p=n_p, n_shift=len(caches_t), dec_seq=dec_seq),
        grid=(n_p + 1,),
        in_specs=[pl.BlockSpec((tm, d), row), _full_spec(xs), _const_spec((1,) + w_bf.shape[1:], (0, 0, 0)),
                  pl.BlockSpec((tm, LANES), tab), pl.BlockSpec((tm, LANES), tab),
                  _full_spec(tab_s[0]), _full_spec(tab_s[1]),
                  pl.BlockSpec((1, 1) + memkv_t.shape[2:], lambda i: (0, cl(i) // nt, 0, 0))] + shift_specs,
        out_specs=[pl.BlockSpec((tm, w), row) for w in widths]
                  + [pl.BlockSpec((ts, w), lambda i: (0, 0)) for w in widths] + shift_specs,
        out_shape=[jax.ShapeDtypeStruct((tp, w), BF16) for w in widths]
                  + [jax.ShapeDtypeStruct((ts, w), F32) for w in widths]
                  + [jax.ShapeDtypeStruct(c.shape, F32) for c in caches_t],
        compiler_params=_params(("arbitrary",)),
        name="in_proj_a",
    )(xp, xs, w_bf, *tab_p, *tab_s, memkv_t, *caches_t)


def _retention_kernel(q_ref, k_ref, v_ref, g_ref, mix_ref, st_ref, dec_ref, rdec_ref, kdec_ref,
                      inner_ref, kv_ref):
    c = pl.program_id(1)
    cs = RET_CHUNK

    @pl.when(c == 0)
    def _():
        st_ref[...] = jnp.zeros_like(st_ref)
        row = lax.broadcasted_iota(jnp.int32, (cs, cs), 0).astype(F32)
        col = lax.broadcasted_iota(jnp.int32, (cs, cs), 1).astype(F32)
        diff = row - col
        for h in range(RET_HEADS):
            lg = LOG_G[h]
            dec_ref[h] = jnp.where(diff >= 0, jnp.exp(jnp.maximum(diff, 0.0) * lg), 0.0)
            rdec_ref[h] = jnp.exp((row + 1.0) * lg)
            kdec_ref[h] = jnp.exp((cs - 1.0 - row) * lg)

    nj = q_ref.shape[0] // cs
    for j in range(nj):
        rs = slice(j * cs, (j + 1) * cs)
        for h in range(RET_HEADS):
            hs = slice(h * RET_HEAD_DIM, (h + 1) * RET_HEAD_DIM)
            kh = k_ref[rs, hs]
            vh = v_ref[rs, hs]
            s = _dot_nt(q_ref[rs, hs], kh) * dec_ref[h]
            inner_ref[j, h] = _dot(s.astype(BF16), vh)
            kd = (kh.astype(F32) * kdec_ref[h]).astype(BF16)
            kv_ref[j, h] = _dot_tn(kd, vh)
    for j in range(nj):
        rs = slice(j * cs, (j + 1) * cs)
        for h in range(RET_HEADS):
            hs = slice(h * RET_HEAD_DIM, (h + 1) * RET_HEAD_DIM)
            st = st_ref[0, h]
            cross = _dot(q_ref[rs, hs], st.astype(BF16)) * rdec_ref[h]
            st_ref[0, h] = math.exp(cs * LOG_G[h]) * st + kv_ref[j, h]
            o = inner_ref[j, h] + cross
            mu = jnp.mean(o, axis=-1, keepdims=True)
            oc = o - mu
            var = jnp.mean(oc * oc, axis=-1, keepdims=True)
            on = oc * lax.rsqrt(var + LN_EPS)
            mix_ref[rs, hs] = (g_ref[rs, hs].astype(F32) * on).astype(mix_ref.dtype)


def _retention(q, k, v, g, batch, seq):
    t = q.shape[0]
    rows = RET_CHUNK * RET_CHUNKS_PER_STEP
    ns = seq // rows
    tok = lambda b, c: (b * ns + c, 0)
    sq = (RET_HEADS, RET_CHUNK, RET_CHUNK)
    return pl.pallas_call(
        _retention_kernel,
        grid=(batch, ns),
        in_specs=[pl.BlockSpec((rows, MAIN_WIDTH), tok)] * 4,
        out_specs=[pl.BlockSpec((rows, MAIN_WIDTH), tok),
                   pl.BlockSpec((1, RET_HEADS, RET_HEAD_DIM, RET_HEAD_DIM), lambda b, c: (b, 0, 0, 0))],
        out_shape=[jax.ShapeDtypeStruct((t, MAIN_WIDTH), BF16),
                   jax.ShapeDtypeStruct((batch, RET_HEADS, RET_HEAD_DIM, RET_HEAD_DIM), F32)],
        scratch_shapes=[pltpu.VMEM(sq, F32)] * 3 + [pltpu.VMEM((RET_CHUNKS_PER_STEP,) + sq, F32)] * 2,
        compiler_params=_params(("arbitrary", "arbitrary")),
        name="retention",
    )(q, k, v, g)


def _retention_sample_kernel(q_ref, k_ref, v_ref, g_ref, qm_ref, mkv_ref, st_ref, mix_ref, nst_ref, *, t):
    nb = q_ref.shape[0] // t
    pad_k = jnp.zeros((LANES - t, RET_HEAD_DIM), F32)
    pad_q = jnp.zeros((BF16_SUBLANES - t, RET_HEAD_DIM), F32)
    lhs = lambda x: jnp.concatenate([x, pad_q], axis=0).astype(BF16)
    row = lax.broadcasted_iota(jnp.int32, (t, LANES), 0).astype(F32)
    col = lax.broadcasted_iota(jnp.int32, (t, LANES), 1).astype(F32)
    prow = lax.broadcasted_iota(jnp.int32, (LANES, RET_HEAD_DIM), 0).astype(F32)
    diff = row - col
    for h in range(RET_HEADS):
        lg = LOG_G[h]
        hs = slice(h * RET_HEAD_DIM, (h + 1) * RET_HEAD_DIM)
        dec = jnp.where(diff >= 0, jnp.exp(jnp.maximum(diff, 0.0) * lg), 0.0)
        rdec = jnp.exp((row + 1.0) * lg)
        kdec = jnp.exp((t - 1.0 - prow) * lg)
        for bb in range(nb):
            rs = slice(bb * t, (bb + 1) * t)
            qh = lhs(q_ref[rs, hs])
            kp = jnp.concatenate([k_ref[rs, hs], pad_k], axis=0)
            vp = jnp.concatenate([v_ref[rs, hs], pad_k], axis=0).astype(BF16)
            st = st_ref[0, bb, h]
            inner = _dot(lhs(_dot_nt(qh, kp.astype(BF16))[:t] * dec), vp)[:t]
            cross = _dot(qh, st.astype(BF16))[:t] * rdec
            nst_ref[bb, h] = math.exp(t * lg) * st + _dot_tn((kp * kdec).astype(BF16), vp)
            o = inner + cross
            mu = jnp.mean(o, axis=-1, keepdims=True)
            oc = o - mu
            var = jnp.mean(oc * oc, axis=-1, keepdims=True)
            on = oc * lax.rsqrt(var + LN_EPS)
            mix_ref[rs, hs] = g_ref[rs, hs] * on
    for bb in range(nb):
        rs = slice(bb * t, (bb + 1) * t)
        mix_ref[rs, MAIN_WIDTH:] = _mem_attn_tile(qm_ref[rs, :], mkv_ref[0, bb])


def _retention_sample(q, k, v, g, qm, memkv_t, state, batch, t):
    nb = SAMPLE_BATCH_PER_STEP
    tok = lambda b: (b, 0)
    hd = (RET_HEADS, RET_HEAD_DIM, RET_HEAD_DIM)
    return pl.pallas_call(
        functools.partial(_retention_sample_kernel, t=t),
        grid=(batch // nb,),
        in_specs=[pl.BlockSpec((nb * t, MAIN_WIDTH), tok)] * 4
                 + [pl.BlockSpec((nb * t, MEM_WIDTH), tok),
                    pl.BlockSpec((1, nb) + memkv_t.shape[2:], lambda b: (0, b, 0, 0)),
                    pl.BlockSpec((1, nb) + hd, lambda b: (0, b, 0, 0, 0))],
        out_specs=[pl.BlockSpec((nb * t, D_MODEL), tok), pl.BlockSpec((nb,) + hd, lambda b: (b, 0, 0, 0))],
        out_shape=[jax.ShapeDtypeStruct((batch * t, D_MODEL), F32),
                   jax.ShapeDtypeStruct((batch,) + hd, F32)],
        compiler_params=_params(("parallel",)),
        name="retention_sample",
    )(q, k, v, g, qm, memkv_t, state)


def _dilated_sample_group(q, cache, new, window, dil):
    t = q.shape[0]
    n_buf = cache.shape[1]
    gw = GROUP_WIDTH
    q4 = _stack_heads(q, DIL_HEAD_DIM)
    rows = q4.shape[0]
    newp = jnp.concatenate([new, jnp.zeros((LANES - t, new.shape[1]), F32)], axis=0).astype(BF16)

    def masked(s, first_index):
        key = lax.broadcasted_iota(jnp.int32, s.shape, 1) + first_index
        tok = lax.broadcasted_iota(jnp.int32, s.shape, 0) & (t - 1)
        delta = n_buf + tok - key
        valid = (delta >= 0) & (delta <= window) & ((delta & (dil - 1)) == 0)
        return jnp.where(valid, s, NEG_BIG)

    s_c = masked(_dot(q4, cache[:gw].astype(BF16)), 0)
    s_n = masked(_dot_nt(q4, newp[:, :gw]), n_buf)
    m = jnp.maximum(jnp.max(s_c, axis=-1, keepdims=True), jnp.max(s_n, axis=-1, keepdims=True))
    e_c = jnp.exp(s_c - m)
    e_n = jnp.exp(s_n - m)
    l = jnp.sum(e_c, axis=-1, keepdims=True) + jnp.sum(e_n, axis=-1, keepdims=True)
    inv = 1.0 / l
    full = (_dot_nt((e_c * inv).astype(BF16), cache[gw:].astype(BF16))
            + _dot((e_n * inv).astype(BF16), newp[:, gw:]))
    o = _pick_heads(full, t, DIL_HEAD_DIM)
    lse = _pick_heads(jnp.broadcast_to(m + jnp.log(l), (rows, gw)), t, DIL_HEAD_DIM)
    return o, lse


def _sample_mixer_kernel(*refs, t):
    q_refs, cache_refs, new_refs, new_t_refs = refs[0:3], refs[3:6], refs[6:9], refs[9:12]
    qm_ref, mkv_ref = refs[12], refs[13]
    shifted_refs, mix_ref, win_refs = refs[14:17], refs[17], refs[18:21]
    gw = GROUP_WIDTH
    nb = mix_ref.shape[0] // t
    for bb in range(nb):
        rs = slice(bb * t, (bb + 1) * t)
        outs, lses = [], []
        col0 = (pl.program_id(0) * nb + bb) * t
        block = pl.multiple_of((col0 // LANES) * LANES, LANES)
        shift = LANES - t - col0 % LANES
        for g, (window, dil) in enumerate(DIL_PAIRS):
            o, lse = _dilated_sample_group(q_refs[g][rs, :], cache_refs[g][bb], new_refs[g][rs, :], window, dil)
            outs.append(o)
            lses.append(lse)
            win_refs[g][bb] = shifted_refs[g][bb]
            new_cols = pltpu.roll(new_t_refs[g][:, pl.ds(block, LANES)], shift, 1)
            win_refs[g][bb, :, LANES - t:] = new_cols[:, LANES - t:]
        m = jnp.maximum(jnp.maximum(lses[0], lses[1]), lses[2])
        es = [jnp.exp(v - m) for v in lses]
        inv = 1.0 / (es[0] + es[1] + es[2])
        for g in range(3):
            mix_ref[rs, g * gw:(g + 1) * gw] = outs[g] * (es[g] * inv)
        mix_ref[rs, MAIN_WIDTH:] = _mem_attn_tile(qm_ref[rs, :], mkv_ref[0, bb])


def _sample_mixer(q_g, caches_t, new_kv, new_kv_t, qm, memkv_t, shifted, batch, t):
    assert LANES % t == 0
    gw = GROUP_WIDTH
    nb = SAMPLE_BATCH_PER_STEP
    tok = lambda b: (b, 0)
    big = lambda b: (b, 0, 0)
    cache_specs = [pl.BlockSpec((nb,) + c.shape[1:], big) for c in caches_t]
    tail_specs = [pl.BlockSpec((nb, c.shape[1], LANES), lambda b, j=c.shape[2] // LANES - 1: (b, 0, j))
                  for c in shifted]
    first_shifted = 3 + len(caches_t) + 3 + 3 + 2
    return pl.pallas_call(
        functools.partial(_sample_mixer_kernel, t=t),
        grid=(batch // nb,),
        in_specs=[pl.BlockSpec((nb * t, gw), tok)] * 3 + cache_specs + [pl.BlockSpec((nb * t, 2 * gw), tok)] * 3
                 + [_full_spec(a) for a in new_kv_t]
                 + [pl.BlockSpec((nb * t, MEM_WIDTH), tok),
                    pl.BlockSpec((1, nb) + memkv_t.shape[2:], lambda b: (1, b, 0, 0))] + tail_specs,
        out_specs=[pl.BlockSpec((nb * t, D_MODEL), tok)] + tail_specs,
        out_shape=[jax.ShapeDtypeStruct((batch * t, D_MODEL), F32)]
                  + [jax.ShapeDtypeStruct(c.shape, F32) for c in shifted],
        input_output_aliases={first_shifted + g: 1 + g for g in range(len(shifted))},
        compiler_params=_params(("parallel",)),
        name="sample_mixer",
    )(*q_g, *caches_t, *new_kv, *new_kv_t, qm, memkv_t, *shifted)


def _natural_rows(ref, dil, scr):
    if dil == 1:
        return ref[0, 0].astype(F32)
    n = ref.shape[2]
    for r in range(dil):
        v = ref[0, r].astype(F32)
        for c in range(v.shape[1] // LANES):
            scr[c, pl.ds(r, n, stride=dil), :] = v[:, c * LANES:(c + 1) * LANES]
    return jnp.concatenate([scr[c] for c in range(scr.shape[0])], axis=1)


def _post_body(mix, x_ref, wo_ref, g1_ref, b1_ref, wi_ref, w2_ref, g2_ref, b2_ref, out_ref, act_ref):
    rows = x_ref.shape[0]
    x1 = _layer_norm(ALPHA * x_ref[...] + _dot(mix, wo_ref[0]), g1_ref[0], b1_ref[0])
    x1b = x1.astype(BF16)
    for c in range(0, FFN_HIDDEN, FFN_COL_CHUNK):
        gate = _dot(x1b, wi_ref[0, :, c:c + FFN_COL_CHUNK])
        up = _dot(x1b, wi_ref[0, :, FFN_HIDDEN + c:FFN_HIDDEN + c + FFN_COL_CHUNK])
        act_ref[0:rows, c:c + FFN_COL_CHUNK] = (_silu(gate) * up).astype(BF16)
    y = _dot(act_ref[0:rows, :], w2_ref[0])
    out_ref[...] = _layer_norm(ALPHA * x1 + y, g2_ref[0], b2_ref[0])


def _post_kernel(*refs, dils, n_mix_p, n_mix_s, n_p, n_shift, dec_seq, flip_blocks):
    n_flip = len(flip_blocks)
    mixp_refs, refs = refs[:n_mix_p], refs[n_mix_p:]
    xp_ref, refs = refs[0], refs[1:]
    mixs_refs, refs = refs[:n_mix_s], refs[n_mix_s:]
    xs_ref, refs = refs[0], refs[1:]
    weights, refs = refs[:7], refs[7:]
    caches, refs = refs[:n_shift], refs[n_shift:]
    flip_in, refs = refs[:n_flip], refs[n_flip:]
    outp_ref, outs_ref, refs = refs[0], refs[1], refs[2:]
    wins, refs = refs[:n_shift], refs[n_shift:]
    flip_out, refs = refs[:n_flip], refs[n_flip:]
    act_ref = refs[0]
    scrs = list(refs[1:])
    i = pl.program_id(0)

    @pl.when(i < n_p)
    def _():
        if dils is not None:
            o_refs, l_refs, mem_ref = mixp_refs[0:3], mixp_refs[3:6], mixp_refs[6]
            pool = list(scrs)
            nat = lambda r, d: _natural_rows(r, d, pool.pop(0) if d > 1 else None)
            os_ = [nat(o_refs[g], dils[g]) for g in range(3)]
            ls = [nat(l_refs[g], dils[g]) for g in range(3)]
            m = jnp.maximum(jnp.maximum(ls[0], ls[1]), ls[2])
            es = [jnp.exp(v - m) for v in ls]
            inv = 1.0 / (es[0] + es[1] + es[2])
            parts = [(os_[g] * (es[g] * inv)).astype(BF16) for g in range(3)]
            parts.append(mem_ref[...].astype(BF16))
        else:
            parts = [r[...].astype(BF16) for r in mixp_refs]
        mix = parts[0] if len(parts) == 1 else jnp.concatenate(parts, axis=1)
        _post_body(mix, xp_ref, *weights, outp_ref, act_ref)
        for c_ref, w_ref in zip(caches, wins):
            _shift_window(c_ref, w_ref, dec_seq)
        for src, dst in zip(flip_in, flip_out):
            dst[0] = jnp.transpose(src[...])

    @pl.when(i == n_p)
    def _():
        parts = [r[...].astype(BF16) for r in mixs_refs]
        mix = parts[0] if len(parts) == 1 else jnp.concatenate(parts, axis=1)
        _post_body(mix, xs_ref, *weights, outs_ref, act_ref)


def _post(layer, mix_p, xp, mix_s, xs, wo, g1, b1, wi, w2, g2, b2, dils, seq, caches_t=(), dec_seq=0,
          flips=()):
    tp, d = xp.shape
    tm = TOKEN_TILE
    n_p = tp // tm
    nt = seq // tm
    cl = lambda i: jnp.minimum(i, n_p - 1)
    row = lambda i: (cl(i), 0)
    if dils is None:
        mixp_specs = [pl.BlockSpec((tm, a.shape[1]), row) for a in mix_p]
        scratch = []
    else:
        res = lambda i: (cl(i) // nt, 0, cl(i) % nt, 0)
        mixp_specs = [pl.BlockSpec((1, dl, tm // dl, GROUP_WIDTH), res) for dl in dils] * 2
        mixp_specs.append(pl.BlockSpec((tm, MEM_WIDTH), row))
        scratch = [pltpu.VMEM((GROUP_WIDTH // LANES, tm, LANES), F32) for dl in dils * 2 if dl > 1]
    lsel = (layer, 0, 0)
    vec = _const_spec((1, 1, d), lsel)
    shift_specs = _shift_specs(caches_t, n_p)
    flip_in_specs, flip_out_specs, flip_shapes, flip_blocks = [], [], [], []
    for a, nbatch in flips:
        w = a.shape[0] // nbatch
        rb = min(w, tm)
        per = w // rb
        nblk = nbatch * per
        assert nblk <= n_p
        blk = lambda i, nblk=nblk: jnp.minimum(i, nblk - 1)
        flip_in_specs.append(pl.BlockSpec((rb, a.shape[1]), lambda i, blk=blk: (blk(i), 0)))
        flip_out_specs.append(pl.BlockSpec((1, a.shape[1], rb),
                                           lambda i, blk=blk, per=per: (blk(i) // per, 0, blk(i) % per)))
        flip_shapes.append(jax.ShapeDtypeStruct((nbatch, a.shape[1], w), F32))
        flip_blocks.append(nblk)
    return pl.pallas_call(
        functools.partial(_post_kernel, dils=dils, n_mix_p=len(mix_p), n_mix_s=len(mix_s), n_p=n_p,
                          n_shift=len(caches_t), dec_seq=dec_seq, flip_blocks=tuple(flip_blocks)),
        grid=(n_p + 1,),
        in_specs=mixp_specs + [pl.BlockSpec((tm, d), row)] + [_full_spec(a) for a in mix_s] + [_full_spec(xs)]
                 + [_const_spec((1,) + wo.shape[1:], lsel), vec, vec,
                    _const_spec((1,) + wi.shape[1:], lsel), _const_spec((1,) + w2.shape[1:], lsel), vec, vec]
                 + shift_specs + flip_in_specs,
        out_specs=[pl.BlockSpec((tm, d), row), pl.BlockSpec(xs.shape, lambda i: (0, 0))] + shift_specs
                  + flip_out_specs,
        out_shape=[jax.ShapeDtypeStruct((tp, d), F32), jax.ShapeDtypeStruct(xs.shape, F32)]
                  + [jax.ShapeDtypeStruct(c.shape, F32) for c in caches_t] + flip_shapes,
        scratch_shapes=[pltpu.VMEM((tm, FFN_HIDDEN), BF16)] + scratch,
        compiler_params=_params(("arbitrary",)),
        name="post" if dils is None else "post_combine",
    )(*mix_p, xp, *mix_s, xs, wo, g1, b1, wi, w2, g2, b2, *caches_t, *[a for a, _ in flips])


def _residue_major(ref, y, dil, scr):
    if dil == 1:
        ref[...] = y.astype(ref.dtype).reshape(ref.shape)
        return
    n = ref.shape[2]
    ncol = y.shape[1] // LANES
    for c in range(ncol):
        scr[c] = y[:, c * LANES:(c + 1) * LANES]
    for r in range(dil):
        rows = [scr[c, pl.ds(r, n, stride=dil), :] for c in range(ncol)]
        ref[0, r] = jnp.concatenate(rows, axis=1).astype(ref.dtype)


def _in_b_body(x_ref, wkv_ref, wq_ref, cos_ref, sin_ref, mkv_ref,
               q_refs, kv_refs, m_ref, win_refs, scrs, dils, win_preds):
    xb = x_ref[...].astype(BF16)
    cos = cos_ref[...]
    sin = sin_ref[...]
    gw = GROUP_WIDTH
    mw = MAIN_WIDTH
    half = DIL_HEAD_DIM // 2
    tm = x_ref.shape[0]
    pool = list(scrs)
    for g in range(3):
        ks = slice(g * gw, (g + 1) * gw)
        vs = slice(mw + g * gw, mw + (g + 1) * gw)
        k = _rope_cols(_dot(xb, wkv_ref[:, ks]), cos, sin, half)
        v = _dot(xb, wkv_ref[:, vs])
        kv = jnp.concatenate([k, v], axis=1)
        _residue_major(kv_refs[g], kv, dils[g], pool.pop(0) if dils[g] > 1 else None)
        q = _rope_cols(_dot(xb, wq_ref[0, :, ks]), cos, sin, half) * ATTN_SCALE
        _residue_major(q_refs[g], q, dils[g], pool.pop(0) if dils[g] > 1 else None)

        def window(g=g, kv=kv):
            if len(win_refs[g].shape) == 2:
                win_refs[g][...] = kv[tm - win_refs[g].shape[0]:, :]
            else:
                win_refs[g][0] = jnp.transpose(kv)

        if win_preds[g] is None:
            window()
        else:
            pl.when(win_preds[g])(window)
    qm = _dot(xb, wq_ref[0, :, mw:mw + MEM_WIDTH]) * ATTN_SCALE
    if mkv_ref is not None:
        m_ref[...] = _mem_attn_tile(qm.astype(BF16), mkv_ref[0, 0]).astype(m_ref.dtype)
    else:
        m_ref[...] = qm.astype(m_ref.dtype)


def _in_b_kernel(xp_ref, xs_ref, wkv_ref, wq_ref, cosp_ref, sinp_ref, coss_ref, sins_ref, mkv_ref, *refs,
                 dils, win_tiles, nt, n_p):
    outp, outs, scrs = refs[0:10], refs[10:20], refs[20:]
    i = pl.program_id(0)

    @pl.when(i < n_p)
    def _():
        _in_b_body(xp_ref, wkv_ref, wq_ref, cosp_ref, sinp_ref, mkv_ref,
                   outp[0:3], outp[3:6], outp[6], outp[7:10], scrs, dils, [None] * 3)

    @pl.when(i == n_p)
    def _():
        _in_b_body(xs_ref, wkv_ref, wq_ref, coss_ref, sins_ref, None,
                   outs[0:3], outs[3:6], outs[6], outs[7:10], (), (1, 1, 1), [None] * 3)


def _in_b(xp, xs, w_kv, w_q, tabs_p, tabs_s, memkv_t, batch, seq, dils, windows):
    tp, d = xp.shape
    ts = xs.shape[0]
    gw = GROUP_WIDTH
    tm = TOKEN_TILE
    n_p = tp // tm
    nt = seq // tm
    cl = lambda i: jnp.minimum(i, n_p - 1)
    row = lambda i: (cl(i), 0)
    tab = lambda i: (cl(i) % nt, 0)
    res = lambda i: (cl(i) // nt, 0, cl(i) % nt, 0)
    fix2 = lambda i: (0, 0)
    out_specs = ([pl.BlockSpec((1, dl, tm // dl, gw), res) for dl in dils]
                 + [pl.BlockSpec((1, dl, tm // dl, 2 * gw), res) for dl in dils]
                 + [pl.BlockSpec((tm, MEM_WIDTH), row)])
    out_shape = ([jax.ShapeDtypeStruct((batch, dl, seq // dl, gw), BF16) for dl in dils]
                 + [jax.ShapeDtypeStruct((batch, dl, seq // dl, 2 * gw), BF16) for dl in dils]
                 + [jax.ShapeDtypeStruct((tp, MEM_WIDTH), BF16)])
    win_tiles = []
    for w in windows:
        rb = min(w, tm)
        nblk = w // rb
        win_tiles.append(nblk)
        out_specs.append(pl.BlockSpec(
            (rb, 2 * gw), lambda i, nblk=nblk: ((cl(i) // nt) * nblk + jnp.maximum(cl(i) % nt - (nt - nblk), 0), 0)))
        out_shape.append(jax.ShapeDtypeStruct((batch * w, 2 * gw), F32))
    out_specs += ([pl.BlockSpec((ts, gw), fix2)] * 3 + [pl.BlockSpec((ts, 2 * gw), fix2)] * 3
                  + [pl.BlockSpec((ts, MEM_WIDTH), fix2)] + [pl.BlockSpec((1, 2 * gw, ts), lambda i: (0, 0, 0))] * 3)
    out_shape += ([jax.ShapeDtypeStruct((ts, gw), F32)] * 3 + [jax.ShapeDtypeStruct((ts, 2 * gw), F32)] * 3
                  + [jax.ShapeDtypeStruct((ts, MEM_WIDTH), F32)] + [jax.ShapeDtypeStruct((1, 2 * gw, ts), F32)] * 3)
    scratch = []
    for dl in dils:
        if dl > 1:
            scratch += [pltpu.VMEM((2 * gw // LANES, tm, LANES), F32), pltpu.VMEM((gw // LANES, tm, LANES), F32)]
    return pl.pallas_call(
        functools.partial(_in_b_kernel, dils=tuple(dils), win_tiles=tuple(win_tiles), nt=nt, n_p=n_p),
        grid=(n_p + 1,),
        in_specs=[pl.BlockSpec((tm, d), row), _full_spec(xs), _const_spec(w_kv.shape, (0, 0)),
                  _const_spec((1,) + w_q.shape[1:], (0, 0, 0)),
                  pl.BlockSpec((tm, LANES), tab), pl.BlockSpec((tm, LANES), tab),
                  _full_spec(tabs_s[0]), _full_spec(tabs_s[1]),
                  pl.BlockSpec((1, 1) + memkv_t.shape[2:], lambda i: (1, cl(i) // nt, 0, 0))],
        out_specs=out_specs,
        out_shape=out_shape,
        scratch_shapes=scratch,
        compiler_params=_params(("arbitrary",)),
        name="in_proj_b",
    )(xp, xs, w_kv, w_q, *tabs_p, *tabs_s, memkv_t)


def _dilated_kernel(q_ref, kv_ref, o_ref, lse_ref, ring_ref):
    i = pl.program_id(2)
    blk = DIL_BLOCK
    gw = GROUP_WIDTH
    nres = q_ref.shape[1]
    nsub = q_ref.shape[2] // blk
    slot = i & 1

    @pl.when(i == 0)
    def _():
        for r in range(nres):
            ring_ref[r, 1] = jnp.zeros(ring_ref.shape[2:], ring_ref.dtype)

    rows = GROUP_HEADS * blk
    qi = lax.broadcasted_iota(jnp.int32, (rows, 2 * blk), 0) & (blk - 1)
    kj = lax.broadcasted_iota(jnp.int32, (rows, 2 * blk), 1)
    delta = qi + blk - kj
    band = (delta >= 0) & (delta <= blk)
    for r in range(nres):
        cur = kv_ref[0, r]
        ring_ref[r, slot] = cur[(nsub - 1) * blk:]
        kext = jnp.concatenate([ring_ref[r, 1 - slot], cur], axis=0)
        for j in range(nsub):
            rs = slice(j * blk, (j + 1) * blk)
            q4 = _stack_heads(q_ref[0, r, rs, :], DIL_HEAD_DIM)
            kv = kext[j * blk:(j + 2) * blk]
            valid = band if j > 0 else band & ((i > 0) | (kj >= blk))
            s = jnp.where(valid, _dot_nt(q4, kv[:, :gw]), NEG_BIG)
            m = jnp.max(s, axis=-1, keepdims=True)
            e = jnp.exp(s - m)
            l = jnp.sum(e, axis=-1, keepdims=True)
            p = (e * (1.0 / l)).astype(BF16)
            full = _dot(p, kv[:, gw:])
            o_ref[0, r, rs, :] = _pick_heads(full, blk, DIL_HEAD_DIM).astype(o_ref.dtype)
            lse_ref[0, r, rs, :] = _pick_heads(jnp.broadcast_to(m + jnp.log(l), (rows, gw)), blk, DIL_HEAD_DIM)


def _dilated(q, kv, dil):
    batch, _, m, gw = q.shape
    nsub = min(DIL_BLOCKS_PER_STEP, m // DIL_BLOCK)
    nres = min(dil, DIL_BLOCKS_PER_STEP // nsub)
    rows = DIL_BLOCK * nsub
    idx = lambda b, r, i: (b, r, i, 0)
    return pl.pallas_call(
        _dilated_kernel,
        grid=(batch, dil // nres, m // rows),
        in_specs=[pl.BlockSpec((1, nres, rows, gw), idx), pl.BlockSpec((1, nres, rows, 2 * gw), idx)],
        out_specs=[pl.BlockSpec((1, nres, rows, gw), idx)] * 2,
        out_shape=[jax.ShapeDtypeStruct(q.shape, BF16), jax.ShapeDtypeStruct(q.shape, F32)],
        scratch_shapes=[pltpu.VMEM((nres, 2, DIL_BLOCK, 2 * gw), BF16)],
        compiler_params=_params(("arbitrary", "arbitrary", "arbitrary")),
        name=f"dilated_d{dil}",
    )(q, kv)


def _rope_tables(base, n):
    split = min(ROPE_SPLIT, n)
    hi = base + split * jnp.arange(n // split, dtype=F32)
    lo = jnp.arange(split, dtype=F32)
    inv = ROPE_THETA ** (-jnp.arange(0, RET_HEAD_DIM, 2, dtype=F32) / RET_HEAD_DIM)
    inv_h = inv[0::2]
    neg = lambda m: jnp.concatenate([-jnp.ones((m,), F32), jnp.ones((m,), F32)])

    def token_major(freq, sign):
        xa, xb = hi[:, None] * freq[None, :], lo[:, None] * freq[None, :]
        ca, sa, cb, sb = jnp.cos(xa), jnp.sin(xa), jnp.cos(xb), jnp.sin(xb)
        cos = ca[:, None, :] * cb[None, :, :] - sa[:, None, :] * sb[None, :, :]
        sin = (sa[:, None, :] * cb[None, :, :] + ca[:, None, :] * sb[None, :, :]) * sign
        return cos.reshape(n, -1), sin.reshape(n, -1)

    f_big = jnp.concatenate([inv, inv])
    f_half = jnp.concatenate([inv_h, inv_h])
    return (token_major(f_big, neg(RET_HEAD_DIM // 2)),
            token_major(jnp.tile(f_half, 2), jnp.tile(neg(DIL_HEAD_DIM // 2), 2)))


def _to_feature_major(x5):
    b, w = x5.shape[0], x5.shape[1]
    return jnp.transpose(x5, (0, 2, 3, 4, 1)).reshape(b, 2 * GROUP_WIDTH, w)


def _from_feature_major(xt):
    b, _, w = xt.shape
    return jnp.transpose(xt.reshape(b, 2, GROUP_HEADS, DIL_HEAD_DIM, w), (0, 4, 1, 2, 3))


def kernel(x_prompt, x_sample, mem_prompt, cache_mem_kv, state_ret, cache_win_kv_g1, cache_win_kv_g2, cache_win_kv_g3, w_in_a, w_in_b, w_out, w_kv_shared, w_mem_kv, ln_mix_g, ln_mix_b, ln_ffn_g, ln_ffn_b, w_ffn_in, w_ffn_out):
    batch, seq, d = x_prompt.shape
    dec_batch, dec_seq, _ = x_sample.shape
    n_mem = mem_prompt.shape[1]
    gw = GROUP_WIDTH
    ts = dec_batch * dec_seq
    win_caches = (cache_win_kv_g1, cache_win_kv_g2, cache_win_kv_g3)
    dils = tuple(dl for _, dl in DIL_PAIRS)

    tab_a_p, tab_b_p = _rope_tables(0.0, seq)
    tab_a_s, tab_b_s = _rope_tables(float(PAST_LEN), dec_seq)
    tab_a_s = tuple(jnp.tile(a, (dec_batch, 1)) for a in tab_a_s)
    tab_b_s = tuple(jnp.tile(a, (dec_batch, 1)) for a in tab_b_s)

    w_a = w_in_a.astype(BF16)
    w_kv = w_kv_shared.astype(BF16)
    w_q = w_in_b.astype(BF16)
    w_o = w_out.astype(BF16)
    w_fi = w_ffn_in.astype(BF16)
    w_fo = w_ffn_out.astype(BF16)
    w_mem_t = jnp.transpose(w_mem_kv, (0, 2, 1)).astype(BF16)
    lnv = lambda a: a.reshape(DEPTH, 1, d)
    ln = (lnv(ln_mix_g), lnv(ln_mix_b), lnv(ln_ffn_g), lnv(ln_ffn_b))

    memkv_p = _mem_proj(mem_prompt, w_mem_t)
    memkv_s = jnp.transpose(cache_mem_kv, (0, 1, 3, 4, 5, 2)).reshape(DEPTH, dec_batch, 2 * MEM_WIDTH, n_mem)

    xp = x_prompt.reshape(batch * seq, d)
    xs = x_sample.reshape(ts, d)

    def post(l, mix_p, mix_s, dls, shift=(), flips=()):
        return _post(l, mix_p, xp, mix_s, xs, w_o, ln[0], ln[1], w_fi, w_fo, ln[2], ln[3], dls, seq,
                     shift, dec_seq, flips)

    caches_t = [_to_feature_major(c) for c in win_caches]

    outs = _in_a(xp, xs, w_a, tab_a_p, tab_a_s, memkv_p, seq, caches_t[:2], dec_seq)
    q, k, v, g, mem_o = outs[:5]
    mix_p, state_p = _retention(q, k, v, g, batch, seq)
    q, k, v, g, qm = outs[5:10]
    shifted = list(outs[10:])
    mix_s, state_s = _retention_sample(q, k, v, g, qm, memkv_s, state_ret, dec_batch, dec_seq)
    xp, xs, *shifted_big = post(0, [mix_p, mem_o], [mix_s], None, caches_t[2:])
    shifted += shifted_big

    windows_p = tuple(min(w, seq) for w, _ in DIL_PAIRS)
    outs = _in_b(xp, xs, w_kv, w_q, tab_b_p, tab_b_s, memkv_p, batch, seq, dils, windows_p)
    q_g, kv_g, mem_o, win_p = outs[0:3], outs[3:6], outs[6], outs[7:10]
    att = [_dilated(q_g[i], kv_g[i], dils[i]) for i in range(3)]

    q_s, kv_new, qm, new_t = outs[10:13], outs[13:16], outs[16], outs[17:20]
    new_t = [a.reshape(2 * gw, ts) for a in new_t]
    mix_s, *win_s = _sample_mixer(q_s, caches_t, kv_new, new_t, qm, memkv_s, shifted, dec_batch, dec_seq)

    xp, xs, *win_p = post(1, [a[0] for a in att] + [a[1] for a in att] + [mem_o], [mix_s], dils,
                          flips=[(w, batch) for w in win_p])

    memkv_out = jnp.transpose(memkv_p.reshape(DEPTH, batch, 2, MEM_HEADS, MEM_HEAD_DIM, n_mem), (0, 1, 5, 2, 3, 4))
    return (xp.reshape(batch, seq, d), xs.reshape(dec_batch, dec_seq, d),
            state_p[None], state_s[None], memkv_out,
            _from_feature_major(win_p[0]), _from_feature_major(win_p[1]), _from_feature_major(win_p[2]),
            _from_feature_major(win_s[0]), _from_feature_major(win_s[1]), _from_feature_major(win_s[2]))
```

```python
import functools
import math

import jax
import jax.numpy as jnp
from jax import lax
from jax.experimental import pallas as pl
from jax.experimental.pallas import tpu as pltpu

F32 = jnp.float32
BF16 = jnp.bfloat16

D_MODEL = 1024
MEM_HEADS = 4
MEM_HEAD_DIM = 64
MEM_WIDTH = MEM_HEADS * MEM_HEAD_DIM
MAIN_WIDTH = D_MODEL - MEM_WIDTH
RET_HEADS = 6
RET_HEAD_DIM = MAIN_WIDTH // RET_HEADS
RET_CHUNK = 128
RET_CHUNKS_PER_STEP = 16
DIL_PAIRS = ((128, 1), (512, 4), (2048, 16))
GROUP_HEADS = 4
DIL_HEAD_DIM = 64
GROUP_WIDTH = GROUP_HEADS * DIL_HEAD_DIM
DIL_BLOCK = 128
DIL_BLOCKS_PER_STEP = 16
FFN_HIDDEN = 2816
ROPE_THETA = 10000.0
ROPE_SPLIT = 64
LN_EPS = 1e-5
DEPTH = 2
ALPHA = (2 * DEPTH) ** 0.25
PAST_LEN = 8192
NEG_BIG = -1e30
ATTN_SCALE = DIL_HEAD_DIM ** -0.5

LANES = 128
BF16_SUBLANES = 16
SAMPLE_BATCH_PER_STEP = 2
VMEM_LIMIT = 56 * 1024 * 1024
FFN_COL_CHUNK = 256
TOKEN_TILE = 512
IN_A_TILE = 1024

LOG_G = tuple(math.log1p(-(2.0 ** (-5.0 - h))) for h in range(RET_HEADS))


def _dot(a, b):
    return jnp.dot(a, b, preferred_element_type=F32)


def _dot_nt(a, b):
    return lax.dot_general(a, b, (((1,), (1,)), ((), ())), preferred_element_type=F32)


def _dot_tn(a, b):
    return lax.dot_general(a, b, (((0,), (0,)), ((), ())), preferred_element_type=F32)


def _silu(x):
    return x / (1.0 + jnp.exp(-x))


def _normalize(z):
    zs = z - z[:, 0:1]
    m1 = jnp.mean(zs, axis=-1, keepdims=True)
    m2 = jnp.mean(zs * zs, axis=-1, keepdims=True)
    return (zs - m1) * lax.rsqrt(m2 - m1 * m1 + LN_EPS)


def _layer_norm(z, g, b):
    return _normalize(z) * g + b


def _rope_lanes(y, cos, sin_signed, half):
    if 2 * half == LANES:
        partner = pltpu.roll(y, half, 1)
    else:
        lane = lax.broadcasted_iota(jnp.int32, y.shape, 1)
        first = (lane & (2 * half - 1)) < half
        partner = jnp.where(first, pltpu.roll(y, LANES - half, 1), pltpu.roll(y, half, 1))
    return y * cos + partner * sin_signed


def _rope_cols(y, cos, sin_signed, half):
    parts = [_rope_lanes(y[:, j:j + LANES], cos, sin_signed, half) for j in range(0, y.shape[1], LANES)]
    return parts[0] if len(parts) == 1 else jnp.concatenate(parts, axis=1)


def _head_mask(shape, h, width):
    lane = lax.broadcasted_iota(jnp.int32, shape, len(shape) - 1)
    return (lane >= h * width) & (lane < (h + 1) * width)


def _stack_heads(q, width):
    zero = jnp.zeros((), q.dtype)
    parts = [jnp.where(_head_mask(q.shape, h, width), q, zero) for h in range(q.shape[1] // width)]
    return jnp.concatenate(parts, axis=0).astype(BF16)


def _pick_heads(full, t, width):
    per = LANES // width
    cols = []
    for c in range(full.shape[1] // LANES):
        out = None
        for i in range(per):
            h = c * per + i
            blk = full[h * t:(h + 1) * t, c * LANES:(c + 1) * LANES]
            out = blk if out is None else jnp.where(_head_mask(blk.shape, i, width), blk, out)
        cols.append(out)
    return cols[0] if len(cols) == 1 else jnp.concatenate(cols, axis=1)


def _const_spec(block, index):
    return pl.BlockSpec(block, lambda *_: index, pipeline_mode=pl.Buffered(1))


def _full_spec(a):
    return pl.BlockSpec(a.shape, lambda *_: (0,) * a.ndim)


def _params(sem):
    return pltpu.CompilerParams(dimension_semantics=sem, vmem_limit_bytes=VMEM_LIMIT)


def _mem_proj_kernel(m_ref, w_ref, o_ref):
    o_ref[0, 0] = _dot_nt(w_ref[0], m_ref[0].astype(BF16))


def _mem_proj(mem, w_t):
    batch, n_mem, d = mem.shape
    depth, n, _ = w_t.shape
    return pl.pallas_call(
        _mem_proj_kernel,
        grid=(depth, batch),
        in_specs=[pl.BlockSpec((1, n_mem, d), lambda l, b: (b, 0, 0)),
                  pl.BlockSpec((1, n, d), lambda l, b: (l, 0, 0))],
        out_specs=pl.BlockSpec((1, 1, n, n_mem), lambda l, b: (l, b, 0, 0)),
        out_shape=jax.ShapeDtypeStruct((depth, batch, n, n_mem), F32),
        compiler_params=_params(("arbitrary", "arbitrary")),
        name="mem_proj",
    )(mem, w_t)


def _mem_attn_tile(qm, kv_t):
    t = qm.shape[0]
    q4 = _stack_heads(qm, MEM_HEAD_DIM)
    k_t = kv_t[:MEM_WIDTH].astype(BF16)
    v_t = kv_t[MEM_WIDTH:].astype(BF16)
    s = _dot(q4, k_t)
    m = jnp.max(s, axis=-1, keepdims=True)
    e = jnp.exp(s - m)
    p = (e / jnp.sum(e, axis=-1, keepdims=True)).astype(BF16)
    return _pick_heads(_dot_nt(p, v_t), t, MEM_HEAD_DIM)


def _in_a_body(x_ref, w_ref, cos_ref, sin_ref, mkv_ref, q_ref, k_ref, v_ref, g_ref, m_ref):
    xb = x_ref[...].astype(BF16)
    cos = cos_ref[...]
    sin = sin_ref[...]
    mw = MAIN_WIDTH
    half = RET_HEAD_DIM // 2
    q = _rope_cols(_dot(xb, w_ref[0, :, 0:mw]), cos, sin, half)
    q_ref[...] = q.astype(q_ref.dtype)
    k = _rope_cols(_dot(xb, w_ref[0, :, mw:2 * mw]), cos, sin, half) * (RET_HEAD_DIM ** -0.5)
    k_ref[...] = k.astype(k_ref.dtype)
    v_ref[...] = _dot(xb, w_ref[0, :, 2 * mw:3 * mw]).astype(v_ref.dtype)
    g_ref[...] = _silu(_dot(xb, w_ref[0, :, 3 * mw:4 * mw])).astype(g_ref.dtype)
    qm = _dot(xb, w_ref[0, :, 4 * mw:4 * mw + MEM_WIDTH]) * ATTN_SCALE
    if mkv_ref is not None:
        m_ref[...] = _mem_attn_tile(qm.astype(BF16), mkv_ref[0, 0]).astype(m_ref.dtype)
    else:
        m_ref[...] = qm.astype(m_ref.dtype)


def _shift_window(cache_ref, win_ref, t):
    n = cache_ref.shape[2]
    for e in range(cache_ref.shape[0]):
        win_ref[e, :, 0:n - t] = cache_ref[e, :, t:n]
        win_ref[e, :, n - t:n] = jnp.zeros((cache_ref.shape[1], t), F32)


def _shift_specs(caches_t, n_steps):
    if not caches_t:
        return []
    nbatch = caches_t[0].shape[0]
    per = -(-nbatch // n_steps)
    assert nbatch % per == 0
    last = nbatch // per - 1
    return [pl.BlockSpec((per,) + c.shape[1:], lambda i: (jnp.minimum(i, last), 0, 0)) for c in caches_t]


def _in_a_kernel(xp_ref, xs_ref, w_ref, cosp_ref, sinp_ref, coss_ref, sins_ref, mkv_ref, *refs,
                 n_p, n_shift, dec_seq):
    caches, outs, wins = refs[:n_shift], refs[n_shift:n_shift + 10], refs[n_shift + 10:]
    i = pl.program_id(0)

    @pl.when(i < n_p)
    def _():
        _in_a_body(xp_ref, w_ref, cosp_ref, sinp_ref, mkv_ref, *outs[:5])
        for c_ref, w_ref_ in zip(caches, wins):
            _shift_window(c_ref, w_ref_, dec_seq)

    @pl.when(i == n_p)
    def _():
        _in_a_body(xs_ref, w_ref, coss_ref, sins_ref, None, *outs[5:])


def _in_a(xp, xs, w_bf, tab_p, tab_s, memkv_t, seq, caches_t, dec_seq):
    tp, d = xp.shape
    ts = xs.shape[0]
    tm = IN_A_TILE
    n_p = tp // tm
    nt = seq // tm
    cl = lambda i: jnp.minimum(i, n_p - 1)
    row = lambda i: (cl(i), 0)
    tab = lambda i: (cl(i) % nt, 0)
    widths = (MAIN_WIDTH,) * 4 + (MEM_WIDTH,)
    shift_specs = _shift_specs(caches_t, n_p)
    return pl.pallas_call(
        functools.partial(_in_a_kernel, n_p=n_p, n_shift=len(caches_t), dec_seq=dec_seq),
        grid=(n_p + 1,),
        in_specs=[pl.BlockSpec((tm, d), row), _full_spec(xs), _const_spec((1,) + w_bf.shape[1:], (0, 0, 0)),
                  pl.BlockSpec((tm, LANES), tab), pl.BlockSpec((tm, LANES), tab),
                  _full_spec(tab_s[0]), _full_spec(tab_s[1]),
                  pl.BlockSpec((1, 1) + memkv_t.shape[2:], lambda i: (0, cl(i) // nt, 0, 0))] + shift_specs,
        out_specs=[pl.BlockSpec((tm, w), row) for w in widths]
                  + [pl.BlockSpec((ts, w), lambda i: (0, 0)) for w in widths] + shift_specs,
        out_shape=[jax.ShapeDtypeStruct((tp, w), BF16) for w in widths]
                  + [jax.ShapeDtypeStruct((ts, w), F32) for w in widths]
                  + [jax.ShapeDtypeStruct(c.shape, F32) for c in caches_t],
        compiler_params=_params(("arbitrary",)),
        name="in_proj_a",
    )(xp, xs, w_bf, *tab_p, *tab_s, memkv_t, *caches_t)


def _retention_kernel(q_ref, k_ref, v_ref, g_ref, mix_ref, st_ref, dec_ref, rdec_ref, kdec_ref,
                      inner_ref, kv_ref):
    c = pl.program_id(1)
    cs = RET_CHUNK

    @pl.when(c == 0)
    def _():
        st_ref[...] = jnp.zeros_like(st_ref)
        row = lax.broadcasted_iota(jnp.int32, (cs, cs), 0).astype(F32)
        col = lax.broadcasted_iota(jnp.int32, (cs, cs), 1).astype(F32)
        diff = row - col
        for h in range(RET_HEADS):
            lg = LOG_G[h]
            dec_ref[h] = jnp.where(diff >= 0, jnp.exp(jnp.maximum(diff, 0.0) * lg), 0.0)
            rdec_ref[h] = jnp.exp((row + 1.0) * lg)
            kdec_ref[h] = jnp.exp((cs - 1.0 - row) * lg)

    nj = q_ref.shape[0] // cs
    for j in range(nj):
        rs = slice(j * cs, (j + 1) * cs)
        for h in range(RET_HEADS):
            hs = slice(h * RET_HEAD_DIM, (h + 1) * RET_HEAD_DIM)
            kh = k_ref[rs, hs]
            vh = v_ref[rs, hs]
            s = _dot_nt(q_ref[rs, hs], kh) * dec_ref[h]
            inner_ref[j, h] = _dot(s.astype(BF16), vh)
            kd = (kh.astype(F32) * kdec_ref[h]).astype(BF16)
            kv_ref[j, h] = _dot_tn(kd, vh)
    for j in range(nj):
        rs = slice(j * cs, (j + 1) * cs)
        for h in range(RET_HEADS):
            hs = slice(h * RET_HEAD_DIM, (h + 1) * RET_HEAD_DIM)
            st = st_ref[0, h]
            cross = _dot(q_ref[rs, hs], st.astype(BF16)) * rdec_ref[h]
            st_ref[0, h] = math.exp(cs * LOG_G[h]) * st + kv_ref[j, h]
            o = inner_ref[j, h] + cross
            mu = jnp.mean(o, axis=-1, keepdims=True)
            oc = o - mu
            var = jnp.mean(oc * oc, axis=-1, keepdims=True)
            on = oc * lax.rsqrt(var + LN_EPS)
            mix_ref[rs, hs] = (g_ref[rs, hs].astype(F32) * on).astype(mix_ref.dtype)


def _retention(q, k, v, g, batch, seq):
    t = q.shape[0]
    rows = RET_CHUNK * RET_CHUNKS_PER_STEP
    ns = seq // rows
    tok = lambda b, c: (b * ns + c, 0)
    sq = (RET_HEADS, RET_CHUNK, RET_CHUNK)
    return pl.pallas_call(
        _retention_kernel,
        grid=(batch, ns),
        in_specs=[pl.BlockSpec((rows, MAIN_WIDTH), tok)] * 4,
        out_specs=[pl.BlockSpec((rows, MAIN_WIDTH), tok),
                   pl.BlockSpec((1, RET_HEADS, RET_HEAD_DIM, RET_HEAD_DIM), lambda b, c: (b, 0, 0, 0))],
        out_shape=[jax.ShapeDtypeStruct((t, MAIN_WIDTH), BF16),
                   jax.ShapeDtypeStruct((batch, RET_HEADS, RET_HEAD_DIM, RET_HEAD_DIM), F32)],
        scratch_shapes=[pltpu.VMEM(sq, F32)] * 3 + [pltpu.VMEM((RET_CHUNKS_PER_STEP,) + sq, F32)] * 2,
        compiler_params=_params(("arbitrary", "arbitrary")),
        name="retention",
    )(q, k, v, g)


def _retention_sample_kernel(q_ref, k_ref, v_ref, g_ref, qm_ref, mkv_ref, st_ref, mix_ref, nst_ref, *, t):
    nb = q_ref.shape[0] // t
    pad_k = jnp.zeros((LANES - t, RET_HEAD_DIM), F32)
    pad_q = jnp.zeros((BF16_SUBLANES - t, RET_HEAD_DIM), F32)
    lhs = lambda x: jnp.concatenate([x, pad_q], axis=0).astype(BF16)
    row = lax.broadcasted_iota(jnp.int32, (t, LANES), 0).astype(F32)
    col = lax.broadcasted_iota(jnp.int32, (t, LANES), 1).astype(F32)
    prow = lax.broadcasted_iota(jnp.int32, (LANES, RET_HEAD_DIM), 0).astype(F32)
    diff = row - col
    for h in range(RET_HEADS):
        lg = LOG_G[h]
        hs = slice(h * RET_HEAD_DIM, (h + 1) * RET_HEAD_DIM)
        dec = jnp.where(diff >= 0, jnp.exp(jnp.maximum(diff, 0.0) * lg), 0.0)
        rdec = jnp.exp((row + 1.0) * lg)
        kdec = jnp.exp((t - 1.0 - prow) * lg)
        for bb in range(nb):
            rs = slice(bb * t, (bb + 1) * t)
            qh = lhs(q_ref[rs, hs])
            kp = jnp.concatenate([k_ref[rs, hs], pad_k], axis=0)
            vp = jnp.concatenate([v_ref[rs, hs], pad_k], axis=0).astype(BF16)
            st = st_ref[0, bb, h]
            inner = _dot(lhs(_dot_nt(qh, kp.astype(BF16))[:t] * dec), vp)[:t]
            cross = _dot(qh, st.astype(BF16))[:t] * rdec
            nst_ref[bb, h] = math.exp(t * lg) * st + _dot_tn((kp * kdec).astype(BF16), vp)
            o = inner + cross
            mu = jnp.mean(o, axis=-1, keepdims=True)
            oc = o - mu
            var = jnp.mean(oc * oc, axis=-1, keepdims=True)
            on = oc * lax.rsqrt(var + LN_EPS)
            mix_ref[rs, hs] = g_ref[rs, hs] * on
    for bb in range(nb):
        rs = slice(bb * t, (bb + 1) * t)
        mix_ref[rs, MAIN_WIDTH:] = _mem_attn_tile(qm_ref[rs, :], mkv_ref[0, bb])


def _retention_sample(q, k, v, g, qm, memkv_t, state, batch, t):
    nb = SAMPLE_BATCH_PER_STEP
    tok = lambda b: (b, 0)
    hd = (RET_HEADS, RET_HEAD_DIM, RET_HEAD_DIM)
    return pl.pallas_call(
        functools.partial(_retention_sample_kernel, t=t),
        grid=(batch // nb,),
        in_specs=[pl.BlockSpec((nb * t, MAIN_WIDTH), tok)] * 4
                 + [pl.BlockSpec((nb * t, MEM_WIDTH), tok),
                    pl.BlockSpec((1, nb) + memkv_t.shape[2:], lambda b: (0, b, 0, 0)),
                    pl.BlockSpec((1, nb) + hd, lambda b: (0, b, 0, 0, 0))],
        out_specs=[pl.BlockSpec((nb * t, D_MODEL), tok), pl.BlockSpec((nb,) + hd, lambda b: (b, 0, 0, 0))],
        out_shape=[jax.ShapeDtypeStruct((batch * t, D_MODEL), F32),
                   jax.ShapeDtypeStruct((batch,) + hd, F32)],
        compiler_params=_params(("parallel",)),
        name="retention_sample",
    )(q, k, v, g, qm, memkv_t, state)


def _dilated_sample_group(q, cache, new, window, dil):
    t = q.shape[0]
    n_buf = cache.shape[1]
    gw = GROUP_WIDTH
    q4 = _stack_heads(q, DIL_HEAD_DIM)
    rows = q4.shape[0]
    newp = jnp.concatenate([new, jnp.zeros((LANES - t, new.shape[1]), F32)], axis=0).astype(BF16)

    def masked(s, first_index):
        key = lax.broadcasted_iota(jnp.int32, s.shape, 1) + first_index
        tok = lax.broadcasted_iota(jnp.int32, s.shape, 0) & (t - 1)
        delta = n_buf + tok - key
        valid = (delta >= 0) & (delta <= window) & ((delta & (dil - 1)) == 0)
        return jnp.where(valid, s, NEG_BIG)

    s_c = masked(_dot(q4, cache[:gw].astype(BF16)), 0)
    s_n = masked(_dot_nt(q4, newp[:, :gw]), n_buf)
    m = jnp.maximum(jnp.max(s_c, axis=-1, keepdims=True), jnp.max(s_n, axis=-1, keepdims=True))
    e_c = jnp.exp(s_c - m)
    e_n = jnp.exp(s_n - m)
    l = jnp.sum(e_c, axis=-1, keepdims=True) + jnp.sum(e_n, axis=-1, keepdims=True)
    inv = 1.0 / l
    full = (_dot_nt((e_c * inv).astype(BF16), cache[gw:].astype(BF16))
            + _dot((e_n * inv).astype(BF16), newp[:, gw:]))
    o = _pick_heads(full, t, DIL_HEAD_DIM)
    lse = _pick_heads(jnp.broadcast_to(m + jnp.log(l), (rows, gw)), t, DIL_HEAD_DIM)
    return o, lse


def _sample_mixer_kernel(*refs, t):
    q_refs, cache_refs, new_refs, new_t_refs = refs[0:3], refs[3:6], refs[6:9], refs[9:12]
    qm_ref, mkv_ref = refs[12], refs[13]
    shifted_refs, mix_ref, win_refs = refs[14:17], refs[17], refs[18:21]
    gw = GROUP_WIDTH
    nb = mix_ref.shape[0] // t
    for bb in range(nb):
        rs = slice(bb * t, (bb + 1) * t)
        outs, lses = [], []
        col0 = (pl.program_id(0) * nb + bb) * t
        block = pl.multiple_of((col0 // LANES) * LANES, LANES)
        shift = LANES - t - col0 % LANES
        for g, (window, dil) in enumerate(DIL_PAIRS):
            o, lse = _dilated_sample_group(q_refs[g][rs, :], cache_refs[g][bb], new_refs[g][rs, :], window, dil)
            outs.append(o)
            lses.append(lse)
            win_refs[g][bb] = shifted_refs[g][bb]
            new_cols = pltpu.roll(new_t_refs[g][:, pl.ds(block, LANES)], shift, 1)
            win_refs[g][bb, :, LANES - t:] = new_cols[:, LANES - t:]
        m = jnp.maximum(jnp.maximum(lses[0], lses[1]), lses[2])
        es = [jnp.exp(v - m) for v in lses]
        inv = 1.0 / (es[0] + es[1] + es[2])
        for g in range(3):
            mix_ref[rs, g * gw:(g + 1) * gw] = outs[g] * (es[g] * inv)
        mix_ref[rs, MAIN_WIDTH:] = _mem_attn_tile(qm_ref[rs, :], mkv_ref[0, bb])


def _sample_mixer(q_g, caches_t, new_kv, new_kv_t, qm, memkv_t, shifted, batch, t):
    assert LANES % t == 0
    gw = GROUP_WIDTH
    nb = SAMPLE_BATCH_PER_STEP
    tok = lambda b: (b, 0)
    big = lambda b: (b, 0, 0)
    cache_specs = [pl.BlockSpec((nb,) + c.shape[1:], big) for c in caches_t]
    tail_specs = [pl.BlockSpec((nb, c.shape[1], LANES), lambda b, j=c.shape[2] // LANES - 1: (b, 0, j))
                  for c in shifted]
    first_shifted = 3 + len(caches_t) + 3 + 3 + 2
    return pl.pallas_call(
        functools.partial(_sample_mixer_kernel, t=t),
        grid=(batch // nb,),
        in_specs=[pl.BlockSpec((nb * t, gw), tok)] * 3 + cache_specs + [pl.BlockSpec((nb * t, 2 * gw), tok)] * 3
                 + [_full_spec(a) for a in new_kv_t]
                 + [pl.BlockSpec((nb * t, MEM_WIDTH), tok),
                    pl.BlockSpec((1, nb) + memkv_t.shape[2:], lambda b: (1, b, 0, 0))] + tail_specs,
        out_specs=[pl.BlockSpec((nb * t, D_MODEL), tok)] + tail_specs,
        out_shape=[jax.ShapeDtypeStruct((batch * t, D_MODEL), F32)]
                  + [jax.ShapeDtypeStruct(c.shape, F32) for c in shifted],
        input_output_aliases={first_shifted + g: 1 + g for g in range(len(shifted))},
        compiler_params=_params(("parallel",)),
        name="sample_mixer",
    )(*q_g, *caches_t, *new_kv, *new_kv_t, qm, memkv_t, *shifted)


def _natural_rows(ref, dil, scr):
    if dil == 1:
        return ref[0, 0].astype(F32)
    n = ref.shape[2]
    for r in range(dil):
        v = ref[0, r].astype(F32)
        for c in range(v.shape[1] // LANES):
            scr[c, pl.ds(r, n, stride=dil), :] = v[:, c * LANES:(c + 1) * LANES]
    return jnp.concatenate([scr[c] for c in range(scr.shape[0])], axis=1)


def _post_body(mix, x_ref, wo_ref, g1_ref, b1_ref, wi_ref, w2_ref, g2_ref, b2_ref, out_ref, act_ref):
    rows = x_ref.shape[0]
    x1 = _layer_norm(ALPHA * x_ref[...] + _dot(mix, wo_ref[0]), g1_ref[0], b1_ref[0])
    x1b = x1.astype(BF16)
    for c in range(0, FFN_HIDDEN, FFN_COL_CHUNK):
        gate = _dot(x1b, wi_ref[0, :, c:c + FFN_COL_CHUNK])
        up = _dot(x1b, wi_ref[0, :, FFN_HIDDEN + c:FFN_HIDDEN + c + FFN_COL_CHUNK])
        act_ref[0:rows, c:c + FFN_COL_CHUNK] = (_silu(gate) * up).astype(BF16)
    y = _dot(act_ref[0:rows, :], w2_ref[0])
    out_ref[...] = _layer_norm(ALPHA * x1 + y, g2_ref[0], b2_ref[0])


def _post_kernel(*refs, dils, n_mix_p, n_mix_s, n_p, n_shift, dec_seq, flip_blocks):
    n_flip = len(flip_blocks)
    mixp_refs, refs = refs[:n_mix_p], refs[n_mix_p:]
    xp_ref, refs = refs[0], refs[1:]
    mixs_refs, refs = refs[:n_mix_s], refs[n_mix_s:]
    xs_ref, refs = refs[0], refs[1:]
    weights, refs = refs[:7], refs[7:]
    caches, refs = refs[:n_shift], refs[n_shift:]
    flip_in, refs = refs[:n_flip], refs[n_flip:]
    outp_ref, outs_ref, refs = refs[0], refs[1], refs[2:]
    wins, refs = refs[:n_shift], refs[n_shift:]
    flip_out, refs = refs[:n_flip], refs[n_flip:]
    act_ref = refs[0]
    scrs = list(refs[1:])
    i = pl.program_id(0)

    @pl.when(i < n_p)
    def _():
        if dils is not None:
            o_refs, l_refs, mem_ref = mixp_refs[0:3], mixp_refs[3:6], mixp_refs[6]
            pool = list(scrs)
            nat = lambda r, d: _natural_rows(r, d, pool.pop(0) if d > 1 else None)
            os_ = [nat(o_refs[g], dils[g]) for g in range(3)]
            ls = [nat(l_refs[g], dils[g]) for g in range(3)]
            m = jnp.maximum(jnp.maximum(ls[0], ls[1]), ls[2])
            es = [jnp.exp(v - m) for v in ls]
            inv = 1.0 / (es[0] + es[1] + es[2])
            parts = [(os_[g] * (es[g] * inv)).astype(BF16) for g in range(3)]
            parts.append(mem_ref[...].astype(BF16))
        else:
            parts = [r[...].astype(BF16) for r in mixp_refs]
        mix = parts[0] if len(parts) == 1 else jnp.concatenate(parts, axis=1)
        _post_body(mix, xp_ref, *weights, outp_ref, act_ref)
        for c_ref, w_ref in zip(caches, wins):
            _shift_window(c_ref, w_ref, dec_seq)
        for src, dst in zip(flip_in, flip_out):
            dst[0] = jnp.transpose(src[...])

    @pl.when(i == n_p)
    def _():
        parts = [r[...].astype(BF16) for r in mixs_refs]
        mix = parts[0] if len(parts) == 1 else jnp.concatenate(parts, axis=1)
        _post_body(mix, xs_ref, *weights, outs_ref, act_ref)


def _post(layer, mix_p, xp, mix_s, xs, wo, g1, b1, wi, w2, g2, b2, dils, seq, caches_t=(), dec_seq=0,
          flips=()):
    tp, d = xp.shape
    tm = TOKEN_TILE
    n_p = tp // tm
    nt = seq // tm
    cl = lambda i: jnp.minimum(i, n_p - 1)
    row = lambda i: (cl(i), 0)
    if dils is None:
        mixp_specs = [pl.BlockSpec((tm, a.shape[1]), row) for a in mix_p]
        scratch = []
    else:
        res = lambda i: (cl(i) // nt, 0, cl(i) % nt, 0)
        mixp_specs = [pl.BlockSpec((1, dl, tm // dl, GROUP_WIDTH), res) for dl in dils] * 2
        mixp_specs.append(pl.BlockSpec((tm, MEM_WIDTH), row))
        scratch = [pltpu.VMEM((GROUP_WIDTH // LANES, tm, LANES), F32) for dl in dils * 2 if dl > 1]
    lsel = (layer, 0, 0)
    vec = _const_spec((1, 1, d), lsel)
    shift_specs = _shift_specs(caches_t, n_p)
    flip_in_specs, flip_out_specs, flip_shapes, flip_blocks = [], [], [], []
    for a, nbatch in flips:
        w = a.shape[0] // nbatch
        rb = min(w, tm)
        per = w // rb
        nblk = nbatch * per
        assert nblk <= n_p
        blk = lambda i, nblk=nblk: jnp.minimum(i, nblk - 1)
        flip_in_specs.append(pl.BlockSpec((rb, a.shape[1]), lambda i, blk=blk: (blk(i), 0)))
        flip_out_specs.append(pl.BlockSpec((1, a.shape[1], rb),
                                           lambda i, blk=blk, per=per: (blk(i) // per, 0, blk(i) % per)))
        flip_shapes.append(jax.ShapeDtypeStruct((nbatch, a.shape[1], w), F32))
        flip_blocks.append(nblk)
    return pl.pallas_call(
        functools.partial(_post_kernel, dils=dils, n_mix_p=len(mix_p), n_mix_s=len(mix_s), n_p=n_p,
                          n_shift=len(caches_t), dec_seq=dec_seq, flip_blocks=tuple(flip_blocks)),
        grid=(n_p + 1,),
        in_specs=mixp_specs + [pl.BlockSpec((tm, d), row)] + [_full_spec(a) for a in mix_s] + [_full_spec(xs)]
                 + [_const_spec((1,) + wo.shape[1:], lsel), vec, vec,
                    _const_spec((1,) + wi.shape[1:], lsel), _const_spec((1,) + w2.shape[1:], lsel), vec, vec]
                 + shift_specs + flip_in_specs,
        out_specs=[pl.BlockSpec((tm, d), row), pl.BlockSpec(xs.shape, lambda i: (0, 0))] + shift_specs
                  + flip_out_specs,
        out_shape=[jax.ShapeDtypeStruct((tp, d), F32), jax.ShapeDtypeStruct(xs.shape, F32)]
                  + [jax.ShapeDtypeStruct(c.shape, F32) for c in caches_t] + flip_shapes,
        scratch_shapes=[pltpu.VMEM((tm, FFN_HIDDEN), BF16)] + scratch,
        compiler_params=_params(("arbitrary",)),
        name="post" if dils is None else "post_combine",
    )(*mix_p, xp, *mix_s, xs, wo, g1, b1, wi, w2, g2, b2, *caches_t, *[a for a, _ in flips])


def _residue_major(ref, y, dil, scr):
    if dil == 1:
        ref[...] = y.astype(ref.dtype).reshape(ref.shape)
        return
    n = ref.shape[2]
    ncol = y.shape[1] // LANES
    for c in range(ncol):
        scr[c] = y[:, c * LANES:(c + 1) * LANES]
    for r in range(dil):
        rows = [scr[c, pl.ds(r, n, stride=dil), :] for c in range(ncol)]
        ref[0, r] = jnp.concatenate(rows, axis=1).astype(ref.dtype)


def _in_b_body(x_ref, wkv_ref, wq_ref, cos_ref, sin_ref, mkv_ref,
               q_refs, kv_refs, m_ref, win_refs, scrs, dils, win_preds):
    xb = x_ref[...].astype(BF16)
    cos = cos_ref[...]
    sin = sin_ref[...]
    gw = GROUP_WIDTH
    mw = MAIN_WIDTH
    half = DIL_HEAD_DIM // 2
    tm = x_ref.shape[0]
    pool = list(scrs)
    for g in range(3):
        ks = slice(g * gw, (g + 1) * gw)
        vs = slice(mw + g * gw, mw + (g + 1) * gw)
        k = _rope_cols(_dot(xb, wkv_ref[:, ks]), cos, sin, half)
        v = _dot(xb, wkv_ref[:, vs])
        kv = jnp.concatenate([k, v], axis=1)
        _residue_major(kv_refs[g], kv, dils[g], pool.pop(0) if dils[g] > 1 else None)
        q = _rope_cols(_dot(xb, wq_ref[0, :, ks]), cos, sin, half) * ATTN_SCALE
        _residue_major(q_refs[g], q, dils[g], pool.pop(0) if dils[g] > 1 else None)

        def window(g=g, kv=kv):
            if len(win_refs[g].shape) == 2:
                win_refs[g][...] = kv[tm - win_refs[g].shape[0]:, :]
            else:
                win_refs[g][0] = jnp.transpose(kv)

        if win_preds[g] is None:
            window()
        else:
            pl.when(win_preds[g])(window)
    qm = _dot(xb, wq_ref[0, :, mw:mw + MEM_WIDTH]) * ATTN_SCALE
    if mkv_ref is not None:
        m_ref[...] = _mem_attn_tile(qm.astype(BF16), mkv_ref[0, 0]).astype(m_ref.dtype)
    else:
        m_ref[...] = qm.astype(m_ref.dtype)


def _in_b_kernel(xp_ref, xs_ref, wkv_ref, wq_ref, cosp_ref, sinp_ref, coss_ref, sins_ref, mkv_ref, *refs,
                 dils, win_tiles, nt, n_p):
    outp, outs, scrs = refs[0:10], refs[10:20], refs[20:]
    i = pl.program_id(0)

    @pl.when(i < n_p)
    def _():
        _in_b_body(xp_ref, wkv_ref, wq_ref, cosp_ref, sinp_ref, mkv_ref,
                   outp[0:3], outp[3:6], outp[6], outp[7:10], scrs, dils, [None] * 3)

    @pl.when(i == n_p)
    def _():
        _in_b_body(xs_ref, wkv_ref, wq_ref, coss_ref, sins_ref, None,
                   outs[0:3], outs[3:6], outs[6], outs[7:10], (), (1, 1, 1), [None] * 3)


def _in_b(xp, xs, w_kv, w_q, tabs_p, tabs_s, memkv_t, batch, seq, dils, windows):
    tp, d = xp.shape
    ts = xs.shape[0]
    gw = GROUP_WIDTH
    tm = TOKEN_TILE
    n_p = tp // tm
    nt = seq // tm
    cl = lambda i: jnp.minimum(i, n_p - 1)
    row = lambda i: (cl(i), 0)
    tab = lambda i: (cl(i) % nt, 0)
    res = lambda i: (cl(i) // nt, 0, cl(i) % nt, 0)
    fix2 = lambda i: (0, 0)
    out_specs = ([pl.BlockSpec((1, dl, tm // dl, gw), res) for dl in dils]
                 + [pl.BlockSpec((1, dl, tm // dl, 2 * gw), res) for dl in dils]
                 + [pl.BlockSpec((tm, MEM_WIDTH), row)])
    out_shape = ([jax.ShapeDtypeStruct((batch, dl, seq // dl, gw), BF16) for dl in dils]
                 + [jax.ShapeDtypeStruct((batch, dl, seq // dl, 2 * gw), BF16) for dl in dils]
                 + [jax.ShapeDtypeStruct((tp, MEM_WIDTH), BF16)])
    win_tiles = []
    for w in windows:
        rb = min(w, tm)
        nblk = w // rb
        win_tiles.append(nblk)
        out_specs.append(pl.BlockSpec(
            (rb, 2 * gw), lambda i, nblk=nblk: ((cl(i) // nt) * nblk + jnp.maximum(cl(i) % nt - (nt - nblk), 0), 0)))
        out_shape.append(jax.ShapeDtypeStruct((batch * w, 2 * gw), F32))
    out_specs += ([pl.BlockSpec((ts, gw), fix2)] * 3 + [pl.BlockSpec((ts, 2 * gw), fix2)] * 3
                  + [pl.BlockSpec((ts, MEM_WIDTH), fix2)] + [pl.BlockSpec((1, 2 * gw, ts), lambda i: (0, 0, 0))] * 3)
    out_shape += ([jax.ShapeDtypeStruct((ts, gw), F32)] * 3 + [jax.ShapeDtypeStruct((ts, 2 * gw), F32)] * 3
                  + [jax.ShapeDtypeStruct((ts, MEM_WIDTH), F32)] + [jax.ShapeDtypeStruct((1, 2 * gw, ts), F32)] * 3)
    scratch = []
    for dl in dils:
        if dl > 1:
            scratch += [pltpu.VMEM((2 * gw // LANES, tm, LANES), F32), pltpu.VMEM((gw // LANES, tm, LANES), F32)]
    return pl.pallas_call(
        functools.partial(_in_b_kernel, dils=tuple(dils), win_tiles=tuple(win_tiles), nt=nt, n_p=n_p),
        grid=(n_p + 1,),
        in_specs=[pl.BlockSpec((tm, d), row), _full_spec(xs), _const_spec(w_kv.shape, (0, 0)),
                  _const_spec((1,) + w_q.shape[1:], (0, 0, 0)),
                  pl.BlockSpec((tm, LANES), tab), pl.BlockSpec((tm, LANES), tab),
                  _full_spec(tabs_s[0]), _full_spec(tabs_s[1]),
                  pl.BlockSpec((1, 1) + memkv_t.shape[2:], lambda i: (1, cl(i) // nt, 0, 0))],
        out_specs=out_specs,
        out_shape=out_shape,
        scratch_shapes=scratch,
        compiler_params=_params(("arbitrary",)),
        name="in_proj_b",
    )(xp, xs, w_kv, w_q, *tabs_p, *tabs_s, memkv_t)


def _dilated_kernel(q_ref, kv_ref, o_ref, lse_ref, ring_ref):
    i = pl.program_id(2)
    blk = DIL_BLOCK
    gw = GROUP_WIDTH
    nres = q_ref.shape[1]
    nsub = q_ref.shape[2] // blk
    slot = i & 1

    @pl.when(i == 0)
    def _():
        for r in range(nres):
            ring_ref[r, 1] = jnp.zeros(ring_ref.shape[2:], ring_ref.dtype)

    rows = GROUP_HEADS * blk
    qi = lax.broadcasted_iota(jnp.int32, (rows, 2 * blk), 0) & (blk - 1)
    kj = lax.broadcasted_iota(jnp.int32, (rows, 2 * blk), 1)
    delta = qi + blk - kj
    band = (delta >= 0) & (delta <= blk)
    for r in range(nres):
        cur = kv_ref[0, r]
        ring_ref[r, slot] = cur[(nsub - 1) * blk:]
        kext = jnp.concatenate([ring_ref[r, 1 - slot], cur], axis=0)
        for j in range(nsub):
            rs = slice(j * blk, (j + 1) * blk)
            q4 = _stack_heads(q_ref[0, r, rs, :], DIL_HEAD_DIM)
            kv = kext[j * blk:(j + 2) * blk]
            valid = band if j > 0 else band & ((i > 0) | (kj >= blk))
            s = jnp.where(valid, _dot_nt(q4, kv[:, :gw]), NEG_BIG)
            m = jnp.max(s, axis=-1, keepdims=True)
            e = jnp.exp(s - m)
            l = jnp.sum(e, axis=-1, keepdims=True)
            p = (e * (1.0 / l)).astype(BF16)
            full = _dot(p, kv[:, gw:])
            o_ref[0, r, rs, :] = _pick_heads(full, blk, DIL_HEAD_DIM).astype(o_ref.dtype)
            lse_ref[0, r, rs, :] = _pick_heads(jnp.broadcast_to(m + jnp.log(l), (rows, gw)), blk, DIL_HEAD_DIM)


def _dilated(q, kv, dil):
    batch, _, m, gw = q.shape
    nsub = min(DIL_BLOCKS_PER_STEP, m // DIL_BLOCK)
    nres = min(dil, DIL_BLOCKS_PER_STEP // nsub)
    rows = DIL_BLOCK * nsub
    idx = lambda b, r, i: (b, r, i, 0)
    return pl.pallas_call(
        _dilated_kernel,
        grid=(batch, dil // nres, m // rows),
        in_specs=[pl.BlockSpec((1, nres, rows, gw), idx), pl.BlockSpec((1, nres, rows, 2 * gw), idx)],
        out_specs=[pl.BlockSpec((1, nres, rows, gw), idx)] * 2,
        out_shape=[jax.ShapeDtypeStruct(q.shape, BF16), jax.ShapeDtypeStruct(q.shape, F32)],
        scratch_shapes=[pltpu.VMEM((nres, 2, DIL_BLOCK, 2 * gw), BF16)],
        compiler_params=_params(("arbitrary", "arbitrary", "arbitrary")),
        name=f"dilated_d{dil}",
    )(q, kv)


def _rope_tables(base, n):
    split = min(ROPE_SPLIT, n)
    hi = base + split * jnp.arange(n // split, dtype=F32)
    lo = jnp.arange(split, dtype=F32)
    inv = ROPE_THETA ** (-jnp.arange(0, RET_HEAD_DIM, 2, dtype=F32) / RET_HEAD_DIM)
    inv_h = inv[0::2]
    neg = lambda m: jnp.concatenate([-jnp.ones((m,), F32), jnp.ones((m,), F32)])

    def token_major(freq, sign):
        xa, xb = hi[:, None] * freq[None, :], lo[:, None] * freq[None, :]
        ca, sa, cb, sb = jnp.cos(xa), jnp.sin(xa), jnp.cos(xb), jnp.sin(xb)
        cos = ca[:, None, :] * cb[None, :, :] - sa[:, None, :] * sb[None, :, :]
        sin = (sa[:, None, :] * cb[None, :, :] + ca[:, None, :] * sb[None, :, :]) * sign
        return cos.reshape(n, -1), sin.reshape(n, -1)

    f_big = jnp.concatenate([inv, inv])
    f_half = jnp.concatenate([inv_h, inv_h])
    return (token_major(f_big, neg(RET_HEAD_DIM // 2)),
            token_major(jnp.tile(f_half, 2), jnp.tile(neg(DIL_HEAD_DIM // 2), 2)))


def _to_feature_major(x5):
    b, w = x5.shape[0], x5.shape[1]
    return jnp.transpose(x5, (0, 2, 3, 4, 1)).reshape(b, 2 * GROUP_WIDTH, w)


def _from_feature_major(xt):
    b, _, w = xt.shape
    return jnp.transpose(xt.reshape(b, 2, GROUP_HEADS, DIL_HEAD_DIM, w), (0, 4, 1, 2, 3))


def kernel(x_prompt, x_sample, mem_prompt, cache_mem_kv, state_ret, cache_win_kv_g1, cache_win_kv_g2, cache_win_kv_g3, w_in_a, w_in_b, w_out, w_kv_shared, w_mem_kv, ln_mix_g, ln_mix_b, ln_ffn_g, ln_ffn_b, w_ffn_in, w_ffn_out):
    batch, seq, d = x_prompt.shape
    dec_batch, dec_seq, _ = x_sample.shape
    n_mem = mem_prompt.shape[1]
    gw = GROUP_WIDTH
    ts = dec_batch * dec_seq
    win_caches = (cache_win_kv_g1, cache_win_kv_g2, cache_win_kv_g3)
    dils = tuple(dl for _, dl in DIL_PAIRS)

    tab_a_p, tab_b_p = _rope_tables(0.0, seq)
    tab_a_s, tab_b_s = _rope_tables(float(PAST_LEN), dec_seq)
    tab_a_s = tuple(jnp.tile(a, (dec_batch, 1)) for a in tab_a_s)
    tab_b_s = tuple(jnp.tile(a, (dec_batch, 1)) for a in tab_b_s)

    w_a = w_in_a.astype(BF16)
    w_kv = w_kv_shared.astype(BF16)
    w_q = w_in_b.astype(BF16)
    w_o = w_out.astype(BF16)
    w_fi = w_ffn_in.astype(BF16)
    w_fo = w_ffn_out.astype(BF16)
    w_mem_t = jnp.transpose(w_mem_kv, (0, 2, 1)).astype(BF16)
    lnv = lambda a: a.reshape(DEPTH, 1, d)
    ln = (lnv(ln_mix_g), lnv(ln_mix_b), lnv(ln_ffn_g), lnv(ln_ffn_b))

    memkv_p = _mem_proj(mem_prompt, w_mem_t)
    memkv_s = jnp.transpose(cache_mem_kv, (0, 1, 3, 4, 5, 2)).reshape(DEPTH, dec_batch, 2 * MEM_WIDTH, n_mem)

    xp = x_prompt.reshape(batch * seq, d)
    xs = x_sample.reshape(ts, d)

    def post(l, mix_p, mix_s, dls, shift=(), flips=()):
        return _post(l, mix_p, xp, mix_s, xs, w_o, ln[0], ln[1], w_fi, w_fo, ln[2], ln[3], dls, seq,
                     shift, dec_seq, flips)

    caches_t = [_to_feature_major(c) for c in win_caches]

    outs = _in_a(xp, xs, w_a, tab_a_p, tab_a_s, memkv_p, seq, caches_t[:2], dec_seq)
    q, k, v, g, mem_o = outs[:5]
    mix_p, state_p = _retention(q, k, v, g, batch, seq)
    q, k, v, g, qm = outs[5:10]
    shifted = list(outs[10:])
    mix_s, state_s = _retention_sample(q, k, v, g, qm, memkv_s, state_ret, dec_batch, dec_seq)
    xp, xs, *shifted_big = post(0, [mix_p, mem_o], [mix_s], None, caches_t[2:])
    shifted += shifted_big

    windows_p = tuple(min(w, seq) for w, _ in DIL_PAIRS)
    outs = _in_b(xp, xs, w_kv, w_q, tab_b_p, tab_b_s, memkv_p, batch, seq, dils, windows_p)
    q_g, kv_g, mem_o, win_p = outs[0:3], outs[3:6], outs[6], outs[7:10]
    att = [_dilated(q_g[i], kv_g[i], dils[i]) for i in range(3)]

    q_s, kv_new, qm, new_t = outs[10:13], outs[13:16], outs[16], outs[17:20]
    new_t = [a.reshape(2 * gw, ts) for a in new_t]
    mix_s, *win_s = _sample_mixer(q_s, caches_t, kv_new, new_t, qm, memkv_s, shifted, dec_batch, dec_seq)

    xp, xs, *win_p = post(1, [a[0] for a in att] + [a[1] for a in att] + [mem_o], [mix_s], dils,
                          flips=[(w, batch) for w in win_p])

    memkv_out = jnp.transpose(memkv_p.reshape(DEPTH, batch, 2, MEM_HEADS, MEM_HEAD_DIM, n_mem), (0, 1, 5, 2, 3, 4))
    return (xp.reshape(batch, seq, d), xs.reshape(dec_batch, dec_seq, d),
            state_p[None], state_s[None], memkv_out,
            _from_feature_major(win_p[0]), _from_feature_major(win_p[1]), _from_feature_major(win_p[2]),
            _from_feature_major(win_s[0]), _from_feature_major(win_s[1]), _from_feature_major(win_s[2]))
```

```python
import functools
import math

import jax
import jax.numpy as jnp
from jax import lax
from jax.experimental import pallas as pl
from jax.experimental.pallas import tpu as pltpu

F32 = jnp.float32
BF16 = jnp.bfloat16

D_MODEL = 1024
MEM_HEADS = 4
MEM_HEAD_DIM = 64
MEM_WIDTH = MEM_HEADS * MEM_HEAD_DIM
MAIN_WIDTH = D_MODEL - MEM_WIDTH
RET_HEADS = 6
RET_HEAD_DIM = MAIN_WIDTH // RET_HEADS
RET_CHUNK = 128
RET_CHUNKS_PER_STEP = 16
DIL_PAIRS = ((128, 1), (512, 4), (2048, 16))
GROUP_HEADS = 4
DIL_HEAD_DIM = 64
GROUP_WIDTH = GROUP_HEADS * DIL_HEAD_DIM
DIL_BLOCK = 128
DIL_BLOCKS_PER_STEP = 32
FFN_HIDDEN = 2816
ROPE_THETA = 10000.0
ROPE_SPLIT = 64
LN_EPS = 1e-5
DEPTH = 2
ALPHA = (2 * DEPTH) ** 0.25
PAST_LEN = 8192
NEG_BIG = -1e30
ATTN_SCALE = DIL_HEAD_DIM ** -0.5

LANES = 128
BF16_SUBLANES = 16
SAMPLE_BATCH_PER_STEP = 2
VMEM_LIMIT = 56 * 1024 * 1024
FFN_COL_CHUNK = 256
TOKEN_TILE = 512
IN_A_TILE = 1024

LOG_G = tuple(math.log1p(-(2.0 ** (-5.0 - h))) for h in range(RET_HEADS))


def _dot(a, b):
    return jnp.dot(a, b, preferred_element_type=F32)


def _dot_nt(a, b):
    return lax.dot_general(a, b, (((1,), (1,)), ((), ())), preferred_element_type=F32)


def _dot_tn(a, b):
    return lax.dot_general(a, b, (((0,), (0,)), ((), ())), preferred_element_type=F32)


def _silu(x):
    return x / (1.0 + jnp.exp(-x))


def _normalize(z):
    zs = z - z[:, 0:1]
    m1 = jnp.mean(zs, axis=-1, keepdims=True)
    m2 = jnp.mean(zs * zs, axis=-1, keepdims=True)
    return (zs - m1) * lax.rsqrt(m2 - m1 * m1 + LN_EPS)


def _layer_norm(z, g, b):
    return _normalize(z) * g + b


def _rope_lanes(y, cos, sin_signed, half):
    if 2 * half == LANES:
        partner = pltpu.roll(y, half, 1)
    else:
        lane = lax.broadcasted_iota(jnp.int32, y.shape, 1)
        first = (lane & (2 * half - 1)) < half
        partner = jnp.where(first, pltpu.roll(y, LANES - half, 1), pltpu.roll(y, half, 1))
    return y * cos + partner * sin_signed


def _rope_cols(y, cos, sin_signed, half):
    parts = [_rope_lanes(y[:, j:j + LANES], cos, sin_signed, half) for j in range(0, y.shape[1], LANES)]
    return parts[0] if len(parts) == 1 else jnp.concatenate(parts, axis=1)


def _head_mask(shape, h, width):
    lane = lax.broadcasted_iota(jnp.int32, shape, len(shape) - 1)
    return (lane >= h * width) & (lane < (h + 1) * width)


def _stack_heads(q, width):
    zero = jnp.zeros((), q.dtype)
    parts = [jnp.where(_head_mask(q.shape, h, width), q, zero) for h in range(q.shape[1] // width)]
    return jnp.concatenate(parts, axis=0).astype(BF16)


def _pick_heads(full, t, width):
    per = LANES // width
    cols = []
    for c in range(full.shape[1] // LANES):
        out = None
        for i in range(per):
            h = c * per + i
            blk = full[h * t:(h + 1) * t, c * LANES:(c + 1) * LANES]
            out = blk if out is None else jnp.where(_head_mask(blk.shape, i, width), blk, out)
        cols.append(out)
    return cols[0] if len(cols) == 1 else jnp.concatenate(cols, axis=1)


def _const_spec(block, index):
    return pl.BlockSpec(block, lambda *_: index, pipeline_mode=pl.Buffered(1))


def _full_spec(a):
    return pl.BlockSpec(a.shape, lambda *_: (0,) * a.ndim)


def _params(sem):
    return pltpu.CompilerParams(dimension_semantics=sem, vmem_limit_bytes=VMEM_LIMIT)


def _mem_proj_kernel(m_ref, w_ref, o_ref):
    o_ref[0, 0] = _dot_nt(w_ref[0], m_ref[0].astype(BF16))


def _mem_proj(mem, w_t):
    batch, n_mem, d = mem.shape
    depth, n, _ = w_t.shape
    return pl.pallas_call(
        _mem_proj_kernel,
        grid=(depth, batch),
        in_specs=[pl.BlockSpec((1, n_mem, d), lambda l, b: (b, 0, 0)),
                  pl.BlockSpec((1, n, d), lambda l, b: (l, 0, 0))],
        out_specs=pl.BlockSpec((1, 1, n, n_mem), lambda l, b: (l, b, 0, 0)),
        out_shape=jax.ShapeDtypeStruct((depth, batch, n, n_mem), F32),
        compiler_params=_params(("arbitrary", "arbitrary")),
        name="mem_proj",
    )(mem, w_t)


def _mem_attn_tile(qm, kv_t):
    t = qm.shape[0]
    q4 = _stack_heads(qm, MEM_HEAD_DIM)
    k_t = kv_t[:MEM_WIDTH].astype(BF16)
    v_t = kv_t[MEM_WIDTH:].astype(BF16)
    s = _dot(q4, k_t)
    m = jnp.max(s, axis=-1, keepdims=True)
    e = jnp.exp(s - m)
    p = (e / jnp.sum(e, axis=-1, keepdims=True)).astype(BF16)
    return _pick_heads(_dot_nt(p, v_t), t, MEM_HEAD_DIM)


def _in_a_body(x_ref, w_ref, cos_ref, sin_ref, mkv_ref, q_ref, k_ref, v_ref, g_ref, m_ref):
    xb = x_ref[...].astype(BF16)
    cos = cos_ref[...]
    sin = sin_ref[...]
    mw = MAIN_WIDTH
    half = RET_HEAD_DIM // 2
    q = _rope_cols(_dot(xb, w_ref[0, :, 0:mw]), cos, sin, half)
    q_ref[...] = q.astype(q_ref.dtype)
    k = _rope_cols(_dot(xb, w_ref[0, :, mw:2 * mw]), cos, sin, half) * (RET_HEAD_DIM ** -0.5)
    k_ref[...] = k.astype(k_ref.dtype)
    v_ref[...] = _dot(xb, w_ref[0, :, 2 * mw:3 * mw]).astype(v_ref.dtype)
    g_ref[...] = _silu(_dot(xb, w_ref[0, :, 3 * mw:4 * mw])).astype(g_ref.dtype)
    qm = _dot(xb, w_ref[0, :, 4 * mw:4 * mw + MEM_WIDTH]) * ATTN_SCALE
    if mkv_ref is not None:
        m_ref[...] = _mem_attn_tile(qm.astype(BF16), mkv_ref[0, 0]).astype(m_ref.dtype)
    else:
        m_ref[...] = qm.astype(m_ref.dtype)


def _shift_window(cache_ref, win_ref, t):
    n = cache_ref.shape[2]
    for e in range(cache_ref.shape[0]):
        win_ref[e, :, 0:n - t] = cache_ref[e, :, t:n]
        win_ref[e, :, n - t:n] = jnp.zeros((cache_ref.shape[1], t), F32)


def _shift_specs(caches_t, n_steps):
    if not caches_t:
        return []
    nbatch = caches_t[0].shape[0]
    per = -(-nbatch // n_steps)
    assert nbatch % per == 0
    last = nbatch // per - 1
    return [pl.BlockSpec((per,) + c.shape[1:], lambda i: (jnp.minimum(i, last), 0, 0)) for c in caches_t]


def _in_a_kernel(xp_ref, xs_ref, w_ref, cosp_ref, sinp_ref, coss_ref, sins_ref, mkv_ref, *refs,
                 n_p, n_shift, dec_seq):
    caches, outs, wins = refs[:n_shift], refs[n_shift:n_shift + 10], refs[n_shift + 10:]
    i = pl.program_id(0)

    @pl.when(i < n_p)
    def _():
        _in_a_body(xp_ref, w_ref, cosp_ref, sinp_ref, mkv_ref, *outs[:5])
        for c_ref, w_ref_ in zip(caches, wins):
            _shift_window(c_ref, w_ref_, dec_seq)

    @pl.when(i == n_p)
    def _():
        _in_a_body(xs_ref, w_ref, coss_ref, sins_ref, None, *outs[5:])


def _in_a(xp, xs, w_bf, tab_p, tab_s, memkv_t, seq, caches_t, dec_seq):
    tp, d = xp.shape
    ts = xs.shape[0]
    tm = IN_A_TILE
    n_p = tp // tm
    nt = seq // tm
    cl = lambda i: jnp.minimum(i, n_p - 1)
    row = lambda i: (cl(i), 0)
    tab = lambda i: (cl(i) % nt, 0)
    widths = (MAIN_WIDTH,) * 4 + (MEM_WIDTH,)
    shift_specs = _shift_specs(caches_t, n_p)
    return pl.pallas_call(
        functools.partial(_in_a_kernel, n_p=n_p, n_shift=len(caches_t), dec_seq=dec_seq),
        grid=(n_p + 1,),
        in_specs=[pl.BlockSpec((tm, d), row), _full_spec(xs), _const_spec((1,) + w_bf.shape[1:], (0, 0, 0)),
                  pl.BlockSpec((tm, LANES), tab), pl.BlockSpec((tm, LANES), tab),
                  _full_spec(tab_s[0]), _full_spec(tab_s[1]),
                  pl.BlockSpec((1, 1) + memkv_t.shape[2:], lambda i: (0, cl(i) // nt, 0, 0))] + shift_specs,
        out_specs=[pl.BlockSpec((tm, w), row) for w in widths]
                  + [pl.BlockSpec((ts, w), lambda i: (0, 0)) for w in widths] + shift_specs,
        out_shape=[jax.ShapeDtypeStruct((tp, w), BF16) for w in widths]
                  + [jax.ShapeDtypeStruct((ts, w), F32) for w in widths]
                  + [jax.ShapeDtypeStruct(c.shape, F32) for c in caches_t],
        compiler_params=_params(("arbitrary",)),
        name="in_proj_a",
    )(xp, xs, w_bf, *tab_p, *tab_s, memkv_t, *caches_t)


def _retention_kernel(q_ref, k_ref, v_ref, g_ref, mix_ref, st_ref, dec_ref, rdec_ref, kdec_ref,
                      inner_ref, kv_ref):
    c = pl.program_id(1)
    cs = RET_CHUNK

    @pl.when(c == 0)
    def _():
        st_ref[...] = jnp.zeros_like(st_ref)
        row = lax.broadcasted_iota(jnp.int32, (cs, cs), 0).astype(F32)
        col = lax.broadcasted_iota(jnp.int32, (cs, cs), 1).astype(F32)
        diff = row - col
        for h in range(RET_HEADS):
            lg = LOG_G[h]
            dec_ref[h] = jnp.where(diff >= 0, jnp.exp(jnp.maximum(diff, 0.0) * lg), 0.0)
            rdec_ref[h] = jnp.exp((row + 1.0) * lg)
            kdec_ref[h] = jnp.exp((cs - 1.0 - row) * lg)

    nj = q_ref.shape[0] // cs
    for j in range(nj):
        rs = slice(j * cs, (j + 1) * cs)
        for h in range(RET_HEADS):
            hs = slice(h * RET_HEAD_DIM, (h + 1) * RET_HEAD_DIM)
            kh = k_ref[rs, hs]
            vh = v_ref[rs, hs]
            s = _dot_nt(q_ref[rs, hs], kh) * dec_ref[h]
            inner_ref[j, h] = _dot(s.astype(BF16), vh)
            kd = (kh.astype(F32) * kdec_ref[h]).astype(BF16)
            kv_ref[j, h] = _dot_tn(kd, vh)
    for j in range(nj):
        rs = slice(j * cs, (j + 1) * cs)
        for h in range(RET_HEADS):
            hs = slice(h * RET_HEAD_DIM, (h + 1) * RET_HEAD_DIM)
            st = st_ref[0, h]
            cross = _dot(q_ref[rs, hs], st.astype(BF16)) * rdec_ref[h]
            st_ref[0, h] = math.exp(cs * LOG_G[h]) * st + kv_ref[j, h]
            o = inner_ref[j, h] + cross
            mu = jnp.mean(o, axis=-1, keepdims=True)
            oc = o - mu
            var = jnp.mean(oc * oc, axis=-1, keepdims=True)
            on = oc * lax.rsqrt(var + LN_EPS)
            mix_ref[rs, hs] = (g_ref[rs, hs].astype(F32) * on).astype(mix_ref.dtype)


def _retention(q, k, v, g, batch, seq):
    t = q.shape[0]
    rows = RET_CHUNK * RET_CHUNKS_PER_STEP
    ns = seq // rows
    tok = lambda b, c: (b * ns + c, 0)
    sq = (RET_HEADS, RET_CHUNK, RET_CHUNK)
    return pl.pallas_call(
        _retention_kernel,
        grid=(batch, ns),
        in_specs=[pl.BlockSpec((rows, MAIN_WIDTH), tok)] * 4,
        out_specs=[pl.BlockSpec((rows, MAIN_WIDTH), tok),
                   pl.BlockSpec((1, RET_HEADS, RET_HEAD_DIM, RET_HEAD_DIM), lambda b, c: (b, 0, 0, 0))],
        out_shape=[jax.ShapeDtypeStruct((t, MAIN_WIDTH), BF16),
                   jax.ShapeDtypeStruct((batch, RET_HEADS, RET_HEAD_DIM, RET_HEAD_DIM), F32)],
        scratch_shapes=[pltpu.VMEM(sq, F32)] * 3 + [pltpu.VMEM((RET_CHUNKS_PER_STEP,) + sq, F32)] * 2,
        compiler_params=_params(("arbitrary", "arbitrary")),
        name="retention",
    )(q, k, v, g)


def _retention_sample_kernel(q_ref, k_ref, v_ref, g_ref, qm_ref, mkv_ref, st_ref, mix_ref, nst_ref, *, t):
    nb = q_ref.shape[0] // t
    pad_k = jnp.zeros((LANES - t, RET_HEAD_DIM), F32)
    pad_q = jnp.zeros((BF16_SUBLANES - t, RET_HEAD_DIM), F32)
    lhs = lambda x: jnp.concatenate([x, pad_q], axis=0).astype(BF16)
    row = lax.broadcasted_iota(jnp.int32, (t, LANES), 0).astype(F32)
    col = lax.broadcasted_iota(jnp.int32, (t, LANES), 1).astype(F32)
    prow = lax.broadcasted_iota(jnp.int32, (LANES, RET_HEAD_DIM), 0).astype(F32)
    diff = row - col
    for h in range(RET_HEADS):
        lg = LOG_G[h]
        hs = slice(h * RET_HEAD_DIM, (h + 1) * RET_HEAD_DIM)
        dec = jnp.where(diff >= 0, jnp.exp(jnp.maximum(diff, 0.0) * lg), 0.0)
        rdec = jnp.exp((row + 1.0) * lg)
        kdec = jnp.exp((t - 1.0 - prow) * lg)
        for bb in range(nb):
            rs = slice(bb * t, (bb + 1) * t)
            qh = lhs(q_ref[rs, hs])
            kp = jnp.concatenate([k_ref[rs, hs], pad_k], axis=0)
            vp = jnp.concatenate([v_ref[rs, hs], pad_k], axis=0).astype(BF16)
            st = st_ref[0, bb, h]
            inner = _dot(lhs(_dot_nt(qh, kp.astype(BF16))[:t] * dec), vp)[:t]
            cross = _dot(qh, st.astype(BF16))[:t] * rdec
            nst_ref[bb, h] = math.exp(t * lg) * st + _dot_tn((kp * kdec).astype(BF16), vp)
            o = inner + cross
            mu = jnp.mean(o, axis=-1, keepdims=True)
            oc = o - mu
            var = jnp.mean(oc * oc, axis=-1, keepdims=True)
            on = oc * lax.rsqrt(var + LN_EPS)
            mix_ref[rs, hs] = g_ref[rs, hs] * on
    for bb in range(nb):
        rs = slice(bb * t, (bb + 1) * t)
        mix_ref[rs, MAIN_WIDTH:] = _mem_attn_tile(qm_ref[rs, :], mkv_ref[0, bb])


def _retention_sample(q, k, v, g, qm, memkv_t, state, batch, t):
    nb = SAMPLE_BATCH_PER_STEP
    tok = lambda b: (b, 0)
    hd = (RET_HEADS, RET_HEAD_DIM, RET_HEAD_DIM)
    return pl.pallas_call(
        functools.partial(_retention_sample_kernel, t=t),
        grid=(batch // nb,),
        in_specs=[pl.BlockSpec((nb * t, MAIN_WIDTH), tok)] * 4
                 + [pl.BlockSpec((nb * t, MEM_WIDTH), tok),
                    pl.BlockSpec((1, nb) + memkv_t.shape[2:], lambda b: (0, b, 0, 0)),
                    pl.BlockSpec((1, nb) + hd, lambda b: (0, b, 0, 0, 0))],
        out_specs=[pl.BlockSpec((nb * t, D_MODEL), tok), pl.BlockSpec((nb,) + hd, lambda b: (b, 0, 0, 0))],
        out_shape=[jax.ShapeDtypeStruct((batch * t, D_MODEL), F32),
                   jax.ShapeDtypeStruct((batch,) + hd, F32)],
        compiler_params=_params(("parallel",)),
        name="retention_sample",
    )(q, k, v, g, qm, memkv_t, state)


def _dilated_sample_group(q, cache, new, window, dil):
    t = q.shape[0]
    n_buf = cache.shape[1]
    gw = GROUP_WIDTH
    q4 = _stack_heads(q, DIL_HEAD_DIM)
    rows = q4.shape[0]
    newp = jnp.concatenate([new, jnp.zeros((LANES - t, new.shape[1]), F32)], axis=0).astype(BF16)

    def masked(s, first_index):
        key = lax.broadcasted_iota(jnp.int32, s.shape, 1) + first_index
        tok = lax.broadcasted_iota(jnp.int32, s.shape, 0) & (t - 1)
        delta = n_buf + tok - key
        valid = (delta >= 0) & (delta <= window) & ((delta & (dil - 1)) == 0)
        return jnp.where(valid, s, NEG_BIG)

    s_c = masked(_dot(q4, cache[:gw].astype(BF16)), 0)
    s_n = masked(_dot_nt(q4, newp[:, :gw]), n_buf)
    m = jnp.maximum(jnp.max(s_c, axis=-1, keepdims=True), jnp.max(s_n, axis=-1, keepdims=True))
    e_c = jnp.exp(s_c - m)
    e_n = jnp.exp(s_n - m)
    l = jnp.sum(e_c, axis=-1, keepdims=True) + jnp.sum(e_n, axis=-1, keepdims=True)
    inv = 1.0 / l
    full = (_dot_nt((e_c * inv).astype(BF16), cache[gw:].astype(BF16))
            + _dot((e_n * inv).astype(BF16), newp[:, gw:]))
    o = _pick_heads(full, t, DIL_HEAD_DIM)
    lse = _pick_heads(jnp.broadcast_to(m + jnp.log(l), (rows, gw)), t, DIL_HEAD_DIM)
    return o, lse


def _sample_mixer_kernel(*refs, t):
    q_refs, cache_refs, new_refs, new_t_refs = refs[0:3], refs[3:6], refs[6:9], refs[9:12]
    qm_ref, mkv_ref = refs[12], refs[13]
    shifted_refs, mix_ref, win_refs = refs[14:17], refs[17], refs[18:21]
    gw = GROUP_WIDTH
    nb = mix_ref.shape[0] // t
    for bb in range(nb):
        rs = slice(bb * t, (bb + 1) * t)
        outs, lses = [], []
        col0 = (pl.program_id(0) * nb + bb) * t
        block = pl.multiple_of((col0 // LANES) * LANES, LANES)
        shift = LANES - t - col0 % LANES
        for g, (window, dil) in enumerate(DIL_PAIRS):
            o, lse = _dilated_sample_group(q_refs[g][rs, :], cache_refs[g][bb], new_refs[g][rs, :], window, dil)
            outs.append(o)
            lses.append(lse)
            win_refs[g][bb] = shifted_refs[g][bb]
            new_cols = pltpu.roll(new_t_refs[g][:, pl.ds(block, LANES)], shift, 1)
            win_refs[g][bb, :, LANES - t:] = new_cols[:, LANES - t:]
        m = jnp.maximum(jnp.maximum(lses[0], lses[1]), lses[2])
        es = [jnp.exp(v - m) for v in lses]
        inv = 1.0 / (es[0] + es[1] + es[2])
        for g in range(3):
            mix_ref[rs, g * gw:(g + 1) * gw] = outs[g] * (es[g] * inv)
        mix_ref[rs, MAIN_WIDTH:] = _mem_attn_tile(qm_ref[rs, :], mkv_ref[0, bb])


def _sample_mixer(q_g, caches_t, new_kv, new_kv_t, qm, memkv_t, shifted, batch, t):
    assert LANES % t == 0
    gw = GROUP_WIDTH
    nb = SAMPLE_BATCH_PER_STEP
    tok = lambda b: (b, 0)
    big = lambda b: (b, 0, 0)
    cache_specs = [pl.BlockSpec((nb,) + c.shape[1:], big) for c in caches_t]
    tail_specs = [pl.BlockSpec((nb, c.shape[1], LANES), lambda b, j=c.shape[2] // LANES - 1: (b, 0, j))
                  for c in shifted]
    first_shifted = 3 + len(caches_t) + 3 + 3 + 2
    return pl.pallas_call(
        functools.partial(_sample_mixer_kernel, t=t),
        grid=(batch // nb,),
        in_specs=[pl.BlockSpec((nb * t, gw), tok)] * 3 + cache_specs + [pl.BlockSpec((nb * t, 2 * gw), tok)] * 3
                 + [_full_spec(a) for a in new_kv_t]
                 + [pl.BlockSpec((nb * t, MEM_WIDTH), tok),
                    pl.BlockSpec((1, nb) + memkv_t.shape[2:], lambda b: (1, b, 0, 0))] + tail_specs,
        out_specs=[pl.BlockSpec((nb * t, D_MODEL), tok)] + tail_specs,
        out_shape=[jax.ShapeDtypeStruct((batch * t, D_MODEL), F32)]
                  + [jax.ShapeDtypeStruct(c.shape, F32) for c in shifted],
        input_output_aliases={first_shifted + g: 1 + g for g in range(len(shifted))},
        compiler_params=_params(("parallel",)),
        name="sample_mixer",
    )(*q_g, *caches_t, *new_kv, *new_kv_t, qm, memkv_t, *shifted)


def _natural_rows(ref, dil, scr):
    if dil == 1:
        return ref[0, 0].astype(F32)
    n = ref.shape[2]
    for r in range(dil):
        v = ref[0, r].astype(F32)
        for c in range(v.shape[1] // LANES):
            scr[c, pl.ds(r, n, stride=dil), :] = v[:, c * LANES:(c + 1) * LANES]
    return jnp.concatenate([scr[c] for c in range(scr.shape[0])], axis=1)


def _post_body(mix, x_ref, wo_ref, g1_ref, b1_ref, wi_ref, w2_ref, g2_ref, b2_ref, out_ref, act_ref):
    rows = x_ref.shape[0]
    x1 = _layer_norm(ALPHA * x_ref[...] + _dot(mix, wo_ref[0]), g1_ref[0], b1_ref[0])
    x1b = x1.astype(BF16)
    for c in range(0, FFN_HIDDEN, FFN_COL_CHUNK):
        gate = _dot(x1b, wi_ref[0, :, c:c + FFN_COL_CHUNK])
        up = _dot(x1b, wi_ref[0, :, FFN_HIDDEN + c:FFN_HIDDEN + c + FFN_COL_CHUNK])
        act_ref[0:rows, c:c + FFN_COL_CHUNK] = (_silu(gate) * up).astype(BF16)
    y = _dot(act_ref[0:rows, :], w2_ref[0])
    out_ref[...] = _layer_norm(ALPHA * x1 + y, g2_ref[0], b2_ref[0])


def _post_kernel(*refs, dils, n_mix_p, n_mix_s, n_p, n_shift, dec_seq, flip_blocks):
    n_flip = len(flip_blocks)
    mixp_refs, refs = refs[:n_mix_p], refs[n_mix_p:]
    xp_ref, refs = refs[0], refs[1:]
    mixs_refs, refs = refs[:n_mix_s], refs[n_mix_s:]
    xs_ref, refs = refs[0], refs[1:]
    weights, refs = refs[:7], refs[7:]
    caches, refs = refs[:n_shift], refs[n_shift:]
    flip_in, refs = refs[:n_flip], refs[n_flip:]
    outp_ref, outs_ref, refs = refs[0], refs[1], refs[2:]
    wins, refs = refs[:n_shift], refs[n_shift:]
    flip_out, refs = refs[:n_flip], refs[n_flip:]
    act_ref = refs[0]
    scrs = list(refs[1:])
    i = pl.program_id(0)

    @pl.when(i < n_p)
    def _():
        if dils is not None:
            o_refs, l_refs, mem_ref = mixp_refs[0:3], mixp_refs[3:6], mixp_refs[6]
            pool = list(scrs)
            nat = lambda r, d: _natural_rows(r, d, pool.pop(0) if d > 1 else None)
            os_ = [nat(o_refs[g], dils[g]) for g in range(3)]
            ls = [nat(l_refs[g], dils[g]) for g in range(3)]
            m = jnp.maximum(jnp.maximum(ls[0], ls[1]), ls[2])
            es = [jnp.exp(v - m) for v in ls]
            inv = 1.0 / (es[0] + es[1] + es[2])
            parts = [(os_[g] * (es[g] * inv)).astype(BF16) for g in range(3)]
            parts.append(mem_ref[...].astype(BF16))
        else:
            parts = [r[...].astype(BF16) for r in mixp_refs]
        mix = parts[0] if len(parts) == 1 else jnp.concatenate(parts, axis=1)
        _post_body(mix, xp_ref, *weights, outp_ref, act_ref)
        for c_ref, w_ref in zip(caches, wins):
            _shift_window(c_ref, w_ref, dec_seq)
        for src, dst in zip(flip_in, flip_out):
            dst[0] = jnp.transpose(src[...])

    @pl.when(i == n_p)
    def _():
        parts = [r[...].astype(BF16) for r in mixs_refs]
        mix = parts[0] if len(parts) == 1 else jnp.concatenate(parts, axis=1)
        _post_body(mix, xs_ref, *weights, outs_ref, act_ref)


def _post(layer, mix_p, xp, mix_s, xs, wo, g1, b1, wi, w2, g2, b2, dils, seq, caches_t=(), dec_seq=0,
          flips=()):
    tp, d = xp.shape
    tm = TOKEN_TILE
    n_p = tp // tm
    nt = seq // tm
    cl = lambda i: jnp.minimum(i, n_p - 1)
    row = lambda i: (cl(i), 0)
    if dils is None:
        mixp_specs = [pl.BlockSpec((tm, a.shape[1]), row) for a in mix_p]
        scratch = []
    else:
        res = lambda i: (cl(i) // nt, 0, cl(i) % nt, 0)
        mixp_specs = [pl.BlockSpec((1, dl, tm // dl, GROUP_WIDTH), res) for dl in dils] * 2
        mixp_specs.append(pl.BlockSpec((tm, MEM_WIDTH), row))
        scratch = [pltpu.VMEM((GROUP_WIDTH // LANES, tm, LANES), F32) for dl in dils * 2 if dl > 1]
    lsel = (layer, 0, 0)
    vec = _const_spec((1, 1, d), lsel)
    shift_specs = _shift_specs(caches_t, n_p)
    flip_in_specs, flip_out_specs, flip_shapes, flip_blocks = [], [], [], []
    for a, nbatch in flips:
        w = a.shape[0] // nbatch
        rb = min(w, tm)
        per = w // rb
        nblk = nbatch * per
        assert nblk <= n_p
        blk = lambda i, nblk=nblk: jnp.minimum(i, nblk - 1)
        flip_in_specs.append(pl.BlockSpec((rb, a.shape[1]), lambda i, blk=blk: (blk(i), 0)))
        flip_out_specs.append(pl.BlockSpec((1, a.shape[1], rb),
                                           lambda i, blk=blk, per=per: (blk(i) // per, 0, blk(i) % per)))
        flip_shapes.append(jax.ShapeDtypeStruct((nbatch, a.shape[1], w), F32))
        flip_blocks.append(nblk)
    return pl.pallas_call(
        functools.partial(_post_kernel, dils=dils, n_mix_p=len(mix_p), n_mix_s=len(mix_s), n_p=n_p,
                          n_shift=len(caches_t), dec_seq=dec_seq, flip_blocks=tuple(flip_blocks)),
        grid=(n_p + 1,),
        in_specs=mixp_specs + [pl.BlockSpec((tm, d), row)] + [_full_spec(a) for a in mix_s] + [_full_spec(xs)]
                 + [_const_spec((1,) + wo.shape[1:], lsel), vec, vec,
                    _const_spec((1,) + wi.shape[1:], lsel), _const_spec((1,) + w2.shape[1:], lsel), vec, vec]
                 + shift_specs + flip_in_specs,
        out_specs=[pl.BlockSpec((tm, d), row), pl.BlockSpec(xs.shape, lambda i: (0, 0))] + shift_specs
                  + flip_out_specs,
        out_shape=[jax.ShapeDtypeStruct((tp, d), F32), jax.ShapeDtypeStruct(xs.shape, F32)]
                  + [jax.ShapeDtypeStruct(c.shape, F32) for c in caches_t] + flip_shapes,
        scratch_shapes=[pltpu.VMEM((tm, FFN_HIDDEN), BF16)] + scratch,
        compiler_params=_params(("arbitrary",)),
        name="post" if dils is None else "post_combine",
    )(*mix_p, xp, *mix_s, xs, wo, g1, b1, wi, w2, g2, b2, *caches_t, *[a for a, _ in flips])


def _residue_major(ref, y, dil, scr):
    if dil == 1:
        ref[...] = y.astype(ref.dtype).reshape(ref.shape)
        return
    n = ref.shape[2]
    ncol = y.shape[1] // LANES
    for c in range(ncol):
        scr[c] = y[:, c * LANES:(c + 1) * LANES]
    for r in range(dil):
        rows = [scr[c, pl.ds(r, n, stride=dil), :] for c in range(ncol)]
        ref[0, r] = jnp.concatenate(rows, axis=1).astype(ref.dtype)


def _in_b_body(x_ref, wkv_ref, wq_ref, cos_ref, sin_ref, mkv_ref,
               q_refs, kv_refs, m_ref, win_refs, scrs, dils, win_preds):
    xb = x_ref[...].astype(BF16)
    cos = cos_ref[...]
    sin = sin_ref[...]
    gw = GROUP_WIDTH
    mw = MAIN_WIDTH
    half = DIL_HEAD_DIM // 2
    tm = x_ref.shape[0]
    pool = list(scrs)
    for g in range(3):
        ks = slice(g * gw, (g + 1) * gw)
        vs = slice(mw + g * gw, mw + (g + 1) * gw)
        k = _rope_cols(_dot(xb, wkv_ref[:, ks]), cos, sin, half)
        v = _dot(xb, wkv_ref[:, vs])
        kv = jnp.concatenate([k, v], axis=1)
        _residue_major(kv_refs[g], kv, dils[g], pool.pop(0) if dils[g] > 1 else None)
        q = _rope_cols(_dot(xb, wq_ref[0, :, ks]), cos, sin, half) * ATTN_SCALE
        _residue_major(q_refs[g], q, dils[g], pool.pop(0) if dils[g] > 1 else None)

        def window(g=g, kv=kv):
            if len(win_refs[g].shape) == 2:
                win_refs[g][...] = kv[tm - win_refs[g].shape[0]:, :]
            else:
                win_refs[g][0] = jnp.transpose(kv)

        if win_preds[g] is None:
            window()
        else:
            pl.when(win_preds[g])(window)
    qm = _dot(xb, wq_ref[0, :, mw:mw + MEM_WIDTH]) * ATTN_SCALE
    if mkv_ref is not None:
        m_ref[...] = _mem_attn_tile(qm.astype(BF16), mkv_ref[0, 0]).astype(m_ref.dtype)
    else:
        m_ref[...] = qm.astype(m_ref.dtype)


def _in_b_kernel(xp_ref, xs_ref, wkv_ref, wq_ref, cosp_ref, sinp_ref, coss_ref, sins_ref, mkv_ref, *refs,
                 dils, win_tiles, nt, n_p):
    outp, outs, scrs = refs[0:10], refs[10:20], refs[20:]
    i = pl.program_id(0)

    @pl.when(i < n_p)
    def _():
        _in_b_body(xp_ref, wkv_ref, wq_ref, cosp_ref, sinp_ref, mkv_ref,
                   outp[0:3], outp[3:6], outp[6], outp[7:10], scrs, dils, [None] * 3)

    @pl.when(i == n_p)
    def _():
        _in_b_body(xs_ref, wkv_ref, wq_ref, coss_ref, sins_ref, None,
                   outs[0:3], outs[3:6], outs[6], outs[7:10], (), (1, 1, 1), [None] * 3)


def _in_b(xp, xs, w_kv, w_q, tabs_p, tabs_s, memkv_t, batch, seq, dils, windows):
    tp, d = xp.shape
    ts = xs.shape[0]
    gw = GROUP_WIDTH
    tm = TOKEN_TILE
    n_p = tp // tm
    nt = seq // tm
    cl = lambda i: jnp.minimum(i, n_p - 1)
    row = lambda i: (cl(i), 0)
    tab = lambda i: (cl(i) % nt, 0)
    res = lambda i: (cl(i) // nt, 0, cl(i) % nt, 0)
    fix2 = lambda i: (0, 0)
    out_specs = ([pl.BlockSpec((1, dl, tm // dl, gw), res) for dl in dils]
                 + [pl.BlockSpec((1, dl, tm // dl, 2 * gw), res) for dl in dils]
                 + [pl.BlockSpec((tm, MEM_WIDTH), row)])
    out_shape = ([jax.ShapeDtypeStruct((batch, dl, seq // dl, gw), BF16) for dl in dils]
                 + [jax.ShapeDtypeStruct((batch, dl, seq // dl, 2 * gw), BF16) for dl in dils]
                 + [jax.ShapeDtypeStruct((tp, MEM_WIDTH), BF16)])
    win_tiles = []
    for w in windows:
        rb = min(w, tm)
        nblk = w // rb
        win_tiles.append(nblk)
        out_specs.append(pl.BlockSpec(
            (rb, 2 * gw), lambda i, nblk=nblk: ((cl(i) // nt) * nblk + jnp.maximum(cl(i) % nt - (nt - nblk), 0), 0)))
        out_shape.append(jax.ShapeDtypeStruct((batch * w, 2 * gw), F32))
    out_specs += ([pl.BlockSpec((ts, gw), fix2)] * 3 + [pl.BlockSpec((ts, 2 * gw), fix2)] * 3
                  + [pl.BlockSpec((ts, MEM_WIDTH), fix2)] + [pl.BlockSpec((1, 2 * gw, ts), lambda i: (0, 0, 0))] * 3)
    out_shape += ([jax.ShapeDtypeStruct((ts, gw), F32)] * 3 + [jax.ShapeDtypeStruct((ts, 2 * gw), F32)] * 3
                  + [jax.ShapeDtypeStruct((ts, MEM_WIDTH), F32)] + [jax.ShapeDtypeStruct((1, 2 * gw, ts), F32)] * 3)
    scratch = []
    for dl in dils:
        if dl > 1:
            scratch += [pltpu.VMEM((2 * gw // LANES, tm, LANES), F32), pltpu.VMEM((gw // LANES, tm, LANES), F32)]
    return pl.pallas_call(
        functools.partial(_in_b_kernel, dils=tuple(dils), win_tiles=tuple(win_tiles), nt=nt, n_p=n_p),
        grid=(n_p + 1,),
        in_specs=[pl.BlockSpec((tm, d), row), _full_spec(xs), _const_spec(w_kv.shape, (0, 0)),
                  _const_spec((1,) + w_q.shape[1:], (0, 0, 0)),
                  pl.BlockSpec((tm, LANES), tab), pl.BlockSpec((tm, LANES), tab),
                  _full_spec(tabs_s[0]), _full_spec(tabs_s[1]),
                  pl.BlockSpec((1, 1) + memkv_t.shape[2:], lambda i: (1, cl(i) // nt, 0, 0))],
        out_specs=out_specs,
        out_shape=out_shape,
        scratch_shapes=scratch,
        compiler_params=_params(("arbitrary",)),
        name="in_proj_b",
    )(xp, xs, w_kv, w_q, *tabs_p, *tabs_s, memkv_t)


def _dilated_kernel(q_ref, kv_ref, o_ref, lse_ref, ring_ref):
    i = pl.program_id(2)
    blk = DIL_BLOCK
    gw = GROUP_WIDTH
    nres = q_ref.shape[1]
    nsub = q_ref.shape[2] // blk
    slot = i & 1

    @pl.when(i == 0)
    def _():
        for r in range(nres):
            ring_ref[r, 1] = jnp.zeros(ring_ref.shape[2:], ring_ref.dtype)

    rows = GROUP_HEADS * blk
    qi = lax.broadcasted_iota(jnp.int32, (rows, 2 * blk), 0) & (blk - 1)
    kj = lax.broadcasted_iota(jnp.int32, (rows, 2 * blk), 1)
    delta = qi + blk - kj
    band = (delta >= 0) & (delta <= blk)
    for r in range(nres):
        cur = kv_ref[0, r]
        ring_ref[r, slot] = cur[(nsub - 1) * blk:]
        kext = jnp.concatenate([ring_ref[r, 1 - slot], cur], axis=0)
        for j in range(nsub):
            rs = slice(j * blk, (j + 1) * blk)
            q4 = _stack_heads(q_ref[0, r, rs, :], DIL_HEAD_DIM)
            kv = kext[j * blk:(j + 2) * blk]
            valid = band if j > 0 else band & ((i > 0) | (kj >= blk))
            s = jnp.where(valid, _dot_nt(q4, kv[:, :gw]), NEG_BIG)
            m = jnp.max(s, axis=-1, keepdims=True)
            e = jnp.exp(s - m)
            l = jnp.sum(e, axis=-1, keepdims=True)
            p = (e * (1.0 / l)).astype(BF16)
            full = _dot(p, kv[:, gw:])
            o_ref[0, r, rs, :] = _pick_heads(full, blk, DIL_HEAD_DIM).astype(o_ref.dtype)
            lse_ref[0, r, rs, :] = _pick_heads(jnp.broadcast_to(m + jnp.log(l), (rows, gw)), blk, DIL_HEAD_DIM)


def _dilated(q, kv, dil):
    batch, _, m, gw = q.shape
    nsub = min(DIL_BLOCKS_PER_STEP, m // DIL_BLOCK)
    nres = min(dil, DIL_BLOCKS_PER_STEP // nsub)
    rows = DIL_BLOCK * nsub
    idx = lambda b, r, i: (b, r, i, 0)
    return pl.pallas_call(
        _dilated_kernel,
        grid=(batch, dil // nres, m // rows),
        in_specs=[pl.BlockSpec((1, nres, rows, gw), idx), pl.BlockSpec((1, nres, rows, 2 * gw), idx)],
        out_specs=[pl.BlockSpec((1, nres, rows, gw), idx)] * 2,
        out_shape=[jax.ShapeDtypeStruct(q.shape, BF16), jax.ShapeDtypeStruct(q.shape, F32)],
        scratch_shapes=[pltpu.VMEM((nres, 2, DIL_BLOCK, 2 * gw), BF16)],
        compiler_params=_params(("arbitrary", "arbitrary", "arbitrary")),
        name=f"dilated_d{dil}",
    )(q, kv)


def _rope_tables(base, n):
    split = min(ROPE_SPLIT, n)
    hi = base + split * jnp.arange(n // split, dtype=F32)
    lo = jnp.arange(split, dtype=F32)
    inv = ROPE_THETA ** (-jnp.arange(0, RET_HEAD_DIM, 2, dtype=F32) / RET_HEAD_DIM)
    inv_h = inv[0::2]
    neg = lambda m: jnp.concatenate([-jnp.ones((m,), F32), jnp.ones((m,), F32)])

    def token_major(freq, sign):
        xa, xb = hi[:, None] * freq[None, :], lo[:, None] * freq[None, :]
        ca, sa, cb, sb = jnp.cos(xa), jnp.sin(xa), jnp.cos(xb), jnp.sin(xb)
        cos = ca[:, None, :] * cb[None, :, :] - sa[:, None, :] * sb[None, :, :]
        sin = (sa[:, None, :] * cb[None, :, :] + ca[:, None, :] * sb[None, :, :]) * sign
        return cos.reshape(n, -1), sin.reshape(n, -1)

    f_big = jnp.concatenate([inv, inv])
    f_half = jnp.concatenate([inv_h, inv_h])
    return (token_major(f_big, neg(RET_HEAD_DIM // 2)),
            token_major(jnp.tile(f_half, 2), jnp.tile(neg(DIL_HEAD_DIM // 2), 2)))


def _to_feature_major(x5):
    b, w = x5.shape[0], x5.shape[1]
    return jnp.transpose(x5, (0, 2, 3, 4, 1)).reshape(b, 2 * GROUP_WIDTH, w)


def _from_feature_major(xt):
    b, _, w = xt.shape
    return jnp.transpose(xt.reshape(b, 2, GROUP_HEADS, DIL_HEAD_DIM, w), (0, 4, 1, 2, 3))


def kernel(x_prompt, x_sample, mem_prompt, cache_mem_kv, state_ret, cache_win_kv_g1, cache_win_kv_g2, cache_win_kv_g3, w_in_a, w_in_b, w_out, w_kv_shared, w_mem_kv, ln_mix_g, ln_mix_b, ln_ffn_g, ln_ffn_b, w_ffn_in, w_ffn_out):
    batch, seq, d = x_prompt.shape
    dec_batch, dec_seq, _ = x_sample.shape
    n_mem = mem_prompt.shape[1]
    gw = GROUP_WIDTH
    ts = dec_batch * dec_seq
    win_caches = (cache_win_kv_g1, cache_win_kv_g2, cache_win_kv_g3)
    dils = tuple(dl for _, dl in DIL_PAIRS)

    tab_a_p, tab_b_p = _rope_tables(0.0, seq)
    tab_a_s, tab_b_s = _rope_tables(float(PAST_LEN), dec_seq)
    tab_a_s = tuple(jnp.tile(a, (dec_batch, 1)) for a in tab_a_s)
    tab_b_s = tuple(jnp.tile(a, (dec_batch, 1)) for a in tab_b_s)

    w_a = w_in_a.astype(BF16)
    w_kv = w_kv_shared.astype(BF16)
    w_q = w_in_b.astype(BF16)
    w_o = w_out.astype(BF16)
    w_fi = w_ffn_in.astype(BF16)
    w_fo = w_ffn_out.astype(BF16)
    w_mem_t = jnp.transpose(w_mem_kv, (0, 2, 1)).astype(BF16)
    lnv = lambda a: a.reshape(DEPTH, 1, d)
    ln = (lnv(ln_mix_g), lnv(ln_mix_b), lnv(ln_ffn_g), lnv(ln_ffn_b))

    memkv_p = _mem_proj(mem_prompt, w_mem_t)
    memkv_s = jnp.transpose(cache_mem_kv, (0, 1, 3, 4, 5, 2)).reshape(DEPTH, dec_batch, 2 * MEM_WIDTH, n_mem)

    xp = x_prompt.reshape(batch * seq, d)
    xs = x_sample.reshape(ts, d)

    def post(l, mix_p, mix_s, dls, shift=(), flips=()):
        return _post(l, mix_p, xp, mix_s, xs, w_o, ln[0], ln[1], w_fi, w_fo, ln[2], ln[3], dls, seq,
                     shift, dec_seq, flips)

    caches_t = [_to_feature_major(c) for c in win_caches]

    outs = _in_a(xp, xs, w_a, tab_a_p, tab_a_s, memkv_p, seq, caches_t[:2], dec_seq)
    q, k, v, g, mem_o = outs[:5]
    mix_p, state_p = _retention(q, k, v, g, batch, seq)
    q, k, v, g, qm = outs[5:10]
    shifted = list(outs[10:])
    mix_s, state_s = _retention_sample(q, k, v, g, qm, memkv_s, state_ret, dec_batch, dec_seq)
    xp, xs, *shifted_big = post(0, [mix_p, mem_o], [mix_s], None, caches_t[2:])
    shifted += shifted_big

    windows_p = tuple(min(w, seq) for w, _ in DIL_PAIRS)
    outs = _in_b(xp, xs, w_kv, w_q, tab_b_p, tab_b_s, memkv_p, batch, seq, dils, windows_p)
    q_g, kv_g, mem_o, win_p = outs[0:3], outs[3:6], outs[6], outs[7:10]
    att = [_dilated(q_g[i], kv_g[i], dils[i]) for i in range(3)]

    q_s, kv_new, qm, new_t = outs[10:13], outs[13:16], outs[16], outs[17:20]
    new_t = [a.reshape(2 * gw, ts) for a in new_t]
    mix_s, *win_s = _sample_mixer(q_s, caches_t, kv_new, new_t, qm, memkv_s, shifted, dec_batch, dec_seq)

    xp, xs, *win_p = post(1, [a[0] for a in att] + [a[1] for a in att] + [mem_o], [mix_s], dils,
                          flips=[(w, batch) for w in win_p])

    memkv_out = jnp.transpose(memkv_p.reshape(DEPTH, batch, 2, MEM_HEADS, MEM_HEAD_DIM, n_mem), (0, 1, 5, 2, 3, 4))
    return (xp.reshape(batch, seq, d), xs.reshape(dec_batch, dec_seq, d),
            state_p[None], state_s[None], memkv_out,
            _from_feature_major(win_p[0]), _from_feature_major(win_p[1]), _from_feature_major(win_p[2]),
            _from_feature_major(win_s[0]), _from_feature_major(win_s[1]), _from_feature_major(win_s[2]))
```

```python
import functools
import math

import jax
import jax.numpy as jnp
from jax import lax
from jax.experimental import pallas as pl
from jax.experimental.pallas import tpu as pltpu

F32 = jnp.float32
BF16 = jnp.bfloat16

D_MODEL = 1024
MEM_HEADS = 4
MEM_HEAD_DIM = 64
MEM_WIDTH = MEM_HEADS * MEM_HEAD_DIM
MAIN_WIDTH = D_MODEL - MEM_WIDTH
RET_HEADS = 6
RET_HEAD_DIM = MAIN_WIDTH // RET_HEADS
RET_CHUNK = 128
RET_CHUNKS_PER_STEP = 16
DIL_PAIRS = ((128, 1), (512, 4), (2048, 16))
GROUP_HEADS = 4
DIL_HEAD_DIM = 64
GROUP_WIDTH = GROUP_HEADS * DIL_HEAD_DIM
DIL_BLOCK = 128
DIL_BLOCKS_PER_STEP = 32
FFN_HIDDEN = 2816
ROPE_THETA = 10000.0
ROPE_SPLIT = 64
LN_EPS = 1e-5
DEPTH = 2
ALPHA = (2 * DEPTH) ** 0.25
PAST_LEN = 8192
NEG_BIG = -1e30
ATTN_SCALE = DIL_HEAD_DIM ** -0.5

LANES = 128
BF16_SUBLANES = 16
SAMPLE_BATCH_PER_STEP = 2
VMEM_LIMIT = 56 * 1024 * 1024
FFN_COL_CHUNK = 256
TOKEN_TILE = 512
IN_A_TILE = 512
CAST_BLOCKS = 16

LOG_G = tuple(math.log1p(-(2.0 ** (-5.0 - h))) for h in range(RET_HEADS))


def _dot(a, b):
    return jnp.dot(a, b, preferred_element_type=F32)


def _dot_nt(a, b):
    return lax.dot_general(a, b, (((1,), (1,)), ((), ())), preferred_element_type=F32)


def _dot_tn(a, b):
    return lax.dot_general(a, b, (((0,), (0,)), ((), ())), preferred_element_type=F32)


def _silu(x):
    return x / (1.0 + jnp.exp(-x))


def _normalize(z):
    zs = z - z[:, 0:1]
    m1 = jnp.mean(zs, axis=-1, keepdims=True)
    m2 = jnp.mean(zs * zs, axis=-1, keepdims=True)
    return (zs - m1) * lax.rsqrt(m2 - m1 * m1 + LN_EPS)


def _layer_norm(z, g, b):
    return _normalize(z) * g + b


def _rope_lanes(y, cos, sin_signed, half):
    if 2 * half == LANES:
        partner = pltpu.roll(y, half, 1)
    else:
        lane = lax.broadcasted_iota(jnp.int32, y.shape, 1)
        first = (lane & (2 * half - 1)) < half
        partner = jnp.where(first, pltpu.roll(y, LANES - half, 1), pltpu.roll(y, half, 1))
    return y * cos + partner * sin_signed


def _rope_cols(y, cos, sin_signed, half):
    parts = [_rope_lanes(y[:, j:j + LANES], cos, sin_signed, half) for j in range(0, y.shape[1], LANES)]
    return parts[0] if len(parts) == 1 else jnp.concatenate(parts, axis=1)


def _head_mask(shape, h, width):
    lane = lax.broadcasted_iota(jnp.int32, shape, len(shape) - 1)
    return (lane >= h * width) & (lane < (h + 1) * width)


def _stack_heads(q, width):
    zero = jnp.zeros((), q.dtype)
    parts = [jnp.where(_head_mask(q.shape, h, width), q, zero) for h in range(q.shape[1] // width)]
    return jnp.concatenate(parts, axis=0).astype(BF16)


def _pick_heads(full, t, width):
    per = LANES // width
    cols = []
    for c in range(full.shape[1] // LANES):
        out = None
        for i in range(per):
            h = c * per + i
            blk = full[h * t:(h + 1) * t, c * LANES:(c + 1) * LANES]
            out = blk if out is None else jnp.where(_head_mask(blk.shape, i, width), blk, out)
        cols.append(out)
    return cols[0] if len(cols) == 1 else jnp.concatenate(cols, axis=1)


def _const_spec(block, index):
    return pl.BlockSpec(block, lambda *_: index, pipeline_mode=pl.Buffered(1))


def _full_spec(a):
    return pl.BlockSpec(a.shape, lambda *_: (0,) * a.ndim)


def _params(sem):
    return pltpu.CompilerParams(dimension_semantics=sem, vmem_limit_bytes=VMEM_LIMIT)


def _mem_proj_kernel(m_ref, w_ref, o_ref):
    o_ref[0, 0] = _dot_nt(w_ref[0], m_ref[0].astype(BF16))


def _mem_proj(mem, w_t):
    batch, n_mem, d = mem.shape
    depth, n, _ = w_t.shape
    return pl.pallas_call(
        _mem_proj_kernel,
        grid=(depth, batch),
        in_specs=[pl.BlockSpec((1, n_mem, d), lambda l, b: (b, 0, 0)),
                  pl.BlockSpec((1, n, d), lambda l, b: (l, 0, 0))],
        out_specs=pl.BlockSpec((1, 1, n, n_mem), lambda l, b: (l, b, 0, 0)),
        out_shape=jax.ShapeDtypeStruct((depth, batch, n, n_mem), F32),
        compiler_params=_params(("arbitrary", "arbitrary")),
        name="mem_proj",
    )(mem, w_t)


def _mem_attn_tile(qm, kv_t):
    t = qm.shape[0]
    q4 = _stack_heads(qm, MEM_HEAD_DIM)
    k_t = kv_t[:MEM_WIDTH].astype(BF16)
    v_t = kv_t[MEM_WIDTH:].astype(BF16)
    s = _dot(q4, k_t)
    m = jnp.max(s, axis=-1, keepdims=True)
    e = jnp.exp(s - m)
    p = (e / jnp.sum(e, axis=-1, keepdims=True)).astype(BF16)
    return _pick_heads(_dot_nt(p, v_t), t, MEM_HEAD_DIM)


def _in_a_body(x_ref, w_ref, cos_ref, sin_ref, mkv_ref, q_ref, k_ref, v_ref, g_ref, m_ref):
    xb = x_ref[...].astype(BF16)
    cos = cos_ref[...]
    sin = sin_ref[...]
    mw = MAIN_WIDTH
    half = RET_HEAD_DIM // 2
    q = _rope_cols(_dot(xb, w_ref[0, :, 0:mw]), cos, sin, half)
    q_ref[...] = q.astype(q_ref.dtype)
    k = _rope_cols(_dot(xb, w_ref[0, :, mw:2 * mw]), cos, sin, half) * (RET_HEAD_DIM ** -0.5)
    k_ref[...] = k.astype(k_ref.dtype)
    v_ref[...] = _dot(xb, w_ref[0, :, 2 * mw:3 * mw]).astype(v_ref.dtype)
    g_ref[...] = _silu(_dot(xb, w_ref[0, :, 3 * mw:4 * mw])).astype(g_ref.dtype)
    qm = _dot(xb, w_ref[0, :, 4 * mw:4 * mw + MEM_WIDTH]) * ATTN_SCALE
    if mkv_ref is not None:
        m_ref[...] = _mem_attn_tile(qm.astype(BF16), mkv_ref[0, 0]).astype(m_ref.dtype)
    else:
        m_ref[...] = qm.astype(m_ref.dtype)


def _shift_window(cache_ref, win_ref, t):
    n = cache_ref.shape[2]
    for e in range(cache_ref.shape[0]):
        win_ref[e, :, 0:n - t] = cache_ref[e, :, t:n]
        win_ref[e, :, n - t:n] = jnp.zeros((cache_ref.shape[1], t), F32)


def _shift_specs(caches_t, n_steps):
    if not caches_t:
        return []
    nbatch = caches_t[0].shape[0]
    per = -(-nbatch // n_steps)
    assert nbatch % per == 0
    last = nbatch // per - 1
    return [pl.BlockSpec((per,) + c.shape[1:], lambda i: (jnp.minimum(i, last), 0, 0)) for c in caches_t]


def _cast_specs(jobs, n_steps):
    assert n_steps >= CAST_BLOCKS
    blk = lambda i: jnp.minimum(i, CAST_BLOCKS - 1)
    in_specs, out_specs, shapes = [], [], []
    for a, layer in jobs:
        block = (1, a.shape[1] // CAST_BLOCKS, a.shape[2])
        in_specs.append(pl.BlockSpec(block, lambda i, layer=layer: (layer, blk(i), 0)))
        out_specs.append(pl.BlockSpec(block, lambda i: (0, blk(i), 0)))
        shapes.append(jax.ShapeDtypeStruct((1,) + a.shape[1:], BF16))
    return in_specs, out_specs, shapes


def _in_a_kernel(xp_ref, xs_ref, w_ref, cosp_ref, sinp_ref, coss_ref, sins_ref, mkv_ref, *refs,
                 n_p, n_shift, n_cast, dec_seq):
    caches, refs = refs[:n_shift], refs[n_shift:]
    cast_in, refs = refs[:n_cast], refs[n_cast:]
    outs, wins, cast_out = refs[:10], refs[10:10 + n_shift], refs[10 + n_shift:]
    i = pl.program_id(0)

    @pl.when(i < n_p)
    def _():
        _in_a_body(xp_ref, w_ref, cosp_ref, sinp_ref, mkv_ref, *outs[:5])
        for c_ref, w_ref_ in zip(caches, wins):
            _shift_window(c_ref, w_ref_, dec_seq)
        for src, dst in zip(cast_in, cast_out):
            dst[...] = src[...].astype(BF16)

    @pl.when(i == n_p)
    def _():
        _in_a_body(xs_ref, w_ref, coss_ref, sins_ref, None, *outs[5:])


def _in_a(xp, xs, w_bf, tab_p, tab_s, memkv_t, seq, caches_t, dec_seq, casts):
    tp, d = xp.shape
    ts = xs.shape[0]
    tm = IN_A_TILE
    n_p = tp // tm
    nt = seq // tm
    cl = lambda i: jnp.minimum(i, n_p - 1)
    row = lambda i: (cl(i), 0)
    tab = lambda i: (cl(i) % nt, 0)
    widths = (MAIN_WIDTH,) * 4 + (MEM_WIDTH,)
    shift_specs = _shift_specs(caches_t, n_p)
    cast_in, cast_out, cast_shapes = _cast_specs(casts, n_p)
    return pl.pallas_call(
        functools.partial(_in_a_kernel, n_p=n_p, n_shift=len(caches_t), n_cast=len(casts), dec_seq=dec_seq),
        grid=(n_p + 1,),
        in_specs=[pl.BlockSpec((tm, d), row), _full_spec(xs), _const_spec((1,) + w_bf.shape[1:], (0, 0, 0)),
                  pl.BlockSpec((tm, LANES), tab), pl.BlockSpec((tm, LANES), tab),
                  _full_spec(tab_s[0]), _full_spec(tab_s[1]),
                  pl.BlockSpec((1, 1) + memkv_t.shape[2:], lambda i: (0, cl(i) // nt, 0, 0))]
                 + shift_specs + cast_in,
        out_specs=[pl.BlockSpec((tm, w), row) for w in widths]
                  + [pl.BlockSpec((ts, w), lambda i: (0, 0)) for w in widths] + shift_specs + cast_out,
        out_shape=[jax.ShapeDtypeStruct((tp, w), BF16) for w in widths]
                  + [jax.ShapeDtypeStruct((ts, w), F32) for w in widths]
                  + [jax.ShapeDtypeStruct(c.shape, F32) for c in caches_t] + cast_shapes,
        compiler_params=_params(("arbitrary",)),
        name="in_proj_a",
    )(xp, xs, w_bf, *tab_p, *tab_s, memkv_t, *caches_t, *[a for a, _ in casts])


def _retention_kernel(q_ref, k_ref, v_ref, g_ref, mix_ref, st_ref, dec_ref, rdec_ref, kdec_ref,
                      inner_ref, kv_ref):
    c = pl.program_id(1)
    cs = RET_CHUNK

    @pl.when(c == 0)
    def _():
        st_ref[...] = jnp.zeros_like(st_ref)
        row = lax.broadcasted_iota(jnp.int32, (cs, cs), 0).astype(F32)
        col = lax.broadcasted_iota(jnp.int32, (cs, cs), 1).astype(F32)
        diff = row - col
        for h in range(RET_HEADS):
            lg = LOG_G[h]
            dec_ref[h] = jnp.where(diff >= 0, jnp.exp(jnp.maximum(diff, 0.0) * lg), 0.0)
            rdec_ref[h] = jnp.exp((row + 1.0) * lg)
            kdec_ref[h] = jnp.exp((cs - 1.0 - row) * lg)

    nj = q_ref.shape[0] // cs
    for j in range(nj):
        rs = slice(j * cs, (j + 1) * cs)
        for h in range(RET_HEADS):
            hs = slice(h * RET_HEAD_DIM, (h + 1) * RET_HEAD_DIM)
            kh = k_ref[rs, hs]
            vh = v_ref[rs, hs]
            s = _dot_nt(q_ref[rs, hs], kh) * dec_ref[h]
            inner_ref[j, h] = _dot(s.astype(BF16), vh)
            kd = (kh.astype(F32) * kdec_ref[h]).astype(BF16)
            kv_ref[j, h] = _dot_tn(kd, vh)
    for j in range(nj):
        rs = slice(j * cs, (j + 1) * cs)
        for h in range(RET_HEADS):
            hs = slice(h * RET_HEAD_DIM, (h + 1) * RET_HEAD_DIM)
            st = st_ref[0, h]
            cross = _dot(q_ref[rs, hs], st.astype(BF16)) * rdec_ref[h]
            st_ref[0, h] = math.exp(cs * LOG_G[h]) * st + kv_ref[j, h]
            o = inner_ref[j, h] + cross
            mu = jnp.mean(o, axis=-1, keepdims=True)
            oc = o - mu
            var = jnp.mean(oc * oc, axis=-1, keepdims=True)
            on = oc * lax.rsqrt(var + LN_EPS)
            mix_ref[rs, hs] = (g_ref[rs, hs].astype(F32) * on).astype(mix_ref.dtype)


def _retention(q, k, v, g, batch, seq):
    t = q.shape[0]
    rows = RET_CHUNK * RET_CHUNKS_PER_STEP
    ns = seq // rows
    tok = lambda b, c: (b * ns + c, 0)
    sq = (RET_HEADS, RET_CHUNK, RET_CHUNK)
    return pl.pallas_call(
        _retention_kernel,
        grid=(batch, ns),
        in_specs=[pl.BlockSpec((rows, MAIN_WIDTH), tok)] * 4,
        out_specs=[pl.BlockSpec((rows, MAIN_WIDTH), tok),
                   pl.BlockSpec((1, RET_HEADS, RET_HEAD_DIM, RET_HEAD_DIM), lambda b, c: (b, 0, 0, 0))],
        out_shape=[jax.ShapeDtypeStruct((t, MAIN_WIDTH), BF16),
                   jax.ShapeDtypeStruct((batch, RET_HEADS, RET_HEAD_DIM, RET_HEAD_DIM), F32)],
        scratch_shapes=[pltpu.VMEM(sq, F32)] * 3 + [pltpu.VMEM((RET_CHUNKS_PER_STEP,) + sq, F32)] * 2,
        compiler_params=_params(("arbitrary", "arbitrary")),
        name="retention",
    )(q, k, v, g)


def _retention_sample_kernel(q_ref, k_ref, v_ref, g_ref, qm_ref, mkv_ref, st_ref, mix_ref, nst_ref, *, t):
    nb = q_ref.shape[0] // t
    pad_k = jnp.zeros((LANES - t, RET_HEAD_DIM), F32)
    pad_q = jnp.zeros((BF16_SUBLANES - t, RET_HEAD_DIM), F32)
    lhs = lambda x: jnp.concatenate([x, pad_q], axis=0).astype(BF16)
    row = lax.broadcasted_iota(jnp.int32, (t, LANES), 0).astype(F32)
    col = lax.broadcasted_iota(jnp.int32, (t, LANES), 1).astype(F32)
    prow = lax.broadcasted_iota(jnp.int32, (LANES, RET_HEAD_DIM), 0).astype(F32)
    diff = row - col
    for h in range(RET_HEADS):
        lg = LOG_G[h]
        hs = slice(h * RET_HEAD_DIM, (h + 1) * RET_HEAD_DIM)
        dec = jnp.where(diff >= 0, jnp.exp(jnp.maximum(diff, 0.0) * lg), 0.0)
        rdec = jnp.exp((row + 1.0) * lg)
        kdec = jnp.exp((t - 1.0 - prow) * lg)
        for bb in range(nb):
            rs = slice(bb * t, (bb + 1) * t)
            qh = lhs(q_ref[rs, hs])
            kp = jnp.concatenate([k_ref[rs, hs], pad_k], axis=0)
            vp = jnp.concatenate([v_ref[rs, hs], pad_k], axis=0).astype(BF16)
            st = st_ref[0, bb, h]
            inner = _dot(lhs(_dot_nt(qh, kp.astype(BF16))[:t] * dec), vp)[:t]
            cross = _dot(qh, st.astype(BF16))[:t] * rdec
            nst_ref[bb, h] = math.exp(t * lg) * st + _dot_tn((kp * kdec).astype(BF16), vp)
            o = inner + cross
            mu = jnp.mean(o, axis=-1, keepdims=True)
            oc = o - mu
            var = jnp.mean(oc * oc, axis=-1, keepdims=True)
            on = oc * lax.rsqrt(var + LN_EPS)
            mix_ref[rs, hs] = g_ref[rs, hs] * on
    for bb in range(nb):
        rs = slice(bb * t, (bb + 1) * t)
        mix_ref[rs, MAIN_WIDTH:] = _mem_attn_tile(qm_ref[rs, :], mkv_ref[0, bb])


def _retention_sample(q, k, v, g, qm, memkv_t, state, batch, t):
    nb = SAMPLE_BATCH_PER_STEP
    tok = lambda b: (b, 0)
    hd = (RET_HEADS, RET_HEAD_DIM, RET_HEAD_DIM)
    return pl.pallas_call(
        functools.partial(_retention_sample_kernel, t=t),
        grid=(batch // nb,),
        in_specs=[pl.BlockSpec((nb * t, MAIN_WIDTH), tok)] * 4
                 + [pl.BlockSpec((nb * t, MEM_WIDTH), tok),
                    pl.BlockSpec((1, nb) + memkv_t.shape[2:], lambda b: (0, b, 0, 0)),
                    pl.BlockSpec((1, nb) + hd, lambda b: (0, b, 0, 0, 0))],
        out_specs=[pl.BlockSpec((nb * t, D_MODEL), tok), pl.BlockSpec((nb,) + hd, lambda b: (b, 0, 0, 0))],
        out_shape=[jax.ShapeDtypeStruct((batch * t, D_MODEL), F32),
                   jax.ShapeDtypeStruct((batch,) + hd, F32)],
        compiler_params=_params(("parallel",)),
        name="retention_sample",
    )(q, k, v, g, qm, memkv_t, state)


def _dilated_sample_group(q, cache, new, window, dil):
    t = q.shape[0]
    n_buf = cache.shape[1]
    gw = GROUP_WIDTH
    q4 = _stack_heads(q, DIL_HEAD_DIM)
    rows = q4.shape[0]
    newp = jnp.concatenate([new, jnp.zeros((LANES - t, new.shape[1]), F32)], axis=0).astype(BF16)

    def masked(s, first_index):
        key = lax.broadcasted_iota(jnp.int32, s.shape, 1) + first_index
        tok = lax.broadcasted_iota(jnp.int32, s.shape, 0) & (t - 1)
        delta = n_buf + tok - key
        valid = (delta >= 0) & (delta <= window) & ((delta & (dil - 1)) == 0)
        return jnp.where(valid, s, NEG_BIG)

    s_c = masked(_dot(q4, cache[:gw].astype(BF16)), 0)
    s_n = masked(_dot_nt(q4, newp[:, :gw]), n_buf)
    m = jnp.maximum(jnp.max(s_c, axis=-1, keepdims=True), jnp.max(s_n, axis=-1, keepdims=True))
    e_c = jnp.exp(s_c - m)
    e_n = jnp.exp(s_n - m)
    l = jnp.sum(e_c, axis=-1, keepdims=True) + jnp.sum(e_n, axis=-1, keepdims=True)
    inv = 1.0 / l
    full = (_dot_nt((e_c * inv).astype(BF16), cache[gw:].astype(BF16))
            + _dot((e_n * inv).astype(BF16), newp[:, gw:]))
    o = _pick_heads(full, t, DIL_HEAD_DIM)
    lse = _pick_heads(jnp.broadcast_to(m + jnp.log(l), (rows, gw)), t, DIL_HEAD_DIM)
    return o, lse


def _sample_mixer_kernel(*refs, t):
    q_refs, cache_refs, new_refs, new_t_refs = refs[0:3], refs[3:6], refs[6:9], refs[9:12]
    qm_ref, mkv_ref = refs[12], refs[13]
    shifted_refs, mix_ref, win_refs = refs[14:17], refs[17], refs[18:21]
    gw = GROUP_WIDTH
    nb = mix_ref.shape[0] // t
    for bb in range(nb):
        rs = slice(bb * t, (bb + 1) * t)
        outs, lses = [], []
        col0 = (pl.program_id(0) * nb + bb) * t
        block = pl.multiple_of((col0 // LANES) * LANES, LANES)
        shift = LANES - t - col0 % LANES
        for g, (window, dil) in enumerate(DIL_PAIRS):
            o, lse = _dilated_sample_group(q_refs[g][rs, :], cache_refs[g][bb], new_refs[g][rs, :], window, dil)
            outs.append(o)
            lses.append(lse)
            win_refs[g][bb] = shifted_refs[g][bb]
            new_cols = pltpu.roll(new_t_refs[g][:, pl.ds(block, LANES)], shift, 1)
            win_refs[g][bb, :, LANES - t:] = new_cols[:, LANES - t:]
        m = jnp.maximum(jnp.maximum(lses[0], lses[1]), lses[2])
        es = [jnp.exp(v - m) for v in lses]
        inv = 1.0 / (es[0] + es[1] + es[2])
        for g in range(3):
            mix_ref[rs, g * gw:(g + 1) * gw] = outs[g] * (es[g] * inv)
        mix_ref[rs, MAIN_WIDTH:] = _mem_attn_tile(qm_ref[rs, :], mkv_ref[0, bb])


def _sample_mixer(q_g, caches_t, new_kv, new_kv_t, qm, memkv_t, shifted, batch, t):
    assert LANES % t == 0
    gw = GROUP_WIDTH
    nb = SAMPLE_BATCH_PER_STEP
    tok = lambda b: (b, 0)
    big = lambda b: (b, 0, 0)
    cache_specs = [pl.BlockSpec((nb,) + c.shape[1:], big) for c in caches_t]
    tail_specs = [pl.BlockSpec((nb, c.shape[1], LANES), lambda b, j=c.shape[2] // LANES - 1: (b, 0, j))
                  for c in shifted]
    first_shifted = 3 + len(caches_t) + 3 + 3 + 2
    return pl.pallas_call(
        functools.partial(_sample_mixer_kernel, t=t),
        grid=(batch // nb,),
        in_specs=[pl.BlockSpec((nb * t, gw), tok)] * 3 + cache_specs + [pl.BlockSpec((nb * t, 2 * gw), tok)] * 3
                 + [_full_spec(a) for a in new_kv_t]
                 + [pl.BlockSpec((nb * t, MEM_WIDTH), tok),
                    pl.BlockSpec((1, nb) + memkv_t.shape[2:], lambda b: (1, b, 0, 0))] + tail_specs,
        out_specs=[pl.BlockSpec((nb * t, D_MODEL), tok)] + tail_specs,
        out_shape=[jax.ShapeDtypeStruct((batch * t, D_MODEL), F32)]
                  + [jax.ShapeDtypeStruct(c.shape, F32) for c in shifted],
        input_output_aliases={first_shifted + g: 1 + g for g in range(len(shifted))},
        compiler_params=_params(("parallel",)),
        name="sample_mixer",
    )(*q_g, *caches_t, *new_kv, *new_kv_t, qm, memkv_t, *shifted)


def _natural_rows(ref, dil, scr):
    if dil == 1:
        return ref[0, 0].astype(F32)
    n = ref.shape[2]
    for r in range(dil):
        v = ref[0, r].astype(F32)
        for c in range(v.shape[1] // LANES):
            scr[c, pl.ds(r, n, stride=dil), :] = v[:, c * LANES:(c + 1) * LANES]
    return jnp.concatenate([scr[c] for c in range(scr.shape[0])], axis=1)


def _post_body(mix, x_ref, wo_ref, g1_ref, b1_ref, wi_ref, w2_ref, g2_ref, b2_ref, out_ref, act_ref):
    rows = x_ref.shape[0]
    x1 = _layer_norm(ALPHA * x_ref[...] + _dot(mix, wo_ref[0]), g1_ref[0], b1_ref[0])
    x1b = x1.astype(BF16)
    for c in range(0, FFN_HIDDEN, FFN_COL_CHUNK):
        gate = _dot(x1b, wi_ref[0, :, c:c + FFN_COL_CHUNK])
        up = _dot(x1b, wi_ref[0, :, FFN_HIDDEN + c:FFN_HIDDEN + c + FFN_COL_CHUNK])
        act_ref[0:rows, c:c + FFN_COL_CHUNK] = (_silu(gate) * up).astype(BF16)
    y = _dot(act_ref[0:rows, :], w2_ref[0])
    out_ref[...] = _layer_norm(ALPHA * x1 + y, g2_ref[0], b2_ref[0])


def _post_kernel(*refs, dils, n_mix_p, n_mix_s, n_p, n_shift, dec_seq, flip_blocks):
    n_flip = len(flip_blocks)
    mixp_refs, refs = refs[:n_mix_p], refs[n_mix_p:]
    xp_ref, refs = refs[0], refs[1:]
    mixs_refs, refs = refs[:n_mix_s], refs[n_mix_s:]
    xs_ref, refs = refs[0], refs[1:]
    weights, refs = refs[:7], refs[7:]
    caches, refs = refs[:n_shift], refs[n_shift:]
    flip_in, refs = refs[:n_flip], refs[n_flip:]
    outp_ref, outs_ref, refs = refs[0], refs[1], refs[2:]
    wins, refs = refs[:n_shift], refs[n_shift:]
    flip_out, refs = refs[:n_flip], refs[n_flip:]
    act_ref = refs[0]
    scrs = list(refs[1:])
    i = pl.program_id(0)

    @pl.when(i < n_p)
    def _():
        if dils is not None:
            o_refs, l_refs, mem_ref = mixp_refs[0:3], mixp_refs[3:6], mixp_refs[6]
            pool = list(scrs)
            nat = lambda r, d: _natural_rows(r, d, pool.pop(0) if d > 1 else None)
            os_ = [nat(o_refs[g], dils[g]) for g in range(3)]
            ls = [nat(l_refs[g], dils[g]) for g in range(3)]
            m = jnp.maximum(jnp.maximum(ls[0], ls[1]), ls[2])
            es = [jnp.exp(v - m) for v in ls]
            inv = 1.0 / (es[0] + es[1] + es[2])
            parts = [(os_[g] * (es[g] * inv)).astype(BF16) for g in range(3)]
            parts.append(mem_ref[...].astype(BF16))
        else:
            parts = [r[...].astype(BF16) for r in mixp_refs]
        mix = parts[0] if len(parts) == 1 else jnp.concatenate(parts, axis=1)
        _post_body(mix, xp_ref, *weights, outp_ref, act_ref)
        for c_ref, w_ref in zip(caches, wins):
            _shift_window(c_ref, w_ref, dec_seq)
        for src, dst in zip(flip_in, flip_out):
            dst[0] = jnp.transpose(src[...])

    @pl.when(i == n_p)
    def _():
        parts = [r[...].astype(BF16) for r in mixs_refs]
        mix = parts[0] if len(parts) == 1 else jnp.concatenate(parts, axis=1)
        _post_body(mix, xs_ref, *weights, outs_ref, act_ref)


def _post(layer, mix_p, xp, mix_s, xs, wo, g1, b1, wi, w2, g2, b2, dils, seq, caches_t=(), dec_seq=0,
          flips=()):
    tp, d = xp.shape
    tm = TOKEN_TILE
    n_p = tp // tm
    nt = seq // tm
    cl = lambda i: jnp.minimum(i, n_p - 1)
    row = lambda i: (cl(i), 0)
    if dils is None:
        mixp_specs = [pl.BlockSpec((tm, a.shape[1]), row) for a in mix_p]
        scratch = []
    else:
        res = lambda i: (cl(i) // nt, 0, cl(i) % nt, 0)
        mixp_specs = [pl.BlockSpec((1, dl, tm // dl, GROUP_WIDTH), res) for dl in dils] * 2
        mixp_specs.append(pl.BlockSpec((tm, MEM_WIDTH), row))
        scratch = [pltpu.VMEM((GROUP_WIDTH // LANES, tm, LANES), F32) for dl in dils * 2 if dl > 1]
    lsel = (0, 0, 0)
    vec = _const_spec((1, 1, d), (layer, 0, 0))
    shift_specs = _shift_specs(caches_t, n_p)
    flip_in_specs, flip_out_specs, flip_shapes, flip_blocks = [], [], [], []
    for a, nbatch in flips:
        w = a.shape[0] // nbatch
        rb = min(w, tm)
        per = w // rb
        nblk = nbatch * per
        assert nblk <= n_p
        blk = lambda i, nblk=nblk: jnp.minimum(i, nblk - 1)
        flip_in_specs.append(pl.BlockSpec((rb, a.shape[1]), lambda i, blk=blk: (blk(i), 0)))
        flip_out_specs.append(pl.BlockSpec((1, a.shape[1], rb),
                                           lambda i, blk=blk, per=per: (blk(i) // per, 0, blk(i) % per)))
        flip_shapes.append(jax.ShapeDtypeStruct((nbatch, a.shape[1], w), F32))
        flip_blocks.append(nblk)
    return pl.pallas_call(
        functools.partial(_post_kernel, dils=dils, n_mix_p=len(mix_p), n_mix_s=len(mix_s), n_p=n_p,
                          n_shift=len(caches_t), dec_seq=dec_seq, flip_blocks=tuple(flip_blocks)),
        grid=(n_p + 1,),
        in_specs=mixp_specs + [pl.BlockSpec((tm, d), row)] + [_full_spec(a) for a in mix_s] + [_full_spec(xs)]
                 + [_const_spec((1,) + wo.shape[1:], lsel), vec, vec,
                    _const_spec((1,) + wi.shape[1:], lsel), _const_spec((1,) + w2.shape[1:], lsel), vec, vec]
                 + shift_specs + flip_in_specs,
        out_specs=[pl.BlockSpec((tm, d), row), pl.BlockSpec(xs.shape, lambda i: (0, 0))] + shift_specs
                  + flip_out_specs,
        out_shape=[jax.ShapeDtypeStruct((tp, d), F32), jax.ShapeDtypeStruct(xs.shape, F32)]
                  + [jax.ShapeDtypeStruct(c.shape, F32) for c in caches_t] + flip_shapes,
        scratch_shapes=[pltpu.VMEM((tm, FFN_HIDDEN), BF16)] + scratch,
        compiler_params=_params(("arbitrary",)),
        name="post" if dils is None else "post_combine",
    )(*mix_p, xp, *mix_s, xs, wo, g1, b1, wi, w2, g2, b2, *caches_t, *[a for a, _ in flips])


def _residue_major(ref, y, dil, scr):
    if dil == 1:
        ref[...] = y.astype(ref.dtype).reshape(ref.shape)
        return
    n = ref.shape[2]
    ncol = y.shape[1] // LANES
    for c in range(ncol):
        scr[c] = y[:, c * LANES:(c + 1) * LANES]
    for r in range(dil):
        rows = [scr[c, pl.ds(r, n, stride=dil), :] for c in range(ncol)]
        ref[0, r] = jnp.concatenate(rows, axis=1).astype(ref.dtype)


def _in_b_body(x_ref, wkv_ref, wq_ref, cos_ref, sin_ref, mkv_ref,
               q_refs, kv_refs, m_ref, win_refs, scrs, dils, win_preds):
    xb = x_ref[...].astype(BF16)
    cos = cos_ref[...]
    sin = sin_ref[...]
    gw = GROUP_WIDTH
    mw = MAIN_WIDTH
    half = DIL_HEAD_DIM // 2
    tm = x_ref.shape[0]
    pool = list(scrs)
    for g in range(3):
        ks = slice(g * gw, (g + 1) * gw)
        vs = slice(mw + g * gw, mw + (g + 1) * gw)
        k = _rope_cols(_dot(xb, wkv_ref[:, ks]), cos, sin, half)
        v = _dot(xb, wkv_ref[:, vs])
        kv = jnp.concatenate([k, v], axis=1)
        _residue_major(kv_refs[g], kv, dils[g], pool.pop(0) if dils[g] > 1 else None)
        q = _rope_cols(_dot(xb, wq_ref[0, :, ks]), cos, sin, half) * ATTN_SCALE
        _residue_major(q_refs[g], q, dils[g], pool.pop(0) if dils[g] > 1 else None)

        def window(g=g, kv=kv):
            if len(win_refs[g].shape) == 2:
                win_refs[g][...] = kv[tm - win_refs[g].shape[0]:, :]
            else:
                win_refs[g][0] = jnp.transpose(kv)

        if win_preds[g] is None:
            window()
        else:
            pl.when(win_preds[g])(window)
    qm = _dot(xb, wq_ref[0, :, mw:mw + MEM_WIDTH]) * ATTN_SCALE
    if mkv_ref is not None:
        m_ref[...] = _mem_attn_tile(qm.astype(BF16), mkv_ref[0, 0]).astype(m_ref.dtype)
    else:
        m_ref[...] = qm.astype(m_ref.dtype)


def _in_b_kernel(xp_ref, xs_ref, wkv_ref, wq_ref, cosp_ref, sinp_ref, coss_ref, sins_ref, mkv_ref, *refs,
                 dils, win_tiles, nt, n_p, n_cast):
    cast_in, refs = refs[:n_cast], refs[n_cast:]
    outp, outs, cast_out, scrs = refs[0:10], refs[10:20], refs[20:20 + n_cast], refs[20 + n_cast:]
    i = pl.program_id(0)

    @pl.when(i < n_p)
    def _():
        _in_b_body(xp_ref, wkv_ref, wq_ref, cosp_ref, sinp_ref, mkv_ref,
                   outp[0:3], outp[3:6], outp[6], outp[7:10], scrs, dils, [None] * 3)
        for src, dst in zip(cast_in, cast_out):
            dst[...] = src[...].astype(BF16)

    @pl.when(i == n_p)
    def _():
        _in_b_body(xs_ref, wkv_ref, wq_ref, coss_ref, sins_ref, None,
                   outs[0:3], outs[3:6], outs[6], outs[7:10], (), (1, 1, 1), [None] * 3)


def _in_b(xp, xs, w_kv, w_q, tabs_p, tabs_s, memkv_t, batch, seq, dils, windows, casts):
    tp, d = xp.shape
    ts = xs.shape[0]
    gw = GROUP_WIDTH
    tm = TOKEN_TILE
    n_p = tp // tm
    nt = seq // tm
    cl = lambda i: jnp.minimum(i, n_p - 1)
    row = lambda i: (cl(i), 0)
    tab = lambda i: (cl(i) % nt, 0)
    res = lambda i: (cl(i) // nt, 0, cl(i) % nt, 0)
    fix2 = lambda i: (0, 0)
    out_specs = ([pl.BlockSpec((1, dl, tm // dl, gw), res) for dl in dils]
                 + [pl.BlockSpec((1, dl, tm // dl, 2 * gw), res) for dl in dils]
                 + [pl.BlockSpec((tm, MEM_WIDTH), row)])
    out_shape = ([jax.ShapeDtypeStruct((batch, dl, seq // dl, gw), BF16) for dl in dils]
                 + [jax.ShapeDtypeStruct((batch, dl, seq // dl, 2 * gw), BF16) for dl in dils]
                 + [jax.ShapeDtypeStruct((tp, MEM_WIDTH), BF16)])
    win_tiles = []
    for w in windows:
        rb = min(w, tm)
        nblk = w // rb
        win_tiles.append(nblk)
        out_specs.append(pl.BlockSpec(
            (rb, 2 * gw), lambda i, nblk=nblk: ((cl(i) // nt) * nblk + jnp.maximum(cl(i) % nt - (nt - nblk), 0), 0)))
        out_shape.append(jax.ShapeDtypeStruct((batch * w, 2 * gw), F32))
    out_specs += ([pl.BlockSpec((ts, gw), fix2)] * 3 + [pl.BlockSpec((ts, 2 * gw), fix2)] * 3
                  + [pl.BlockSpec((ts, MEM_WIDTH), fix2)] + [pl.BlockSpec((1, 2 * gw, ts), lambda i: (0, 0, 0))] * 3)
    out_shape += ([jax.ShapeDtypeStruct((ts, gw), F32)] * 3 + [jax.ShapeDtypeStruct((ts, 2 * gw), F32)] * 3
                  + [jax.ShapeDtypeStruct((ts, MEM_WIDTH), F32)] + [jax.ShapeDtypeStruct((1, 2 * gw, ts), F32)] * 3)
    scratch = []
    for dl in dils:
        if dl > 1:
            scratch += [pltpu.VMEM((2 * gw // LANES, tm, LANES), F32), pltpu.VMEM((gw // LANES, tm, LANES), F32)]
    cast_in, cast_out, cast_shapes = _cast_specs(casts, n_p)
    return pl.pallas_call(
        functools.partial(_in_b_kernel, dils=tuple(dils), win_tiles=tuple(win_tiles), nt=nt, n_p=n_p,
                          n_cast=len(casts)),
        grid=(n_p + 1,),
        in_specs=[pl.BlockSpec((tm, d), row), _full_spec(xs), _const_spec(w_kv.shape, (0, 0)),
                  _const_spec((1,) + w_q.shape[1:], (0, 0, 0)),
                  pl.BlockSpec((tm, LANES), tab), pl.BlockSpec((tm, LANES), tab),
                  _full_spec(tabs_s[0]), _full_spec(tabs_s[1]),
                  pl.BlockSpec((1, 1) + memkv_t.shape[2:], lambda i: (1, cl(i) // nt, 0, 0))] + cast_in,
        out_specs=out_specs + cast_out,
        out_shape=out_shape + cast_shapes,
        scratch_shapes=scratch,
        compiler_params=_params(("arbitrary",)),
        name="in_proj_b",
    )(xp, xs, w_kv, w_q, *tabs_p, *tabs_s, memkv_t, *[a for a, _ in casts])


def _dilated_kernel(q_ref, kv_ref, o_ref, lse_ref, ring_ref):
    i = pl.program_id(2)
    blk = DIL_BLOCK
    gw = GROUP_WIDTH
    nres = q_ref.shape[1]
    nsub = q_ref.shape[2] // blk
    slot = i & 1

    @pl.when(i == 0)
    def _():
        for r in range(nres):
            ring_ref[r, 1] = jnp.zeros(ring_ref.shape[2:], ring_ref.dtype)

    rows = GROUP_HEADS * blk
    qi = lax.broadcasted_iota(jnp.int32, (rows, 2 * blk), 0) & (blk - 1)
    kj = lax.broadcasted_iota(jnp.int32, (rows, 2 * blk), 1)
    delta = qi + blk - kj
    band = (delta >= 0) & (delta <= blk)
    for r in range(nres):
        cur = kv_ref[0, r]
        ring_ref[r, slot] = cur[(nsub - 1) * blk:]
        kext = jnp.concatenate([ring_ref[r, 1 - slot], cur], axis=0)
        for j in range(nsub):
            rs = slice(j * blk, (j + 1) * blk)
            q4 = _stack_heads(q_ref[0, r, rs, :], DIL_HEAD_DIM)
            kv = kext[j * blk:(j + 2) * blk]
            valid = band if j > 0 else band & ((i > 0) | (kj >= blk))
            s = jnp.where(valid, _dot_nt(q4, kv[:, :gw]), NEG_BIG)
            m = jnp.max(s, axis=-1, keepdims=True)
            e = jnp.exp(s - m)
            l = jnp.sum(e, axis=-1, keepdims=True)
            p = (e * (1.0 / l)).astype(BF16)
            full = _dot(p, kv[:, gw:])
            o_ref[0, r, rs, :] = _pick_heads(full, blk, DIL_HEAD_DIM).astype(o_ref.dtype)
            lse_ref[0, r, rs, :] = _pick_heads(jnp.broadcast_to(m + jnp.log(l), (rows, gw)), blk, DIL_HEAD_DIM)


def _dilated(q, kv, dil):
    batch, _, m, gw = q.shape
    nsub = min(DIL_BLOCKS_PER_STEP, m // DIL_BLOCK)
    nres = min(dil, DIL_BLOCKS_PER_STEP // nsub)
    rows = DIL_BLOCK * nsub
    idx = lambda b, r, i: (b, r, i, 0)
    return pl.pallas_call(
        _dilated_kernel,
        grid=(batch, dil // nres, m // rows),
        in_specs=[pl.BlockSpec((1, nres, rows, gw), idx), pl.BlockSpec((1, nres, rows, 2 * gw), idx)],
        out_specs=[pl.BlockSpec((1, nres, rows, gw), idx)] * 2,
        out_shape=[jax.ShapeDtypeStruct(q.shape, BF16), jax.ShapeDtypeStruct(q.shape, F32)],
        scratch_shapes=[pltpu.VMEM((nres, 2, DIL_BLOCK, 2 * gw), BF16)],
        compiler_params=_params(("arbitrary", "arbitrary", "arbitrary")),
        name=f"dilated_d{dil}",
    )(q, kv)


def _rope_tables(base, n):
    split = min(ROPE_SPLIT, n)
    hi = base + split * jnp.arange(n // split, dtype=F32)
    lo = jnp.arange(split, dtype=F32)
    inv = ROPE_THETA ** (-jnp.arange(0, RET_HEAD_DIM, 2, dtype=F32) / RET_HEAD_DIM)
    inv_h = inv[0::2]
    neg = lambda m: jnp.concatenate([-jnp.ones((m,), F32), jnp.ones((m,), F32)])

    def token_major(freq, sign):
        xa, xb = hi[:, None] * freq[None, :], lo[:, None] * freq[None, :]
        ca, sa, cb, sb = jnp.cos(xa), jnp.sin(xa), jnp.cos(xb), jnp.sin(xb)
        cos = ca[:, None, :] * cb[None, :, :] - sa[:, None, :] * sb[None, :, :]
        sin = (sa[:, None, :] * cb[None, :, :] + ca[:, None, :] * sb[None, :, :]) * sign
        return cos.reshape(n, -1), sin.reshape(n, -1)

    f_big = jnp.concatenate([inv, inv])
    f_half = jnp.concatenate([inv_h, inv_h])
    return (token_major(f_big, neg(RET_HEAD_DIM // 2)),
            token_major(jnp.tile(f_half, 2), jnp.tile(neg(DIL_HEAD_DIM // 2), 2)))


def _to_feature_major(x5):
    b, w = x5.shape[0], x5.shape[1]
    return jnp.transpose(x5, (0, 2, 3, 4, 1)).reshape(b, 2 * GROUP_WIDTH, w)


def _from_feature_major(xt):
    b, _, w = xt.shape
    return jnp.transpose(xt.reshape(b, 2, GROUP_HEADS, DIL_HEAD_DIM, w), (0, 4, 1, 2, 3))


def kernel(x_prompt, x_sample, mem_prompt, cache_mem_kv, state_ret, cache_win_kv_g1, cache_win_kv_g2, cache_win_kv_g3, w_in_a, w_in_b, w_out, w_kv_shared, w_mem_kv, ln_mix_g, ln_mix_b, ln_ffn_g, ln_ffn_b, w_ffn_in, w_ffn_out):
    batch, seq, d = x_prompt.shape
    dec_batch, dec_seq, _ = x_sample.shape
    n_mem = mem_prompt.shape[1]
    gw = GROUP_WIDTH
    ts = dec_batch * dec_seq
    win_caches = (cache_win_kv_g1, cache_win_kv_g2, cache_win_kv_g3)
    dils = tuple(dl for _, dl in DIL_PAIRS)

    tab_a_p, tab_b_p = _rope_tables(0.0, seq)
    tab_a_s, tab_b_s = _rope_tables(float(PAST_LEN), dec_seq)
    tab_a_s = tuple(jnp.tile(a, (dec_batch, 1)) for a in tab_a_s)
    tab_b_s = tuple(jnp.tile(a, (dec_batch, 1)) for a in tab_b_s)

    w_a = w_in_a.astype(BF16)
    w_mem_t = jnp.transpose(w_mem_kv, (0, 2, 1)).astype(BF16)
    lnv = lambda a: a.reshape(DEPTH, 1, d)
    ln = (lnv(ln_mix_g), lnv(ln_mix_b), lnv(ln_ffn_g), lnv(ln_ffn_b))

    memkv_p = _mem_proj(mem_prompt, w_mem_t)
    memkv_s = jnp.transpose(cache_mem_kv, (0, 1, 3, 4, 5, 2)).reshape(DEPTH, dec_batch, 2 * MEM_WIDTH, n_mem)

    xp = x_prompt.reshape(batch * seq, d)
    xs = x_sample.reshape(ts, d)

    def post(l, weights, mix_p, mix_s, dls, shift=(), flips=()):
        w_o, w_fi, w_fo = weights
        return _post(l, mix_p, xp, mix_s, xs, w_o, ln[0], ln[1], w_fi, w_fo, ln[2], ln[3], dls, seq,
                     shift, dec_seq, flips)

    layer_weights = lambda l: [(w_out, l), (w_ffn_in, l), (w_ffn_out, l)]

    caches_t = [_to_feature_major(c) for c in win_caches]

    outs = _in_a(xp, xs, w_a, tab_a_p, tab_a_s, memkv_p, seq, caches_t[:2], dec_seq,
                 layer_weights(0) + [(w_kv_shared[None], 0), (w_in_b, 0)])
    q, k, v, g, mem_o = outs[:5]
    mix_p, state_p = _retention(q, k, v, g, batch, seq)
    q, k, v, g, qm = outs[5:10]
    shifted = list(outs[10:12])
    weights0, w_kv, w_q = outs[12:15], outs[15][0], outs[16]
    mix_s, state_s = _retention_sample(q, k, v, g, qm, memkv_s, state_ret, dec_batch, dec_seq)
    xp, xs, *shifted_big = post(0, weights0, [mix_p, mem_o], [mix_s], None, caches_t[2:])
    shifted += shifted_big

    windows_p = tuple(min(w, seq) for w, _ in DIL_PAIRS)
    outs = _in_b(xp, xs, w_kv, w_q, tab_b_p, tab_b_s, memkv_p, batch, seq, dils, windows_p, layer_weights(1))
    weights1 = outs[20:23]
    q_g, kv_g, mem_o, win_p = outs[0:3], outs[3:6], outs[6], outs[7:10]
    att = [_dilated(q_g[i], kv_g[i], dils[i]) for i in range(3)]

    q_s, kv_new, qm, new_t = outs[10:13], outs[13:16], outs[16], outs[17:20]
    new_t = [a.reshape(2 * gw, ts) for a in new_t]
    mix_s, *win_s = _sample_mixer(q_s, caches_t, kv_new, new_t, qm, memkv_s, shifted, dec_batch, dec_seq)

    xp, xs, *win_p = post(1, weights1, [a[0] for a in att] + [a[1] for a in att] + [mem_o], [mix_s], dils,
                          flips=[(w, batch) for w in win_p])

    memkv_out = jnp.transpose(memkv_p.reshape(DEPTH, batch, 2, MEM_HEADS, MEM_HEAD_DIM, n_mem), (0, 1, 5, 2, 3, 4))
    return (xp.reshape(batch, seq, d), xs.reshape(dec_batch, dec_seq, d),
            state_p[None], state_s[None], memkv_out,
            _from_feature_major(win_p[0]), _from_feature_major(win_p[1]), _from_feature_major(win_p[2]),
            _from_feature_major(win_s[0]), _from_feature_major(win_s[1]), _from_feature_major(win_s[2]))
```

```python
import functools
import math

import jax
import jax.numpy as jnp
from jax import lax
from jax.experimental import pallas as pl
from jax.experimental.pallas import tpu as pltpu

F32 = jnp.float32
BF16 = jnp.bfloat16

D_MODEL = 1024
MEM_HEADS = 4
MEM_HEAD_DIM = 64
MEM_WIDTH = MEM_HEADS * MEM_HEAD_DIM
MAIN_WIDTH = D_MODEL - MEM_WIDTH
RET_HEADS = 6
RET_HEAD_DIM = MAIN_WIDTH // RET_HEADS
RET_CHUNK = 128
RET_CHUNKS_PER_STEP = 16
DIL_PAIRS = ((128, 1), (512, 4), (2048, 16))
GROUP_HEADS = 4
DIL_HEAD_DIM = 64
GROUP_WIDTH = GROUP_HEADS * DIL_HEAD_DIM
DIL_BLOCK = 128
DIL_BLOCKS_PER_STEP = 32
FFN_HIDDEN = 2816
ROPE_THETA = 10000.0
ROPE_SPLIT = 64
LN_EPS = 1e-5
DEPTH = 2
ALPHA = (2 * DEPTH) ** 0.25
PAST_LEN = 8192
NEG_BIG = -1e30
ATTN_SCALE = DIL_HEAD_DIM ** -0.5

LANES = 128
BF16_SUBLANES = 16
SAMPLE_BATCH_PER_STEP = 2
VMEM_LIMIT = 56 * 1024 * 1024
FFN_COL_CHUNK = 256
TOKEN_TILE = 512
IN_A_TILE = 1024
IN_A_VMEM_LIMIT = 62 * 1024 * 1024
CAST_BLOCKS = 16

LOG_G = tuple(math.log1p(-(2.0 ** (-5.0 - h))) for h in range(RET_HEADS))


def _dot(a, b):
    return jnp.dot(a, b, preferred_element_type=F32)


def _dot_nt(a, b):
    return lax.dot_general(a, b, (((1,), (1,)), ((), ())), preferred_element_type=F32)


def _dot_tn(a, b):
    return lax.dot_general(a, b, (((0,), (0,)), ((), ())), preferred_element_type=F32)


def _silu(x):
    return x / (1.0 + jnp.exp(-x))


def _normalize(z):
    zs = z - z[:, 0:1]
    m1 = jnp.mean(zs, axis=-1, keepdims=True)
    m2 = jnp.mean(zs * zs, axis=-1, keepdims=True)
    return (zs - m1) * lax.rsqrt(m2 - m1 * m1 + LN_EPS)


def _layer_norm(z, g, b):
    return _normalize(z) * g + b


def _rope_lanes(y, cos, sin_signed, half):
    if 2 * half == LANES:
        partner = pltpu.roll(y, half, 1)
    else:
        lane = lax.broadcasted_iota(jnp.int32, y.shape, 1)
        first = (lane & (2 * half - 1)) < half
        partner = jnp.where(first, pltpu.roll(y, LANES - half, 1), pltpu.roll(y, half, 1))
    return y * cos + partner * sin_signed


def _rope_cols(y, cos, sin_signed, half):
    parts = [_rope_lanes(y[:, j:j + LANES], cos, sin_signed, half) for j in range(0, y.shape[1], LANES)]
    return parts[0] if len(parts) == 1 else jnp.concatenate(parts, axis=1)


def _head_mask(shape, h, width):
    lane = lax.broadcasted_iota(jnp.int32, shape, len(shape) - 1)
    return (lane >= h * width) & (lane < (h + 1) * width)


def _stack_heads(q, width):
    zero = jnp.zeros((), q.dtype)
    parts = [jnp.where(_head_mask(q.shape, h, width), q, zero) for h in range(q.shape[1] // width)]
    return jnp.concatenate(parts, axis=0).astype(BF16)


def _pick_heads(full, t, width):
    per = LANES // width
    cols = []
    for c in range(full.shape[1] // LANES):
        out = None
        for i in range(per):
            h = c * per + i
            blk = full[h * t:(h + 1) * t, c * LANES:(c + 1) * LANES]
            out = blk if out is None else jnp.where(_head_mask(blk.shape, i, width), blk, out)
        cols.append(out)
    return cols[0] if len(cols) == 1 else jnp.concatenate(cols, axis=1)


def _const_spec(block, index):
    return pl.BlockSpec(block, lambda *_: index, pipeline_mode=pl.Buffered(1))


def _full_spec(a):
    return pl.BlockSpec(a.shape, lambda *_: (0,) * a.ndim)


def _params(sem, vmem=VMEM_LIMIT):
    return pltpu.CompilerParams(dimension_semantics=sem, vmem_limit_bytes=vmem)


def _mem_proj_kernel(m_ref, w_ref, o_ref):
    o_ref[0, 0] = _dot_nt(w_ref[0], m_ref[0].astype(BF16))


def _mem_proj(mem, w_t):
    batch, n_mem, d = mem.shape
    depth, n, _ = w_t.shape
    return pl.pallas_call(
        _mem_proj_kernel,
        grid=(depth, batch),
        in_specs=[pl.BlockSpec((1, n_mem, d), lambda l, b: (b, 0, 0)),
                  pl.BlockSpec((1, n, d), lambda l, b: (l, 0, 0))],
        out_specs=pl.BlockSpec((1, 1, n, n_mem), lambda l, b: (l, b, 0, 0)),
        out_shape=jax.ShapeDtypeStruct((depth, batch, n, n_mem), F32),
        compiler_params=_params(("arbitrary", "arbitrary")),
        name="mem_proj",
    )(mem, w_t)


def _mem_attn_tile(qm, kv_t):
    t = qm.shape[0]
    q4 = _stack_heads(qm, MEM_HEAD_DIM)
    k_t = kv_t[:MEM_WIDTH].astype(BF16)
    v_t = kv_t[MEM_WIDTH:].astype(BF16)
    s = _dot(q4, k_t)
    m = jnp.max(s, axis=-1, keepdims=True)
    e = jnp.exp(s - m)
    p = (e / jnp.sum(e, axis=-1, keepdims=True)).astype(BF16)
    return _pick_heads(_dot_nt(p, v_t), t, MEM_HEAD_DIM)


def _in_a_body(x_ref, w_ref, cos_ref, sin_ref, mkv_ref, q_ref, k_ref, v_ref, g_ref, m_ref):
    xb = x_ref[...].astype(BF16)
    cos = cos_ref[...]
    sin = sin_ref[...]
    mw = MAIN_WIDTH
    half = RET_HEAD_DIM // 2
    q = _rope_cols(_dot(xb, w_ref[0, :, 0:mw]), cos, sin, half)
    q_ref[...] = q.astype(q_ref.dtype)
    k = _rope_cols(_dot(xb, w_ref[0, :, mw:2 * mw]), cos, sin, half) * (RET_HEAD_DIM ** -0.5)
    k_ref[...] = k.astype(k_ref.dtype)
    v_ref[...] = _dot(xb, w_ref[0, :, 2 * mw:3 * mw]).astype(v_ref.dtype)
    g_ref[...] = _silu(_dot(xb, w_ref[0, :, 3 * mw:4 * mw])).astype(g_ref.dtype)
    qm = _dot(xb, w_ref[0, :, 4 * mw:4 * mw + MEM_WIDTH]) * ATTN_SCALE
    if mkv_ref is not None:
        m_ref[...] = _mem_attn_tile(qm.astype(BF16), mkv_ref[0, 0]).astype(m_ref.dtype)
    else:
        m_ref[...] = qm.astype(m_ref.dtype)


def _shift_window(cache_ref, win_ref, t):
    n = cache_ref.shape[2]
    for e in range(cache_ref.shape[0]):
        win_ref[e, :, 0:n - t] = cache_ref[e, :, t:n]
        win_ref[e, :, n - t:n] = jnp.zeros((cache_ref.shape[1], t), F32)


def _shift_specs(caches_t, n_steps):
    if not caches_t:
        return []
    nbatch = caches_t[0].shape[0]
    per = -(-nbatch // n_steps)
    assert nbatch % per == 0
    last = nbatch // per - 1
    return [pl.BlockSpec((per,) + c.shape[1:], lambda i: (jnp.minimum(i, last), 0, 0)) for c in caches_t]


def _cast_specs(jobs, n_steps):
    assert n_steps >= CAST_BLOCKS
    blk = lambda i: jnp.minimum(i, CAST_BLOCKS - 1)
    in_specs, out_specs, shapes = [], [], []
    for a, layer in jobs:
        block = (1, a.shape[1] // CAST_BLOCKS, a.shape[2])
        in_specs.append(pl.BlockSpec(block, lambda i, layer=layer: (layer, blk(i), 0)))
        out_specs.append(pl.BlockSpec(block, lambda i: (0, blk(i), 0)))
        shapes.append(jax.ShapeDtypeStruct((1,) + a.shape[1:], BF16))
    return in_specs, out_specs, shapes


def _in_a_kernel(xp_ref, xs_ref, w_ref, cosp_ref, sinp_ref, coss_ref, sins_ref, mkv_ref, *refs,
                 n_p, n_shift, n_cast, dec_seq):
    caches, refs = refs[:n_shift], refs[n_shift:]
    cast_in, refs = refs[:n_cast], refs[n_cast:]
    outs, wins, cast_out = refs[:10], refs[10:10 + n_shift], refs[10 + n_shift:]
    i = pl.program_id(0)

    @pl.when(i < n_p)
    def _():
        _in_a_body(xp_ref, w_ref, cosp_ref, sinp_ref, mkv_ref, *outs[:5])
        for c_ref, w_ref_ in zip(caches, wins):
            _shift_window(c_ref, w_ref_, dec_seq)
        for src, dst in zip(cast_in, cast_out):
            dst[...] = src[...].astype(BF16)

    @pl.when(i == n_p)
    def _():
        _in_a_body(xs_ref, w_ref, coss_ref, sins_ref, None, *outs[5:])


def _in_a(xp, xs, w_bf, tab_p, tab_s, memkv_t, seq, caches_t, dec_seq, casts):
    tp, d = xp.shape
    ts = xs.shape[0]
    tm = IN_A_TILE
    n_p = tp // tm
    nt = seq // tm
    cl = lambda i: jnp.minimum(i, n_p - 1)
    row = lambda i: (cl(i), 0)
    tab = lambda i: (cl(i) % nt, 0)
    widths = (MAIN_WIDTH,) * 4 + (MEM_WIDTH,)
    shift_specs = _shift_specs(caches_t, n_p)
    cast_in, cast_out, cast_shapes = _cast_specs(casts, n_p)
    return pl.pallas_call(
        functools.partial(_in_a_kernel, n_p=n_p, n_shift=len(caches_t), n_cast=len(casts), dec_seq=dec_seq),
        grid=(n_p + 1,),
        in_specs=[pl.BlockSpec((tm, d), row), _full_spec(xs), _const_spec((1,) + w_bf.shape[1:], (0, 0, 0)),
                  pl.BlockSpec((tm, LANES), tab), pl.BlockSpec((tm, LANES), tab),
                  _full_spec(tab_s[0]), _full_spec(tab_s[1]),
                  pl.BlockSpec((1, 1) + memkv_t.shape[2:], lambda i: (0, cl(i) // nt, 0, 0))]
                 + shift_specs + cast_in,
        out_specs=[pl.BlockSpec((tm, w), row) for w in widths]
                  + [pl.BlockSpec((ts, w), lambda i: (0, 0)) for w in widths] + shift_specs + cast_out,
        out_shape=[jax.ShapeDtypeStruct((tp, w), BF16) for w in widths]
                  + [jax.ShapeDtypeStruct((ts, w), F32) for w in widths]
                  + [jax.ShapeDtypeStruct(c.shape, F32) for c in caches_t] + cast_shapes,
        compiler_params=_params(("arbitrary",), IN_A_VMEM_LIMIT),
        name="in_proj_a",
    )(xp, xs, w_bf, *tab_p, *tab_s, memkv_t, *caches_t, *[a for a, _ in casts])


def _retention_kernel(q_ref, k_ref, v_ref, g_ref, mix_ref, st_ref, dec_ref, rdec_ref, kdec_ref,
                      inner_ref, kv_ref):
    c = pl.program_id(1)
    cs = RET_CHUNK

    @pl.when(c == 0)
    def _():
        st_ref[...] = jnp.zeros_like(st_ref)
        row = lax.broadcasted_iota(jnp.int32, (cs, cs), 0).astype(F32)
        col = lax.broadcasted_iota(jnp.int32, (cs, cs), 1).astype(F32)
        diff = row - col
        for h in range(RET_HEADS):
            lg = LOG_G[h]
            dec_ref[h] = jnp.where(diff >= 0, jnp.exp(jnp.maximum(diff, 0.0) * lg), 0.0)
            rdec_ref[h] = jnp.exp((row + 1.0) * lg)
            kdec_ref[h] = jnp.exp((cs - 1.0 - row) * lg)

    nj = q_ref.shape[0] // cs
    for j in range(nj):
        rs = slice(j * cs, (j + 1) * cs)
        for h in range(RET_HEADS):
            hs = slice(h * RET_HEAD_DIM, (h + 1) * RET_HEAD_DIM)
            kh = k_ref[rs, hs]
            vh = v_ref[rs, hs]
            s = _dot_nt(q_ref[rs, hs], kh) * dec_ref[h]
            inner_ref[j, h] = _dot(s.astype(BF16), vh)
            kd = (kh.astype(F32) * kdec_ref[h]).astype(BF16)
            kv_ref[j, h] = _dot_tn(kd, vh)
    for j in range(nj):
        rs = slice(j * cs, (j + 1) * cs)
        for h in range(RET_HEADS):
            hs = slice(h * RET_HEAD_DIM, (h + 1) * RET_HEAD_DIM)
            st = st_ref[0, h]
            cross = _dot(q_ref[rs, hs], st.astype(BF16)) * rdec_ref[h]
            st_ref[0, h] = math.exp(cs * LOG_G[h]) * st + kv_ref[j, h]
            o = inner_ref[j, h] + cross
            mu = jnp.mean(o, axis=-1, keepdims=True)
            oc = o - mu
            var = jnp.mean(oc * oc, axis=-1, keepdims=True)
            on = oc * lax.rsqrt(var + LN_EPS)
            mix_ref[rs, hs] = (g_ref[rs, hs].astype(F32) * on).astype(mix_ref.dtype)


def _retention(q, k, v, g, batch, seq):
    t = q.shape[0]
    rows = RET_CHUNK * RET_CHUNKS_PER_STEP
    ns = seq // rows
    tok = lambda b, c: (b * ns + c, 0)
    sq = (RET_HEADS, RET_CHUNK, RET_CHUNK)
    return pl.pallas_call(
        _retention_kernel,
        grid=(batch, ns),
        in_specs=[pl.BlockSpec((rows, MAIN_WIDTH), tok)] * 4,
        out_specs=[pl.BlockSpec((rows, MAIN_WIDTH), tok),
                   pl.BlockSpec((1, RET_HEADS, RET_HEAD_DIM, RET_HEAD_DIM), lambda b, c: (b, 0, 0, 0))],
        out_shape=[jax.ShapeDtypeStruct((t, MAIN_WIDTH), BF16),
                   jax.ShapeDtypeStruct((batch, RET_HEADS, RET_HEAD_DIM, RET_HEAD_DIM), F32)],
        scratch_shapes=[pltpu.VMEM(sq, F32)] * 3 + [pltpu.VMEM((RET_CHUNKS_PER_STEP,) + sq, F32)] * 2,
        compiler_params=_params(("arbitrary", "arbitrary")),
        name="retention",
    )(q, k, v, g)


def _retention_sample_kernel(q_ref, k_ref, v_ref, g_ref, qm_ref, mkv_ref, st_ref, mix_ref, nst_ref, *, t):
    nb = q_ref.shape[0] // t
    pad_k = jnp.zeros((LANES - t, RET_HEAD_DIM), F32)
    pad_q = jnp.zeros((BF16_SUBLANES - t, RET_HEAD_DIM), F32)
    lhs = lambda x: jnp.concatenate([x, pad_q], axis=0).astype(BF16)
    row = lax.broadcasted_iota(jnp.int32, (t, LANES), 0).astype(F32)
    col = lax.broadcasted_iota(jnp.int32, (t, LANES), 1).astype(F32)
    prow = lax.broadcasted_iota(jnp.int32, (LANES, RET_HEAD_DIM), 0).astype(F32)
    diff = row - col
    for h in range(RET_HEADS):
        lg = LOG_G[h]
        hs = slice(h * RET_HEAD_DIM, (h + 1) * RET_HEAD_DIM)
        dec = jnp.where(diff >= 0, jnp.exp(jnp.maximum(diff, 0.0) * lg), 0.0)
        rdec = jnp.exp((row + 1.0) * lg)
        kdec = jnp.exp((t - 1.0 - prow) * lg)
        for bb in range(nb):
            rs = slice(bb * t, (bb + 1) * t)
            qh = lhs(q_ref[rs, hs])
            kp = jnp.concatenate([k_ref[rs, hs], pad_k], axis=0)
            vp = jnp.concatenate([v_ref[rs, hs], pad_k], axis=0).astype(BF16)
            st = st_ref[0, bb, h]
            inner = _dot(lhs(_dot_nt(qh, kp.astype(BF16))[:t] * dec), vp)[:t]
            cross = _dot(qh, st.astype(BF16))[:t] * rdec
            nst_ref[bb, h] = math.exp(t * lg) * st + _dot_tn((kp * kdec).astype(BF16), vp)
            o = inner + cross
            mu = jnp.mean(o, axis=-1, keepdims=True)
            oc = o - mu
            var = jnp.mean(oc * oc, axis=-1, keepdims=True)
            on = oc * lax.rsqrt(var + LN_EPS)
            mix_ref[rs, hs] = g_ref[rs, hs] * on
    for bb in range(nb):
        rs = slice(bb * t, (bb + 1) * t)
        mix_ref[rs, MAIN_WIDTH:] = _mem_attn_tile(qm_ref[rs, :], mkv_ref[0, bb])


def _retention_sample(q, k, v, g, qm, memkv_t, state, batch, t):
    nb = SAMPLE_BATCH_PER_STEP
    tok = lambda b: (b, 0)
    hd = (RET_HEADS, RET_HEAD_DIM, RET_HEAD_DIM)
    return pl.pallas_call(
        functools.partial(_retention_sample_kernel, t=t),
        grid=(batch // nb,),
        in_specs=[pl.BlockSpec((nb * t, MAIN_WIDTH), tok)] * 4
                 + [pl.BlockSpec((nb * t, MEM_WIDTH), tok),
                    pl.BlockSpec((1, nb) + memkv_t.shape[2:], lambda b: (0, b, 0, 0)),
                    pl.BlockSpec((1, nb) + hd, lambda b: (0, b, 0, 0, 0))],
        out_specs=[pl.BlockSpec((nb * t, D_MODEL), tok), pl.BlockSpec((nb,) + hd, lambda b: (b, 0, 0, 0))],
        out_shape=[jax.ShapeDtypeStruct((batch * t, D_MODEL), F32),
                   jax.ShapeDtypeStruct((batch,) + hd, F32)],
        compiler_params=_params(("parallel",)),
        name="retention_sample",
    )(q, k, v, g, qm, memkv_t, state)


def _dilated_sample_group(q, cache, new, window, dil):
    t = q.shape[0]
    n_buf = cache.shape[1]
    gw = GROUP_WIDTH
    q4 = _stack_heads(q, DIL_HEAD_DIM)
    rows = q4.shape[0]
    newp = jnp.concatenate([new, jnp.zeros((LANES - t, new.shape[1]), F32)], axis=0).astype(BF16)

    def masked(s, first_index):
        key = lax.broadcasted_iota(jnp.int32, s.shape, 1) + first_index
        tok = lax.broadcasted_iota(jnp.int32, s.shape, 0) & (t - 1)
        delta = n_buf + tok - key
        valid = (delta >= 0) & (delta <= window) & ((delta & (dil - 1)) == 0)
        return jnp.where(valid, s, NEG_BIG)

    s_c = masked(_dot(q4, cache[:gw].astype(BF16)), 0)
    s_n = masked(_dot_nt(q4, newp[:, :gw]), n_buf)
    m = jnp.maximum(jnp.max(s_c, axis=-1, keepdims=True), jnp.max(s_n, axis=-1, keepdims=True))
    e_c = jnp.exp(s_c - m)
    e_n = jnp.exp(s_n - m)
    l = jnp.sum(e_c, axis=-1, keepdims=True) + jnp.sum(e_n, axis=-1, keepdims=True)
    inv = 1.0 / l
    full = (_dot_nt((e_c * inv).astype(BF16), cache[gw:].astype(BF16))
            + _dot((e_n * inv).astype(BF16), newp[:, gw:]))
    o = _pick_heads(full, t, DIL_HEAD_DIM)
    lse = _pick_heads(jnp.broadcast_to(m + jnp.log(l), (rows, gw)), t, DIL_HEAD_DIM)
    return o, lse


def _sample_mixer_kernel(*refs, t):
    q_refs, cache_refs, new_refs, new_t_refs = refs[0:3], refs[3:6], refs[6:9], refs[9:12]
    qm_ref, mkv_ref = refs[12], refs[13]
    shifted_refs, mix_ref, win_refs = refs[14:17], refs[17], refs[18:21]
    gw = GROUP_WIDTH
    nb = mix_ref.shape[0] // t
    for bb in range(nb):
        rs = slice(bb * t, (bb + 1) * t)
        outs, lses = [], []
        col0 = (pl.program_id(0) * nb + bb) * t
        block = pl.multiple_of((col0 // LANES) * LANES, LANES)
        shift = LANES - t - col0 % LANES
        for g, (window, dil) in enumerate(DIL_PAIRS):
            o, lse = _dilated_sample_group(q_refs[g][rs, :], cache_refs[g][bb], new_refs[g][rs, :], window, dil)
            outs.append(o)
            lses.append(lse)
            win_refs[g][bb] = shifted_refs[g][bb]
            new_cols = pltpu.roll(new_t_refs[g][:, pl.ds(block, LANES)], shift, 1)
            win_refs[g][bb, :, LANES - t:] = new_cols[:, LANES - t:]
        m = jnp.maximum(jnp.maximum(lses[0], lses[1]), lses[2])
        es = [jnp.exp(v - m) for v in lses]
        inv = 1.0 / (es[0] + es[1] + es[2])
        for g in range(3):
            mix_ref[rs, g * gw:(g + 1) * gw] = outs[g] * (es[g] * inv)
        mix_ref[rs, MAIN_WIDTH:] = _mem_attn_tile(qm_ref[rs, :], mkv_ref[0, bb])


def _sample_mixer(q_g, caches_t, new_kv, new_kv_t, qm, memkv_t, shifted, batch, t):
    assert LANES % t == 0
    gw = GROUP_WIDTH
    nb = SAMPLE_BATCH_PER_STEP
    tok = lambda b: (b, 0)
    big = lambda b: (b, 0, 0)
    cache_specs = [pl.BlockSpec((nb,) + c.shape[1:], big) for c in caches_t]
    tail_specs = [pl.BlockSpec((nb, c.shape[1], LANES), lambda b, j=c.shape[2] // LANES - 1: (b, 0, j))
                  for c in shifted]
    first_shifted = 3 + len(caches_t) + 3 + 3 + 2
    return pl.pallas_call(
        functools.partial(_sample_mixer_kernel, t=t),
        grid=(batch // nb,),
        in_specs=[pl.BlockSpec((nb * t, gw), tok)] * 3 + cache_specs + [pl.BlockSpec((nb * t, 2 * gw), tok)] * 3
                 + [_full_spec(a) for a in new_kv_t]
                 + [pl.BlockSpec((nb * t, MEM_WIDTH), tok),
                    pl.BlockSpec((1, nb) + memkv_t.shape[2:], lambda b: (1, b, 0, 0))] + tail_specs,
        out_specs=[pl.BlockSpec((nb * t, D_MODEL), tok)] + tail_specs,
        out_shape=[jax.ShapeDtypeStruct((batch * t, D_MODEL), F32)]
                  + [jax.ShapeDtypeStruct(c.shape, F32) for c in shifted],
        input_output_aliases={first_shifted + g: 1 + g for g in range(len(shifted))},
        compiler_params=_params(("parallel",)),
        name="sample_mixer",
    )(*q_g, *caches_t, *new_kv, *new_kv_t, qm, memkv_t, *shifted)


def _natural_rows(ref, dil, scr):
    if dil == 1:
        return ref[0, 0].astype(F32)
    n = ref.shape[2]
    for r in range(dil):
        v = ref[0, r].astype(F32)
        for c in range(v.shape[1] // LANES):
            scr[c, pl.ds(r, n, stride=dil), :] = v[:, c * LANES:(c + 1) * LANES]
    return jnp.concatenate([scr[c] for c in range(scr.shape[0])], axis=1)


def _post_body(mix, x_ref, wo_ref, g1_ref, b1_ref, wi_ref, w2_ref, g2_ref, b2_ref, out_ref, act_ref):
    rows = x_ref.shape[0]
    x1 = _layer_norm(ALPHA * x_ref[...] + _dot(mix, wo_ref[0]), g1_ref[0], b1_ref[0])
    x1b = x1.astype(BF16)
    for c in range(0, FFN_HIDDEN, FFN_COL_CHUNK):
        gate = _dot(x1b, wi_ref[0, :, c:c + FFN_COL_CHUNK])
        up = _dot(x1b, wi_ref[0, :, FFN_HIDDEN + c:FFN_HIDDEN + c + FFN_COL_CHUNK])
        act_ref[0:rows, c:c + FFN_COL_CHUNK] = (_silu(gate) * up).astype(BF16)
    y = _dot(act_ref[0:rows, :], w2_ref[0])
    out_ref[...] = _layer_norm(ALPHA * x1 + y, g2_ref[0], b2_ref[0])


def _post_kernel(*refs, dils, n_mix_p, n_mix_s, n_p, n_shift, dec_seq, flip_blocks):
    n_flip = len(flip_blocks)
    mixp_refs, refs = refs[:n_mix_p], refs[n_mix_p:]
    xp_ref, refs = refs[0], refs[1:]
    mixs_refs, refs = refs[:n_mix_s], refs[n_mix_s:]
    xs_ref, refs = refs[0], refs[1:]
    weights, refs = refs[:7], refs[7:]
    caches, refs = refs[:n_shift], refs[n_shift:]
    flip_in, refs = refs[:n_flip], refs[n_flip:]
    outp_ref, outs_ref, refs = refs[0], refs[1], refs[2:]
    wins, refs = refs[:n_shift], refs[n_shift:]
    flip_out, refs = refs[:n_flip], refs[n_flip:]
    act_ref = refs[0]
    scrs = list(refs[1:])
    i = pl.program_id(0)

    @pl.when(i < n_p)
    def _():
        if dils is not None:
            o_refs, l_refs, mem_ref = mixp_refs[0:3], mixp_refs[3:6], mixp_refs[6]
            pool = list(scrs)
            nat = lambda r, d: _natural_rows(r, d, pool.pop(0) if d > 1 else None)
            os_ = [nat(o_refs[g], dils[g]) for g in range(3)]
            ls = [nat(l_refs[g], dils[g]) for g in range(3)]
            m = jnp.maximum(jnp.maximum(ls[0], ls[1]), ls[2])
            es = [jnp.exp(v - m) for v in ls]
            inv = 1.0 / (es[0] + es[1] + es[2])
            parts = [(os_[g] * (es[g] * inv)).astype(BF16) for g in range(3)]
            parts.append(mem_ref[...].astype(BF16))
        else:
            parts = [r[...].astype(BF16) for r in mixp_refs]
        mix = parts[0] if len(parts) == 1 else jnp.concatenate(parts, axis=1)
        _post_body(mix, xp_ref, *weights, outp_ref, act_ref)
        for c_ref, w_ref in zip(caches, wins):
            _shift_window(c_ref, w_ref, dec_seq)
        for src, dst in zip(flip_in, flip_out):
            dst[0] = jnp.transpose(src[...])

    @pl.when(i == n_p)
    def _():
        parts = [r[...].astype(BF16) for r in mixs_refs]
        mix = parts[0] if len(parts) == 1 else jnp.concatenate(parts, axis=1)
        _post_body(mix, xs_ref, *weights, outs_ref, act_ref)


def _post(layer, mix_p, xp, mix_s, xs, wo, g1, b1, wi, w2, g2, b2, dils, seq, caches_t=(), dec_seq=0,
          flips=()):
    tp, d = xp.shape
    tm = TOKEN_TILE
    n_p = tp // tm
    nt = seq // tm
    cl = lambda i: jnp.minimum(i, n_p - 1)
    row = lambda i: (cl(i), 0)
    if dils is None:
        mixp_specs = [pl.BlockSpec((tm, a.shape[1]), row) for a in mix_p]
        scratch = []
    else:
        res = lambda i: (cl(i) // nt, 0, cl(i) % nt, 0)
        mixp_specs = [pl.BlockSpec((1, dl, tm // dl, GROUP_WIDTH), res) for dl in dils] * 2
        mixp_specs.append(pl.BlockSpec((tm, MEM_WIDTH), row))
        scratch = [pltpu.VMEM((GROUP_WIDTH // LANES, tm, LANES), F32) for dl in dils * 2 if dl > 1]
    lsel = (0, 0, 0)
    vec = _const_spec((1, 1, d), (layer, 0, 0))
    shift_specs = _shift_specs(caches_t, n_p)
    flip_in_specs, flip_out_specs, flip_shapes, flip_blocks = [], [], [], []
    for a, nbatch in flips:
        w = a.shape[0] // nbatch
        rb = min(w, tm)
        per = w // rb
        nblk = nbatch * per
        assert nblk <= n_p
        blk = lambda i, nblk=nblk: jnp.minimum(i, nblk - 1)
        flip_in_specs.append(pl.BlockSpec((rb, a.shape[1]), lambda i, blk=blk: (blk(i), 0)))
        flip_out_specs.append(pl.BlockSpec((1, a.shape[1], rb),
                                           lambda i, blk=blk, per=per: (blk(i) // per, 0, blk(i) % per)))
        flip_shapes.append(jax.ShapeDtypeStruct((nbatch, a.shape[1], w), F32))
        flip_blocks.append(nblk)
    return pl.pallas_call(
        functools.partial(_post_kernel, dils=dils, n_mix_p=len(mix_p), n_mix_s=len(mix_s), n_p=n_p,
                          n_shift=len(caches_t), dec_seq=dec_seq, flip_blocks=tuple(flip_blocks)),
        grid=(n_p + 1,),
        in_specs=mixp_specs + [pl.BlockSpec((tm, d), row)] + [_full_spec(a) for a in mix_s] + [_full_spec(xs)]
                 + [_const_spec((1,) + wo.shape[1:], lsel), vec, vec,
                    _const_spec((1,) + wi.shape[1:], lsel), _const_spec((1,) + w2.shape[1:], lsel), vec, vec]
                 + shift_specs + flip_in_specs,
        out_specs=[pl.BlockSpec((tm, d), row), pl.BlockSpec(xs.shape, lambda i: (0, 0))] + shift_specs
                  + flip_out_specs,
        out_shape=[jax.ShapeDtypeStruct((tp, d), F32), jax.ShapeDtypeStruct(xs.shape, F32)]
                  + [jax.ShapeDtypeStruct(c.shape, F32) for c in caches_t] + flip_shapes,
        scratch_shapes=[pltpu.VMEM((tm, FFN_HIDDEN), BF16)] + scratch,
        compiler_params=_params(("arbitrary",)),
        name="post" if dils is None else "post_combine",
    )(*mix_p, xp, *mix_s, xs, wo, g1, b1, wi, w2, g2, b2, *caches_t, *[a for a, _ in flips])


def _residue_major(ref, y, dil, scr):
    if dil == 1:
        ref[...] = y.astype(ref.dtype).reshape(ref.shape)
        return
    n = ref.shape[2]
    ncol = y.shape[1] // LANES
    for c in range(ncol):
        scr[c] = y[:, c * LANES:(c + 1) * LANES]
    for r in range(dil):
        rows = [scr[c, pl.ds(r, n, stride=dil), :] for c in range(ncol)]
        ref[0, r] = jnp.concatenate(rows, axis=1).astype(ref.dtype)


def _in_b_body(x_ref, wkv_ref, wq_ref, cos_ref, sin_ref, mkv_ref,
               q_refs, kv_refs, m_ref, win_refs, scrs, dils, win_preds):
    xb = x_ref[...].astype(BF16)
    cos = cos_ref[...]
    sin = sin_ref[...]
    gw = GROUP_WIDTH
    mw = MAIN_WIDTH
    half = DIL_HEAD_DIM // 2
    tm = x_ref.shape[0]
    pool = list(scrs)
    for g in range(3):
        ks = slice(g * gw, (g + 1) * gw)
        vs = slice(mw + g * gw, mw + (g + 1) * gw)
        k = _rope_cols(_dot(xb, wkv_ref[:, ks]), cos, sin, half)
        v = _dot(xb, wkv_ref[:, vs])
        kv = jnp.concatenate([k, v], axis=1)
        _residue_major(kv_refs[g], kv, dils[g], pool.pop(0) if dils[g] > 1 else None)
        q = _rope_cols(_dot(xb, wq_ref[0, :, ks]), cos, sin, half) * ATTN_SCALE
        _residue_major(q_refs[g], q, dils[g], pool.pop(0) if dils[g] > 1 else None)

        def window(g=g, kv=kv):
            if len(win_refs[g].shape) == 2:
                win_refs[g][...] = kv[tm - win_refs[g].shape[0]:, :]
            else:
                win_refs[g][0] = jnp.transpose(kv)

        if win_preds[g] is None:
            window()
        else:
            pl.when(win_preds[g])(window)
    qm = _dot(xb, wq_ref[0, :, mw:mw + MEM_WIDTH]) * ATTN_SCALE
    if mkv_ref is not None:
        m_ref[...] = _mem_attn_tile(qm.astype(BF16), mkv_ref[0, 0]).astype(m_ref.dtype)
    else:
        m_ref[...] = qm.astype(m_ref.dtype)


def _in_b_kernel(xp_ref, xs_ref, wkv_ref, wq_ref, cosp_ref, sinp_ref, coss_ref, sins_ref, mkv_ref, *refs,
                 dils, win_tiles, nt, n_p, n_cast):
    cast_in, refs = refs[:n_cast], refs[n_cast:]
    outp, outs, cast_out, scrs = refs[0:10], refs[10:20], refs[20:20 + n_cast], refs[20 + n_cast:]
    i = pl.program_id(0)

    @pl.when(i < n_p)
    def _():
        _in_b_body(xp_ref, wkv_ref, wq_ref, cosp_ref, sinp_ref, mkv_ref,
                   outp[0:3], outp[3:6], outp[6], outp[7:10], scrs, dils, [None] * 3)
        for src, dst in zip(cast_in, cast_out):
            dst[...] = src[...].astype(BF16)

    @pl.when(i == n_p)
    def _():
        _in_b_body(xs_ref, wkv_ref, wq_ref, coss_ref, sins_ref, None,
                   outs[0:3], outs[3:6], outs[6], outs[7:10], (), (1, 1, 1), [None] * 3)


def _in_b(xp, xs, w_kv, w_q, tabs_p, tabs_s, memkv_t, batch, seq, dils, windows, casts):
    tp, d = xp.shape
    ts = xs.shape[0]
    gw = GROUP_WIDTH
    tm = TOKEN_TILE
    n_p = tp // tm
    nt = seq // tm
    cl = lambda i: jnp.minimum(i, n_p - 1)
    row = lambda i: (cl(i), 0)
    tab = lambda i: (cl(i) % nt, 0)
    res = lambda i: (cl(i) // nt, 0, cl(i) % nt, 0)
    fix2 = lambda i: (0, 0)
    out_specs = ([pl.BlockSpec((1, dl, tm // dl, gw), res) for dl in dils]
                 + [pl.BlockSpec((1, dl, tm // dl, 2 * gw), res) for dl in dils]
                 + [pl.BlockSpec((tm, MEM_WIDTH), row)])
    out_shape = ([jax.ShapeDtypeStruct((batch, dl, seq // dl, gw), BF16) for dl in dils]
                 + [jax.ShapeDtypeStruct((batch, dl, seq // dl, 2 * gw), BF16) for dl in dils]
                 + [jax.ShapeDtypeStruct((tp, MEM_WIDTH), BF16)])
    win_tiles = []
    for w in windows:
        rb = min(w, tm)
        nblk = w // rb
        win_tiles.append(nblk)
        out_specs.append(pl.BlockSpec(
            (rb, 2 * gw), lambda i, nblk=nblk: ((cl(i) // nt) * nblk + jnp.maximum(cl(i) % nt - (nt - nblk), 0), 0)))
        out_shape.append(jax.ShapeDtypeStruct((batch * w, 2 * gw), F32))
    out_specs += ([pl.BlockSpec((ts, gw), fix2)] * 3 + [pl.BlockSpec((ts, 2 * gw), fix2)] * 3
                  + [pl.BlockSpec((ts, MEM_WIDTH), fix2)] + [pl.BlockSpec((1, 2 * gw, ts), lambda i: (0, 0, 0))] * 3)
    out_shape += ([jax.ShapeDtypeStruct((ts, gw), F32)] * 3 + [jax.ShapeDtypeStruct((ts, 2 * gw), F32)] * 3
                  + [jax.ShapeDtypeStruct((ts, MEM_WIDTH), F32)] + [jax.ShapeDtypeStruct((1, 2 * gw, ts), F32)] * 3)
    scratch = []
    for dl in dils:
        if dl > 1:
            scratch += [pltpu.VMEM((2 * gw // LANES, tm, LANES), F32), pltpu.VMEM((gw // LANES, tm, LANES), F32)]
    cast_in, cast_out, cast_shapes = _cast_specs(casts, n_p)
    return pl.pallas_call(
        functools.partial(_in_b_kernel, dils=tuple(dils), win_tiles=tuple(win_tiles), nt=nt, n_p=n_p,
                          n_cast=len(casts)),
        grid=(n_p + 1,),
        in_specs=[pl.BlockSpec((tm, d), row), _full_spec(xs), _const_spec(w_kv.shape, (0, 0)),
                  _const_spec((1,) + w_q.shape[1:], (0, 0, 0)),
                  pl.BlockSpec((tm, LANES), tab), pl.BlockSpec((tm, LANES), tab),
                  _full_spec(tabs_s[0]), _full_spec(tabs_s[1]),
                  pl.BlockSpec((1, 1) + memkv_t.shape[2:], lambda i: (1, cl(i) // nt, 0, 0))] + cast_in,
        out_specs=out_specs + cast_out,
        out_shape=out_shape + cast_shapes,
        scratch_shapes=scratch,
        compiler_params=_params(("arbitrary",)),
        name="in_proj_b",
    )(xp, xs, w_kv, w_q, *tabs_p, *tabs_s, memkv_t, *[a for a, _ in casts])


def _dilated_kernel(q_ref, kv_ref, o_ref, lse_ref, ring_ref):
    i = pl.program_id(2)
    blk = DIL_BLOCK
    gw = GROUP_WIDTH
    nres = q_ref.shape[1]
    nsub = q_ref.shape[2] // blk
    slot = i & 1

    @pl.when(i == 0)
    def _():
        for r in range(nres):
            ring_ref[r, 1] = jnp.zeros(ring_ref.shape[2:], ring_ref.dtype)

    rows = GROUP_HEADS * blk
    qi = lax.broadcasted_iota(jnp.int32, (rows, 2 * blk), 0) & (blk - 1)
    kj = lax.broadcasted_iota(jnp.int32, (rows, 2 * blk), 1)
    delta = qi + blk - kj
    band = (delta >= 0) & (delta <= blk)
    for r in range(nres):
        cur = kv_ref[0, r]
        ring_ref[r, slot] = cur[(nsub - 1) * blk:]
        kext = jnp.concatenate([ring_ref[r, 1 - slot], cur], axis=0)
        for j in range(nsub):
            rs = slice(j * blk, (j + 1) * blk)
            q4 = _stack_heads(q_ref[0, r, rs, :], DIL_HEAD_DIM)
            kv = kext[j * blk:(j + 2) * blk]
            valid = band if j > 0 else band & ((i > 0) | (kj >= blk))
            s = jnp.where(valid, _dot_nt(q4, kv[:, :gw]), NEG_BIG)
            m = jnp.max(s, axis=-1, keepdims=True)
            e = jnp.exp(s - m)
            l = jnp.sum(e, axis=-1, keepdims=True)
            p = (e * (1.0 / l)).astype(BF16)
            full = _dot(p, kv[:, gw:])
            o_ref[0, r, rs, :] = _pick_heads(full, blk, DIL_HEAD_DIM).astype(o_ref.dtype)
            lse_ref[0, r, rs, :] = _pick_heads(jnp.broadcast_to(m + jnp.log(l), (rows, gw)), blk, DIL_HEAD_DIM)


def _dilated(q, kv, dil):
    batch, _, m, gw = q.shape
    nsub = min(DIL_BLOCKS_PER_STEP, m // DIL_BLOCK)
    nres = min(dil, DIL_BLOCKS_PER_STEP // nsub)
    rows = DIL_BLOCK * nsub
    idx = lambda b, r, i: (b, r, i, 0)
    return pl.pallas_call(
        _dilated_kernel,
        grid=(batch, dil // nres, m // rows),
        in_specs=[pl.BlockSpec((1, nres, rows, gw), idx), pl.BlockSpec((1, nres, rows, 2 * gw), idx)],
        out_specs=[pl.BlockSpec((1, nres, rows, gw), idx)] * 2,
        out_shape=[jax.ShapeDtypeStruct(q.shape, BF16), jax.ShapeDtypeStruct(q.shape, F32)],
        scratch_shapes=[pltpu.VMEM((nres, 2, DIL_BLOCK, 2 * gw), BF16)],
        compiler_params=_params(("arbitrary", "arbitrary", "arbitrary")),
        name=f"dilated_d{dil}",
    )(q, kv)


def _rope_tables(base, n):
    split = min(ROPE_SPLIT, n)
    hi = base + split * jnp.arange(n // split, dtype=F32)
    lo = jnp.arange(split, dtype=F32)
    inv = ROPE_THETA ** (-jnp.arange(0, RET_HEAD_DIM, 2, dtype=F32) / RET_HEAD_DIM)
    inv_h = inv[0::2]
    neg = lambda m: jnp.concatenate([-jnp.ones((m,), F32), jnp.ones((m,), F32)])

    def token_major(freq, sign):
        xa, xb = hi[:, None] * freq[None, :], lo[:, None] * freq[None, :]
        ca, sa, cb, sb = jnp.cos(xa), jnp.sin(xa), jnp.cos(xb), jnp.sin(xb)
        cos = ca[:, None, :] * cb[None, :, :] - sa[:, None, :] * sb[None, :, :]
        sin = (sa[:, None, :] * cb[None, :, :] + ca[:, None, :] * sb[None, :, :]) * sign
        return cos.reshape(n, -1), sin.reshape(n, -1)

    f_big = jnp.concatenate([inv, inv])
    f_half = jnp.concatenate([inv_h, inv_h])
    return (token_major(f_big, neg(RET_HEAD_DIM // 2)),
            token_major(jnp.tile(f_half, 2), jnp.tile(neg(DIL_HEAD_DIM // 2), 2)))


def _to_feature_major(x5):
    b, w = x5.shape[0], x5.shape[1]
    return jnp.transpose(x5, (0, 2, 3, 4, 1)).reshape(b, 2 * GROUP_WIDTH, w)


def _from_feature_major(xt):
    b, _, w = xt.shape
    return jnp.transpose(xt.reshape(b, 2, GROUP_HEADS, DIL_HEAD_DIM, w), (0, 4, 1, 2, 3))


def kernel(x_prompt, x_sample, mem_prompt, cache_mem_kv, state_ret, cache_win_kv_g1, cache_win_kv_g2, cache_win_kv_g3, w_in_a, w_in_b, w_out, w_kv_shared, w_mem_kv, ln_mix_g, ln_mix_b, ln_ffn_g, ln_ffn_b, w_ffn_in, w_ffn_out):
    batch, seq, d = x_prompt.shape
    dec_batch, dec_seq, _ = x_sample.shape
    n_mem = mem_prompt.shape[1]
    gw = GROUP_WIDTH
    ts = dec_batch * dec_seq
    win_caches = (cache_win_kv_g1, cache_win_kv_g2, cache_win_kv_g3)
    dils = tuple(dl for _, dl in DIL_PAIRS)

    tab_a_p, tab_b_p = _rope_tables(0.0, seq)
    tab_a_s, tab_b_s = _rope_tables(float(PAST_LEN), dec_seq)
    tab_a_s = tuple(jnp.tile(a, (dec_batch, 1)) for a in tab_a_s)
    tab_b_s = tuple(jnp.tile(a, (dec_batch, 1)) for a in tab_b_s)

    w_a = w_in_a.astype(BF16)
    w_mem_t = jnp.transpose(w_mem_kv, (0, 2, 1)).astype(BF16)
    lnv = lambda a: a.reshape(DEPTH, 1, d)
    ln = (lnv(ln_mix_g), lnv(ln_mix_b), lnv(ln_ffn_g), lnv(ln_ffn_b))

    memkv_p = _mem_proj(mem_prompt, w_mem_t)
    memkv_s = jnp.transpose(cache_mem_kv, (0, 1, 3, 4, 5, 2)).reshape(DEPTH, dec_batch, 2 * MEM_WIDTH, n_mem)

    xp = x_prompt.reshape(batch * seq, d)
    xs = x_sample.reshape(ts, d)

    def post(l, weights, mix_p, mix_s, dls, shift=(), flips=()):
        w_o, w_fi, w_fo = weights
        return _post(l, mix_p, xp, mix_s, xs, w_o, ln[0], ln[1], w_fi, w_fo, ln[2], ln[3], dls, seq,
                     shift, dec_seq, flips)

    layer_weights = lambda l: [(w_out, l), (w_ffn_in, l), (w_ffn_out, l)]

    caches_t = [_to_feature_major(c) for c in win_caches]

    outs = _in_a(xp, xs, w_a, tab_a_p, tab_a_s, memkv_p, seq, caches_t[:2], dec_seq,
                 layer_weights(0) + [(w_kv_shared[None], 0), (w_in_b, 0)])
    q, k, v, g, mem_o = outs[:5]
    mix_p, state_p = _retention(q, k, v, g, batch, seq)
    q, k, v, g, qm = outs[5:10]
    shifted = list(outs[10:12])
    weights0, w_kv, w_q = outs[12:15], outs[15][0], outs[16]
    mix_s, state_s = _retention_sample(q, k, v, g, qm, memkv_s, state_ret, dec_batch, dec_seq)
    xp, xs, *shifted_big = post(0, weights0, [mix_p, mem_o], [mix_s], None, caches_t[2:])
    shifted += shifted_big

    windows_p = tuple(min(w, seq) for w, _ in DIL_PAIRS)
    outs = _in_b(xp, xs, w_kv, w_q, tab_b_p, tab_b_s, memkv_p, batch, seq, dils, windows_p, layer_weights(1))
    weights1 = outs[20:23]
    q_g, kv_g, mem_o, win_p = outs[0:3], outs[3:6], outs[6], outs[7:10]
    att = [_dilated(q_g[i], kv_g[i], dils[i]) for i in range(3)]

    q_s, kv_new, qm, new_t = outs[10:13], outs[13:16], outs[16], outs[17:20]
    new_t = [a.reshape(2 * gw, ts) for a in new_t]
    mix_s, *win_s = _sample_mixer(q_s, caches_t, kv_new, new_t, qm, memkv_s, shifted, dec_batch, dec_seq)

    xp, xs, *win_p = post(1, weights1, [a[0] for a in att] + [a[1] for a in att] + [mem_o], [mix_s], dils,
                          flips=[(w, batch) for w in win_p])

    memkv_out = jnp.transpose(memkv_p.reshape(DEPTH, batch, 2, MEM_HEADS, MEM_HEAD_DIM, n_mem), (0, 1, 5, 2, 3, 4))
    return (xp.reshape(batch, seq, d), xs.reshape(dec_batch, dec_seq, d),
            state_p[None], state_s[None], memkv_out,
            _from_feature_major(win_p[0]), _from_feature_major(win_p[1]), _from_feature_major(win_p[2]),
            _from_feature_major(win_s[0]), _from_feature_major(win_s[1]), _from_feature_major(win_s[2]))
```
